```python
import math
import jax
import jax.numpy as jnp
from jax import lax
import numpy as np

D_MODEL = 1024
BATCH = 8
SEQ = 2048
DEPTH = 2

N_MIXERS = 2
N_CONV_LAYERS = (DEPTH + 1) // 2
N_DELTA_LAYERS = DEPTH // 2

RMS_EPS = 1e-6
LN_EPS = 1e-5

CONV_WIDTH = 31

N_HEADS = 8
HEAD_DIM = D_MODEL // N_HEADS
KEY_DIM = N_HEADS * HEAD_DIM
VAL_DIM = N_HEADS * HEAD_DIM
QKV_DIM = 2 * KEY_DIM + VAL_DIM
N_DIRS = 2
IN_PROJ_DIM = QKV_DIM + VAL_DIM + 2 * N_DIRS * N_HEADS
SHORT_CONV = 5
CHUNK = 64

D_FF = ((8 * D_MODEL // 3 + 127) // 128) * 128
N_EXPERTS = 8
TOP_K = 2
D_FF_EXPERT = 7 * D_MODEL // 2

kernel_name = "hybrid_conformer_gdn_moe_encoder"


def rmsnorm(x, g):
    xf = x.astype(jnp.float32)
    y = xf * lax.rsqrt(jnp.mean(xf * xf, axis=-1, keepdims=True) + RMS_EPS)
    return (y * g.astype(jnp.float32)).astype(x.dtype)


def layernorm(x, g, b):
    xf = x.astype(jnp.float32)
    mu = jnp.mean(xf, axis=-1, keepdims=True)
    xc = xf - mu
    y = xc * lax.rsqrt(jnp.mean(xc * xc, axis=-1, keepdims=True) + LN_EPS)
    return (y * g.astype(jnp.float32) + b.astype(jnp.float32)).astype(x.dtype)


def depthwise_conv(x, w):
    width, chans = w.shape
    return lax.conv_general_dilated(
        x, w[:, None, :].astype(x.dtype), window_strides=(1,),
        padding=[(width // 2, width // 2)],
        dimension_numbers=("NWC", "WIO", "NWC"),
        feature_group_count=chans)


def conformer_conv_mixer(h, pw1_w, pw1_b, dw_w, dw_b, ln_g, ln_b, pw2_w, pw2_b):
    u = h @ pw1_w + pw1_b
    a, b = jnp.split(u, 2, axis=-1)
    u = a * jax.nn.sigmoid(b)
    u = depthwise_conv(u, dw_w) + dw_b
    u = jax.nn.silu(layernorm(u, ln_g, ln_b))
    return u @ pw2_w + pw2_b


def l2norm(t):
    return t * lax.rsqrt(jnp.sum(t * t, axis=-1, keepdims=True) + 1e-6)


def chunk_gated_delta(q, k, v, log_a, beta):
    B, H, S, dk = q.shape
    dv = v.shape[-1]
    C = CHUNK
    N = S // C
    q = q.reshape(B, H, N, C, dk)
    k = k.reshape(B, H, N, C, dk)
    v = v.reshape(B, H, N, C, dv)
    g = jnp.cumsum(log_a.reshape(B, H, N, C), axis=-1)
    bt = beta.reshape(B, H, N, C)[..., None]
    kb = k * bt
    vb = v * bt
    incl = jnp.tril(jnp.ones((C, C), dtype=bool))
    strict = jnp.tril(jnp.ones((C, C), dtype=bool), -1)
    diff = g[..., :, None] - g[..., None, :]
    decay = jnp.where(incl, jnp.exp(jnp.where(incl, diff, 0.0)), 0.0)
    lower = jnp.where(strict, jnp.einsum("bhnid,bhnjd->bhnij", kb, k) * decay, 0.0)
    eg = jnp.exp(g)[..., None]
    rhs = jnp.concatenate([vb, kb * eg], axis=-1)
    sol = lax.linalg.triangular_solve(lower + jnp.eye(C, dtype=lower.dtype), rhs,
                                      left_side=True, lower=True, unit_diagonal=True)
    w_val = sol[..., :dv]
    k_cum = sol[..., dv:]
    p_intra = jnp.einsum("bhnid,bhnjd->bhnij", q, k) * decay
    q_dec = q * eg
    g_last = g[..., -1]
    k_dec = k * jnp.exp(g_last[..., None] - g)[..., None]

    def step(state, xs):
        w_n, kc_n, p_n, qd_n, kd_n, gl_n = xs
        u = w_n - jnp.einsum("bhcd,bhdv->bhcv", kc_n, state)
        o = jnp.einsum("bhcd,bhdv->bhcv", qd_n, state) + jnp.einsum("bhij,bhjv->bhiv", p_n, u)
        state = state * jnp.exp(gl_n)[..., None, None] + jnp.einsum("bhcd,bhcv->bhdv", kd_n, u)
        return state, o

    xs = tuple(jnp.moveaxis(t, 2, 0) for t in (w_val, k_cum, p_intra, q_dec, k_dec, g_last))
    state0 = jnp.zeros((B, H, dk, dv), jnp.float32)
    _, o = lax.scan(step, state0, xs)
    return jnp.moveaxis(o, 0, 2).reshape(B, H, S, dv)


def gated_deltanet_mixer(h, w_in, conv_w, a_log, dt_bias, o_norm, w_out):
    B, S, _ = h.shape
    H, dh = N_HEADS, HEAD_DIM
    proj = h @ w_in
    qkv = proj[..., :QKV_DIM]
    z = proj[..., QKV_DIM:QKV_DIM + VAL_DIM]
    gates = proj[..., QKV_DIM + VAL_DIM:].astype(jnp.float32)
    b_raw = gates[..., :N_DIRS * H].reshape(B, S, N_DIRS, H)
    a_raw = gates[..., N_DIRS * H:].reshape(B, S, N_DIRS, H)
    qkv = jax.nn.silu(depthwise_conv(qkv, conv_w)).astype(jnp.float32)

    def heads(t):
        return t.reshape(B, S, H, dh).transpose(0, 2, 1, 3)

    q = l2norm(heads(qkv[..., :KEY_DIM])) * (dh ** -0.5)
    k = l2norm(heads(qkv[..., KEY_DIM:2 * KEY_DIM]))
    v = heads(qkv[..., 2 * KEY_DIM:])
    beta = jax.nn.sigmoid(b_raw).transpose(2, 0, 3, 1)
    log_a = (-jnp.exp(a_log.astype(jnp.float32))
             * jax.nn.softplus(a_raw + dt_bias.astype(jnp.float32))).transpose(2, 0, 3, 1)

    def flip(t):
        return jnp.flip(t, axis=2)

    o_fwd = chunk_gated_delta(q, k, v, log_a[0], beta[0])
    o_bwd = flip(chunk_gated_delta(flip(q), flip(k), flip(v), flip(log_a[1]), flip(beta[1])))
    o = (o_fwd + o_bwd).transpose(0, 2, 1, 3)
    o = o * lax.rsqrt(jnp.mean(o * o, axis=-1, keepdims=True) + RMS_EPS) * o_norm.astype(jnp.float32)
    o = o * jax.nn.silu(z.astype(jnp.float32).reshape(B, S, H, dh))
    return o.reshape(B, S, VAL_DIM).astype(h.dtype) @ w_out


def swiglu(h, w_gate, w_up, w_down):
    return (jax.nn.silu(h @ w_gate) * (h @ w_up)) @ w_down


def moe_swiglu(h, router, e_gate, e_up, e_down):
    B, S, D = h.shape
    t = h.reshape(B * S, D)
    logits = (t @ router).astype(jnp.float32)
    top_val, top_idx = lax.top_k(logits, TOP_K)
    top_w = jax.nn.softmax(top_val, axis=-1)
    gate = jnp.sum(jax.nn.one_hot(top_idx, N_EXPERTS, dtype=jnp.float32) * top_w[..., None], axis=1)
    out = jnp.zeros((B * S, D), jnp.float32)
    for e in range(N_EXPERTS):
        out = out + gate[:, e:e + 1] * swiglu(t, e_gate[e], e_up[e], e_down[e]).astype(jnp.float32)
    return out.astype(h.dtype).reshape(B, S, D)


def setup_inputs(seed: int = 0) -> dict:
    key = jax.random.key(seed)
    ks = iter(jax.random.split(key, 40))
    D = D_MODEL
    nc, nd = N_CONV_LAYERS, N_DELTA_LAYERS

    def nrm(shape, scale):
        return jax.random.normal(next(ks), shape, jnp.float32) * scale

    x = nrm((BATCH, SEQ, D), 1.0)
    mix_norm = 1.0 + nrm((DEPTH, D), 0.01)
    ffn_norm = 1.0 + nrm((DEPTH, D), 0.01)
    cf_pw1_w = nrm((nc, D, 2 * D), D ** -0.5)
    cf_pw1_b = nrm((nc, 2 * D), 0.01)
    cf_dw_w = nrm((nc, CONV_WIDTH, D), CONV_WIDTH ** -0.5)
    cf_dw_b = nrm((nc, D), 0.01)
    cf_ln_g = 1.0 + nrm((nc, D), 0.01)
    cf_ln_b = nrm((nc, D), 0.01)
    cf_pw2_w = nrm((nc, D, D), D ** -0.5)
    cf_pw2_b = nrm((nc, D), 0.01)
    ffn_w_gate = nrm((nc, D, D_FF), D ** -0.5)
    ffn_w_up = nrm((nc, D, D_FF), D ** -0.5)
    ffn_w_down = nrm((nc, D_FF, D), D_FF ** -0.5)
    gdn_w_in = nrm((nd, D, IN_PROJ_DIM), D ** -0.5)
    gdn_conv_w = nrm((nd, SHORT_CONV, QKV_DIM), SHORT_CONV ** -0.5)
    gdn_a_log = jnp.log(jax.random.uniform(next(ks), (nd, N_DIRS, N_HEADS), jnp.float32, 1.0, 16.0))
    dt = jnp.exp(jax.random.uniform(next(ks), (nd, N_DIRS, N_HEADS), jnp.float32,
                                    math.log(1e-3), math.log(1e-1)))
    gdn_dt_bias = dt + jnp.log(-jnp.expm1(-dt))
    gdn_o_norm = 1.0 + nrm((nd, HEAD_DIM), 0.01)
    gdn_w_out = nrm((nd, VAL_DIM, D), VAL_DIM ** -0.5)
    moe_router = nrm((nd, D, N_EXPERTS), D ** -0.5)
    moe_w_gate = nrm((nd, N_EXPERTS, D, D_FF_EXPERT), D ** -0.5)
    moe_w_up = nrm((nd, N_EXPERTS, D, D_FF_EXPERT), D ** -0.5)
    moe_w_down = nrm((nd, N_EXPERTS, D_FF_EXPERT, D), D_FF_EXPERT ** -0.5)
    final_norm = 1.0 + nrm((D,), 0.01)
    return {
        "x": x, "mix_norm": mix_norm, "ffn_norm": ffn_norm,
        "cf_pw1_w": cf_pw1_w, "cf_pw1_b": cf_pw1_b, "cf_dw_w": cf_dw_w, "cf_dw_b": cf_dw_b,
        "cf_ln_g": cf_ln_g, "cf_ln_b": cf_ln_b, "cf_pw2_w": cf_pw2_w, "cf_pw2_b": cf_pw2_b,
        "ffn_w_gate": ffn_w_gate, "ffn_w_up": ffn_w_up, "ffn_w_down": ffn_w_down,
        "gdn_w_in": gdn_w_in, "gdn_conv_w": gdn_conv_w, "gdn_a_log": gdn_a_log,
        "gdn_dt_bias": gdn_dt_bias, "gdn_o_norm": gdn_o_norm, "gdn_w_out": gdn_w_out,
        "moe_router": moe_router, "moe_w_gate": moe_w_gate, "moe_w_up": moe_w_up,
        "moe_w_down": moe_w_down, "final_norm": final_norm,
    }


def reference(x, mix_norm, ffn_norm, cf_pw1_w, cf_pw1_b, cf_dw_w, cf_dw_b, cf_ln_g, cf_ln_b,
              cf_pw2_w, cf_pw2_b, ffn_w_gate, ffn_w_up, ffn_w_down, gdn_w_in, gdn_conv_w,
              gdn_a_log, gdn_dt_bias, gdn_o_norm, gdn_w_out, moe_router, moe_w_gate, moe_w_up,
              moe_w_down, final_norm):
    for i in range(DEPTH):
        j = i // N_MIXERS
        h = rmsnorm(x, mix_norm[i])
        if i % N_MIXERS == 0:
            x = x + conformer_conv_mixer(h, cf_pw1_w[j], cf_pw1_b[j], cf_dw_w[j], cf_dw_b[j],
                                         cf_ln_g[j], cf_ln_b[j], cf_pw2_w[j], cf_pw2_b[j])
        else:
            x = x + gated_deltanet_mixer(h, gdn_w_in[j], gdn_conv_w[j], gdn_a_log[j],
                                         gdn_dt_bias[j], gdn_o_norm[j], gdn_w_out[j])
        h = rmsnorm(x, ffn_norm[i])
        if i % 2 == 0:
            x = x + swiglu(h, ffn_w_gate[j], ffn_w_up[j], ffn_w_down[j])
        else:
            x = x + moe_swiglu(h, moe_router[j], moe_w_gate[j], moe_w_up[j], moe_w_down[j])
    return rmsnorm(x, final_norm)
```

```python
import functools

import jax
import jax.numpy as jnp
from jax import lax
from jax.experimental import pallas as pl
from jax.experimental.pallas import tpu as pltpu

F32 = jnp.float32
BF16 = jnp.bfloat16

RMS_EPS = 1e-6
LN_EPS = 1e-5
L2_EPS = 1e-6
N_HEADS = 8
HEAD_DIM = 128
CHUNK = 64
N_EXPERTS = 8
LANES = 128
CONV_HALO = 16
SHORT_HALO = 8
VMEM_LIMIT_BYTES = 56 * 1024 * 1024


def _params(*sem):
    return pltpu.CompilerParams(dimension_semantics=sem, vmem_limit_bytes=VMEM_LIMIT_BYTES)


def _const_spec(shape):
    nd = len(shape)
    return pl.BlockSpec(shape, lambda *_: (0,) * nd, pipeline_mode=pl.Buffered(1))


def _rms(x, g):
    return x * lax.rsqrt(jnp.mean(x * x, axis=-1, keepdims=True) + RMS_EPS) * g


def _silu(x):
    return x * jax.nn.sigmoid(x)


def _mm(a, b):
    return jnp.dot(a.astype(BF16), b.astype(BF16), preferred_element_type=F32)


def _mm_nt(a, b):
    return lax.dot_general(a.astype(BF16), b.astype(BF16), (((1,), (1,)), ((), ())),
                           preferred_element_type=F32)


def _mm_tn(a, b):
    return lax.dot_general(a.astype(BF16), b.astype(BF16), (((0,), (0,)), ((), ())),
                           preferred_element_type=F32)


def _mm_exact(a, b):
    return jnp.dot(a, b, preferred_element_type=F32, precision=lax.Precision.HIGHEST)


def _pw1_glu_body(x_ref, g_ref, w_ref, b_ref, o_ref):
    d = o_ref.shape[-1]
    h = _rms(x_ref[...], g_ref[...])
    u = _mm(h, w_ref[...]) + b_ref[...]
    o_ref[...] = u[:, :d] * jax.nn.sigmoid(u[:, d:])


def _pw1_glu(x2d, g, w, b, tm):
    t, d = x2d.shape
    return pl.pallas_call(
        _pw1_glu_body,
        grid=(t // tm,),
        in_specs=[pl.BlockSpec((tm, d), lambda i: (i, 0)),
                  _const_spec((1, d)), _const_spec((d, 2 * d)), _const_spec((1, 2 * d))],
        out_specs=pl.BlockSpec((tm, d), lambda i: (i, 0)),
        out_shape=jax.ShapeDtypeStruct((t, d), F32),
        compiler_params=_params("parallel"),
        name="pw1_glu",
    )(x2d, g, w, b)


def _conf_conv_body(u_ref, up_ref, un_ref, x_ref, dw_ref, dwb_ref, lng_ref, lnb_ref, w2_ref, b2_ref,
                    o_ref, ext_ref, cv_ref):
    s = pl.program_id(1)
    ns = pl.num_programs(1)
    ts, d = cv_ref.shape
    width = dw_ref.shape[0]
    ext_ref[0:CONV_HALO, :] = jnp.where(s > 0, up_ref[0], 0.0)
    ext_ref[CONV_HALO:CONV_HALO + ts, :] = u_ref[0]
    ext_ref[CONV_HALO + ts:, :] = jnp.where(s < ns - 1, un_ref[0], 0.0)
    rb = 128
    base = CONV_HALO - width // 2
    for c in range(d // LANES):
        cs = slice(c * LANES, (c + 1) * LANES)
        for r in range(ts // rb):
            acc = jnp.zeros((rb, LANES), F32)
            for k in range(width):
                acc = acc + dw_ref[k:k + 1, cs] * ext_ref[pl.ds(r * rb + base + k, rb), cs]
            cv_ref[r * rb:(r + 1) * rb, cs] = acc + dwb_ref[:, cs]
    y = cv_ref[...]
    mu = jnp.mean(y, axis=-1, keepdims=True)
    yc = y - mu
    yn = yc * lax.rsqrt(jnp.mean(yc * yc, axis=-1, keepdims=True) + LN_EPS) * lng_ref[...] + lnb_ref[...]
    o_ref[0] = x_ref[0] + _mm(_silu(yn), w2_ref[...]) + b2_ref[...]


def _conf_conv(u3, x3, dw_w, dw_b, ln_g, ln_b, w2, b2, ts):
    bsz, s, d = u3.shape
    width = dw_w.shape[0]
    hb = ts // CONV_HALO
    nhb = s // CONV_HALO
    return pl.pallas_call(
        _conf_conv_body,
        grid=(bsz, s // ts),
        in_specs=[pl.BlockSpec((1, ts, d), lambda b, i: (b, i, 0)),
                  pl.BlockSpec((1, CONV_HALO, d), lambda b, i: (b, jnp.maximum(i * hb - 1, 0), 0)),
                  pl.BlockSpec((1, CONV_HALO, d), lambda b, i: (b, jnp.minimum((i + 1) * hb, nhb - 1), 0)),
                  pl.BlockSpec((1, ts, d), lambda b, i: (b, i, 0)),
                  _const_spec((width, d)), _const_spec((1, d)), _const_spec((1, d)), _const_spec((1, d)),
                  _const_spec((d, d)), _const_spec((1, d))],
        out_specs=pl.BlockSpec((1, ts, d), lambda b, i: (b, i, 0)),
        out_shape=jax.ShapeDtypeStruct((bsz, s, d), F32),
        scratch_shapes=[pltpu.VMEM((ts + 2 * CONV_HALO, d), F32), pltpu.VMEM((ts, d), F32)],
        compiler_params=_params("parallel", "parallel"),
        name="conf_conv",
    )(u3, u3, u3, x3, dw_w, dw_b, ln_g, ln_b, w2, b2)


def _ffn_body(x_ref, g_ref, wg_ref, wu_ref, wd_ref, o_ref, *, fc):
    x = x_ref[...]
    h = _rms(x, g_ref[...]).astype(BF16)
    dff = wg_ref.shape[1]
    acc = x
    for c in range(dff // fc):
        cs = slice(c * fc, (c + 1) * fc)
        a = _silu(_mm(h, wg_ref[:, cs])) * _mm(h, wu_ref[:, cs])
        acc = acc + _mm(a, wd_ref[cs, :])
    o_ref[...] = acc


def _ffn(x2d, g, wg, wu, wd, tm, fc):
    t, d = x2d.shape
    dff = wg.shape[1]
    return pl.pallas_call(
        functools.partial(_ffn_body, fc=fc),
        grid=(t // tm,),
        in_specs=[pl.BlockSpec((tm, d), lambda i: (i, 0)), _const_spec((1, d)),
                  _const_spec((d, dff)), _const_spec((d, dff)), _const_spec((dff, d))],
        out_specs=pl.BlockSpec((tm, d), lambda i: (i, 0)),
        out_shape=jax.ShapeDtypeStruct((t, d), F32),
        compiler_params=_params("parallel"),
        name="dense_swiglu",
    )(x2d, g, wg, wu, wd)


def _in_proj_body(x_ref, g_ref, w_ref, wgate_ref, qkv_ref, z_ref, gates_ref):
    nq = qkv_ref.shape[-1]
    h = _rms(x_ref[...], g_ref[...]).astype(BF16)
    p = _mm(h, w_ref[...])
    qkv_ref[...] = p[:, :nq]
    z_ref[...] = p[:, nq:]
    gates_ref[...] = _mm(h, wgate_ref[...])


def _in_proj(x2d, g, w_main, w_gate, tm):
    t, d = x2d.shape
    nmain = w_main.shape[1]
    nq = nmain - d
    return pl.pallas_call(
        _in_proj_body,
        grid=(t // tm,),
        in_specs=[pl.BlockSpec((tm, d), lambda i: (i, 0)), _const_spec((1, d)),
                  _const_spec((d, nmain)), _const_spec((d, LANES))],
        out_specs=[pl.BlockSpec((tm, nq), lambda i: (i, 0)), pl.BlockSpec((tm, d), lambda i: (i, 0)),
                   pl.BlockSpec((tm, LANES), lambda i: (i, 0))],
        out_shape=[jax.ShapeDtypeStruct((t, nq), F32), jax.ShapeDtypeStruct((t, d), F32),
                   jax.ShapeDtypeStruct((t, LANES), F32)],
        compiler_params=_params("parallel"),
        name="gdn_in_proj",
    )(x2d, g, w_main, w_gate)


def _tri_inverse(a, upper):
    c = a.shape[0]
    ri = lax.broadcasted_iota(jnp.int32, (c, c), 0)
    ci = lax.broadcasted_iota(jnp.int32, (c, c), 1)
    if upper:
        ri, ci = ci, ri
    eye = (ri == ci).astype(F32)
    l0 = jnp.where((ri // 8) == (ci // 8), a, 0.0)
    l2 = _mm(l0, l0)
    l4 = _mm(l2, l2)
    x = eye - l0 + l2 - _mm(l0, l2)
    x = x + _mm(x, l4)
    size = 8
    while size < c:
        off = ((ri // (2 * size)) == (ci // (2 * size))) & ((ri // size) != (ci // size))
        am = jnp.where(off, a, 0.0)
        x = x - _mm(x, _mm(am, x))
        size *= 2
    return x


def _gdn_prep_body(qkv_ref, qp_ref, qn_ref, gates_ref, cw_ref, alog_ref, dtb_ref,
                   w_ref, kc_ref, qd_ref, kd_ref, p_ref, egl_ref, ext_ref, act_ref):
    s = pl.program_id(1)
    ns = pl.num_programs(1)
    tg, nq = act_ref.shape
    kw = cw_ref.shape[0]
    dk = N_HEADS * HEAD_DIM
    nchunk = tg // CHUNK

    ext_ref[0:SHORT_HALO, :] = jnp.where(s > 0, qp_ref[0], 0.0)
    ext_ref[SHORT_HALO:SHORT_HALO + tg, :] = qkv_ref[0]
    ext_ref[SHORT_HALO + tg:, :] = jnp.where(s < ns - 1, qn_ref[0], 0.0)
    base = SHORT_HALO - kw // 2
    for c in range(nq // LANES):
        cs = slice(c * LANES, (c + 1) * LANES)
        acc = jnp.zeros((tg, LANES), F32)
        for k in range(kw):
            acc = acc + cw_ref[k:k + 1, cs] * ext_ref[pl.ds(base + k, tg), cs]
        act_ref[:, cs] = _silu(acc)

    gates = gates_ref[0]
    beta = jax.nn.sigmoid(gates)
    xa = gates + dtb_ref[...]
    softplus = jnp.maximum(xa, 0.0) + jnp.log1p(jnp.exp(-jnp.abs(xa)))
    log_a = -jnp.exp(alog_ref[...]) * softplus
    ri = lax.broadcasted_iota(jnp.int32, (tg, tg), 0)
    ci = lax.broadcasted_iota(jnp.int32, (tg, tg), 1)
    same = (ri // CHUNK) == (ci // CHUNK)
    cum_f = _mm_exact((same & (ci <= ri)).astype(F32), log_a)
    cum_b = _mm_exact((same & (ci >= ri)).astype(F32), log_a)
    lane = lax.broadcasted_iota(jnp.int32, (tg, LANES), 1)
    g_all = jnp.where(lane < 3 * N_HEADS, cum_f, cum_b)
    g_all_t = g_all.T

    r64 = lax.broadcasted_iota(jnp.int32, (CHUNK, CHUNK), 0)
    c64 = lax.broadcasted_iota(jnp.int32, (CHUNK, CHUNK), 1)

    for h in range(N_HEADS):
        hs = slice(h * HEAD_DIM, (h + 1) * HEAD_DIM)
        q = act_ref[:, hs]
        k = act_ref[:, dk + h * HEAD_DIM:dk + (h + 1) * HEAD_DIM]
        v = act_ref[:, 2 * dk + h * HEAD_DIM:2 * dk + (h + 1) * HEAD_DIM]
        q = q * lax.rsqrt(jnp.sum(q * q, axis=-1, keepdims=True) + L2_EPS) * (HEAD_DIM ** -0.5)
        k = k * lax.rsqrt(jnp.sum(k * k, axis=-1, keepdims=True) + L2_EPS)
        for dr in range(2):
            bcol = dr * N_HEADS + h
            gcol = 2 * N_HEADS + dr * N_HEADS + h
            bt = beta[:, bcol:bcol + 1]
            kb = k * bt
            vb = v * bt
            for n in range(nchunk):
                rs = slice(n * CHUNK, (n + 1) * CHUNK)
                g_c = g_all[rs, gcol:gcol + 1]
                g_r = g_all_t[gcol:gcol + 1, rs]
                last = n * CHUNK + (CHUNK - 1 if dr == 0 else 0)
                g_l = g_all[last:last + 1, gcol:gcol + 1]
                incl = (c64 <= r64) if dr == 0 else (c64 >= r64)
                strict = (c64 < r64) if dr == 0 else (c64 > r64)
                decay = jnp.where(incl, jnp.exp(jnp.where(incl, g_c - g_r, 0.0)), 0.0)
                kn = k[rs]
                qn = q[rs]
                kbn = kb[rs]
                eg = jnp.exp(g_c)
                low = jnp.where(strict, _mm_nt(kbn, kn) * decay, 0.0)
                inv = _tri_inverse(low, upper=(dr == 1))
                w_ref[dr, 0, h, rs, :] = _mm(inv, vb[rs])
                kc_ref[dr, 0, h, rs, :] = _mm(inv, kbn * eg).astype(BF16)
                p_ref[dr, 0, h, rs, :] = (_mm_nt(qn, kn) * decay).astype(BF16)
                qd_ref[dr, 0, h, rs, :] = (qn * eg).astype(BF16)
                kd_ref[dr, 0, h, rs, :] = (kn * jnp.exp(g_l - g_c)).astype(BF16)
                egl_ref[dr, 0, h, n] = jnp.broadcast_to(jnp.exp(g_l), (1, HEAD_DIM))


def _gdn_prep(qkv3, gates3, conv_w, alog_row, dtb_row, tg):
    bsz, s, nq = qkv3.shape
    hb = tg // SHORT_HALO
    nhb = s // SHORT_HALO
    nchunk = tg // CHUNK
    kw = conv_w.shape[0]

    def big(dtype, last=HEAD_DIM):
        return (jax.ShapeDtypeStruct((2, bsz, N_HEADS, s, last), dtype),
                pl.BlockSpec((2, 1, N_HEADS, tg, last), lambda b, i: (0, b, 0, i, 0)))

    outs = [big(F32), big(BF16), big(BF16), big(BF16), big(BF16, CHUNK),
            (jax.ShapeDtypeStruct((2, bsz, N_HEADS, s // CHUNK, 1, HEAD_DIM), F32),
             pl.BlockSpec((2, 1, N_HEADS, nchunk, 1, HEAD_DIM), lambda b, i: (0, b, 0, i, 0, 0)))]
    return pl.pallas_call(
        _gdn_prep_body,
        grid=(bsz, s // tg),
        in_specs=[pl.BlockSpec((1, tg, nq), lambda b, i: (b, i, 0)),
                  pl.BlockSpec((1, SHORT_HALO, nq), lambda b, i: (b, jnp.maximum(i * hb - 1, 0), 0)),
                  pl.BlockSpec((1, SHORT_HALO, nq), lambda b, i: (b, jnp.minimum((i + 1) * hb, nhb - 1), 0)),
                  pl.BlockSpec((1, tg, LANES), lambda b, i: (b, i, 0)),
                  _const_spec((kw, nq)), _const_spec((1, LANES)), _const_spec((1, LANES))],
        out_specs=[o[1] for o in outs],
        out_shape=[o[0] for o in outs],
        scratch_shapes=[pltpu.VMEM((tg + 2 * SHORT_HALO, nq), F32), pltpu.VMEM((tg, nq), F32)],
        compiler_params=_params("parallel", "parallel"),
        name="gdn_prep",
    )(qkv3, qkv3, qkv3, gates3, conv_w, alog_row, dtb_row)


def _gdn_scan_body(wf, kcf, qdf, kdf, pf, eglf, wb, kcb, qdb, kdb, pb, eglb, of_ref, ob_ref, state_ref):
    n = pl.program_id(0)

    @pl.when(n == 0)
    def _():
        state_ref[...] = jnp.zeros_like(state_ref)

    nbh = state_ref.shape[1]
    dirs = ((wf, kcf, qdf, kdf, pf, eglf, of_ref), (wb, kcb, qdb, kdb, pb, eglb, ob_ref))

    def body(i, carry):
        for dr, (w, kc, qd, kd, p, egl, o) in enumerate(dirs):
            st = state_ref[dr, i]
            sb = st.astype(BF16)
            u = w[0, i] - _mm(kc[0, i], sb)
            ub = u.astype(BF16)
            o[i] = _mm(qd[0, i], sb) + _mm(p[0, i], ub)
            state_ref[dr, i] = st * egl[0, i, 0] + _mm_tn(kd[0, i], ub)
        return carry

    lax.fori_loop(0, nbh, body, 0)


def _gdn_scan(w, kc, qd, kd, p, egl):
    _, nbh, s, dh = w.shape
    nchunk = s // CHUNK

    def specs(dr):
        def im(n):
            return (dr, 0, n if dr == 0 else nchunk - 1 - n, 0)

        def im5(n):
            return im(n) + (0,)
        return [pl.BlockSpec((1, nbh, CHUNK, dh), im), pl.BlockSpec((1, nbh, CHUNK, dh), im),
                pl.BlockSpec((1, nbh, CHUNK, dh), im), pl.BlockSpec((1, nbh, CHUNK, dh), im),
                pl.BlockSpec((1, nbh, CHUNK, CHUNK), im), pl.BlockSpec((1, nbh, 1, 1, dh), im5)]

    o_shape = jax.ShapeDtypeStruct((nbh, s, dh), F32)
    return pl.pallas_call(
        _gdn_scan_body,
        grid=(nchunk,),
        in_specs=specs(0) + specs(1),
        out_specs=[pl.BlockSpec((nbh, CHUNK, dh), lambda n: (0, n, 0)),
                   pl.BlockSpec((nbh, CHUNK, dh), lambda n: (0, nchunk - 1 - n, 0))],
        out_shape=[o_shape, o_shape],
        scratch_shapes=[pltpu.VMEM((2, nbh, dh, dh), F32)],
        compiler_params=_params("arbitrary"),
        name="gdn_scan",
    )(w, kc, qd, kd, p, egl, w, kc, qd, kd, p, egl)


def _gdn_out_body(of_ref, ob_ref, z_ref, x_ref, on_ref, wo_ref, fg_ref, rw_ref,
                  xo_ref, h_ref, gate_ref, act_ref):
    for h in range(N_HEADS):
        hs = slice(h * HEAD_DIM, (h + 1) * HEAD_DIM)
        o = of_ref[0, h] + ob_ref[0, h]
        o = o * lax.rsqrt(jnp.mean(o * o, axis=-1, keepdims=True) + RMS_EPS) * on_ref[...]
        act_ref[:, hs] = (o * _silu(z_ref[0, :, hs])).astype(BF16)
    x = x_ref[0] + _mm(act_ref[...], wo_ref[...])
    xo_ref[0] = x
    hn = _rms(x, fg_ref[...])
    h_ref[0] = hn.astype(BF16)
    logits = _mm_exact(hn, rw_ref[...])
    lane = lax.broadcasted_iota(jnp.int32, logits.shape, 1)
    neg = jnp.float32(-jnp.inf)
    logits = jnp.where(lane < N_EXPERTS, logits, neg)
    m1 = jnp.max(logits, axis=-1, keepdims=True)
    i1 = jnp.min(jnp.where(logits == m1, lane, LANES), axis=-1, keepdims=True)
    rest = jnp.where(lane == i1, neg, logits)
    m2 = jnp.max(rest, axis=-1, keepdims=True)
    i2 = jnp.min(jnp.where(rest == m2, lane, LANES), axis=-1, keepdims=True)
    e2 = jnp.exp(m2 - m1)
    w1 = 1.0 / (1.0 + e2)
    gate_ref[0] = jnp.where(lane == i1, w1, 0.0) + jnp.where(lane == i2, e2 * w1, 0.0)


def _gdn_out(o_f, o_b, z3, x3, o_norm, w_out, ffn_g, router_w, ts):
    bsz, s, d = x3.shape
    return pl.pallas_call(
        _gdn_out_body,
        grid=(bsz, s // ts),
        in_specs=[pl.BlockSpec((1, N_HEADS, ts, HEAD_DIM), lambda b, i: (b, 0, i, 0)),
                  pl.BlockSpec((1, N_HEADS, ts, HEAD_DIM), lambda b, i: (b, 0, i, 0)),
                  pl.BlockSpec((1, ts, d), lambda b, i: (b, i, 0)),
                  pl.BlockSpec((1, ts, d), lambda b, i: (b, i, 0)),
                  _const_spec((1, HEAD_DIM)), _const_spec((d, d)), _const_spec((1, d)),
                  _const_spec((d, LANES))],
        out_specs=[pl.BlockSpec((1, ts, d), lambda b, i: (b, i, 0)),
                   pl.BlockSpec((1, ts, d), lambda b, i: (b, i, 0)),
                   pl.BlockSpec((1, ts, LANES), lambda b, i: (b, i, 0))],
        out_shape=[jax.ShapeDtypeStruct((bsz, s, d), F32), jax.ShapeDtypeStruct((bsz, s, d), BF16),
                   jax.ShapeDtypeStruct((bsz, s, LANES), F32)],
        scratch_shapes=[pltpu.VMEM((ts, d), BF16)],
        compiler_params=_params("parallel", "parallel"),
        name="gdn_out_router",
    )(o_f, o_b, z3, x3, o_norm, w_out, ffn_g, router_w)


def _moe_dense_body(h_ref, gate_ref, x_ref, wg_ref, wu_ref, wd_ref, fn_ref, o_ref, acc_ref):
    e = pl.program_id(1)
    f = pl.program_id(2)

    @pl.when((e == 0) & (f == 0))
    def _():
        acc_ref[...] = jnp.zeros_like(acc_ref)

    h = h_ref[...]
    a = _silu(_mm(h, wg_ref[0])) * _mm(h, wu_ref[0])
    gate = gate_ref[...]
    lane = lax.broadcasted_iota(jnp.int32, gate.shape, 1)
    ge = jnp.sum(jnp.where(lane == e, gate, 0.0), axis=-1, keepdims=True)
    acc_ref[...] += ge * _mm(a, wd_ref[0])

    @pl.when((e == pl.num_programs(1) - 1) & (f == pl.num_programs(2) - 1))
    def _():
        o_ref[...] = _rms(x_ref[...] + acc_ref[...], fn_ref[...])


def _moe_dense(h2d, gate2d, x2d, wg, wu, wd, final_g, tm, tf):
    t, d = x2d.shape
    ne, _, dff = wg.shape
    return pl.pallas_call(
        _moe_dense_body,
        grid=(t // tm, ne, dff // tf),
        in_specs=[pl.BlockSpec((tm, d), lambda i, e, f: (i, 0)),
                  pl.BlockSpec((tm, LANES), lambda i, e, f: (i, 0)),
                  pl.BlockSpec((tm, d), lambda i, e, f: (i, 0)),
                  pl.BlockSpec((1, d, tf), lambda i, e, f: (e, 0, f)),
                  pl.BlockSpec((1, d, tf), lambda i, e, f: (e, 0, f)),
                  pl.BlockSpec((1, tf, d), lambda i, e, f: (e, f, 0)),
                  pl.BlockSpec((1, d), lambda i, e, f: (0, 0))],
        out_specs=pl.BlockSpec((tm, d), lambda i, e, f: (i, 0)),
        out_shape=jax.ShapeDtypeStruct((t, d), F32),
        scratch_shapes=[pltpu.VMEM((tm, d), F32)],
        compiler_params=_params("parallel", "arbitrary", "arbitrary"),
        name="moe_dense",
    )(h2d, gate2d, x2d, wg, wu, wd, final_g)


def _row(v):
    return v.reshape(1, -1).astype(F32)


def _pad_cols(w, n):
    return jnp.pad(w, ((0, 0), (0, n - w.shape[1])))


def _conformer_layer(x3, mix_g, ffn_g, pw1_w, pw1_b, dw_w, dw_b, ln_g, ln_b, pw2_w, pw2_b,
                     w_gate, w_up, w_down):
    bsz, s, d = x3.shape
    t = bsz * s
    tm = min(512, t)
    ts = min(512, s)
    u = _pw1_glu(x3.reshape(t, d), _row(mix_g), pw1_w.astype(BF16), _row(pw1_b), tm)
    x3 = _conf_conv(u.reshape(bsz, s, d), x3, dw_w, _row(dw_b), _row(ln_g), _row(ln_b),
                    pw2_w.astype(BF16), _row(pw2_b), ts)
    x2 = _ffn(x3.reshape(t, d), _row(ffn_g), w_gate.astype(BF16), w_up.astype(BF16),
              w_down.astype(BF16), tm, 256)
    return x2.reshape(bsz, s, d)


def _deltanet_moe_layer(x3, mix_g, ffn_g, w_in, conv_w, a_log, dt_bias, o_norm, w_out,
                        router, e_gate, e_up, e_down, final_g):
    bsz, s, d = x3.shape
    t = bsz * s
    tm = min(512, t)
    ts = min(512, s)
    nmain = 4 * d
    qkv, z, gates = _in_proj(x3.reshape(t, d), _row(mix_g), w_in[:, :nmain].astype(BF16),
                             _pad_cols(w_in[:, nmain:], LANES).astype(BF16), tm)
    nq = qkv.shape[1]
    zero16 = jnp.zeros((2 * N_HEADS,), F32)
    alog_row = _row(_pad_cols(jnp.concatenate([zero16, a_log.reshape(-1)])[None], LANES))
    dtb_row = _row(_pad_cols(jnp.concatenate([zero16, dt_bias.reshape(-1)])[None], LANES))
    w, kc, qd, kd, p, egl = _gdn_prep(qkv.reshape(bsz, s, nq), gates.reshape(bsz, s, LANES),
                                      conv_w, alog_row, dtb_row, min(256, s))
    nbh = bsz * N_HEADS
    o_f, o_b = _gdn_scan(w.reshape(2, nbh, s, HEAD_DIM), kc.reshape(2, nbh, s, HEAD_DIM),
                         qd.reshape(2, nbh, s, HEAD_DIM), kd.reshape(2, nbh, s, HEAD_DIM),
                         p.reshape(2, nbh, s, CHUNK), egl.reshape(2, nbh, s // CHUNK, 1, HEAD_DIM))
    x3, h, gate = _gdn_out(o_f.reshape(bsz, N_HEADS, s, HEAD_DIM), o_b.reshape(bsz, N_HEADS, s, HEAD_DIM),
                           z.reshape(bsz, s, d), x3, _row(o_norm), w_out.astype(BF16), _row(ffn_g),
                           _pad_cols(router, LANES), ts)
    out = _moe_dense(h.reshape(t, d), gate.reshape(t, LANES), x3.reshape(t, d),
                     e_gate.astype(BF16), e_up.astype(BF16), e_down.astype(BF16), _row(final_g),
                     tm, 512)
    return out.reshape(bsz, s, d)


def kernel(x, mix_norm, ffn_norm, cf_pw1_w, cf_pw1_b, cf_dw_w, cf_dw_b, cf_ln_g, cf_ln_b, cf_pw2_w, cf_pw2_b, ffn_w_gate, ffn_w_up, ffn_w_down, gdn_w_in, gdn_conv_w, gdn_a_log, gdn_dt_bias, gdn_o_norm, gdn_w_out, moe_router, moe_w_gate, moe_w_up, moe_w_down, final_norm):
    x = _conformer_layer(x, mix_norm[0], ffn_norm[0], cf_pw1_w[0], cf_pw1_b[0], cf_dw_w[0], cf_dw_b[0],
                         cf_ln_g[0], cf_ln_b[0], cf_pw2_w[0], cf_pw2_b[0],
                         ffn_w_gate[0], ffn_w_up[0], ffn_w_down[0])
    return _deltanet_moe_layer(x, mix_norm[1], ffn_norm[1], gdn_w_in[0], gdn_conv_w[0], gdn_a_log[0],
                               gdn_dt_bias[0], gdn_o_norm[0], gdn_w_out[0], moe_router[0],
                               moe_w_gate[0], moe_w_up[0], moe_w_down[0], final_norm)
```

```python
import functools

import jax
import jax.numpy as jnp
from jax import lax
from jax.experimental import pallas as pl
from jax.experimental.pallas import tpu as pltpu

F32 = jnp.float32
BF16 = jnp.bfloat16

RMS_EPS = 1e-6
LN_EPS = 1e-5
L2_EPS = 1e-6
N_HEADS = 8
HEAD_DIM = 128
CHUNK = 64
N_EXPERTS = 8
LANES = 128
CONV_HALO = 16
SHORT_HALO = 8
VMEM_LIMIT_BYTES = 56 * 1024 * 1024


def _params(*sem):
    return pltpu.CompilerParams(dimension_semantics=sem, vmem_limit_bytes=VMEM_LIMIT_BYTES)


def _const_spec(shape):
    nd = len(shape)
    return pl.BlockSpec(shape, lambda *_: (0,) * nd, pipeline_mode=pl.Buffered(1))


def _rms(x, g):
    return x * lax.rsqrt(jnp.mean(x * x, axis=-1, keepdims=True) + RMS_EPS) * g


def _silu(x):
    return x * jax.nn.sigmoid(x)


def _mm(a, b):
    return jnp.dot(a.astype(BF16), b.astype(BF16), preferred_element_type=F32)


def _mm_nt(a, b):
    return lax.dot_general(a.astype(BF16), b.astype(BF16), (((1,), (1,)), ((), ())),
                           preferred_element_type=F32)


def _mm_tn(a, b):
    return lax.dot_general(a.astype(BF16), b.astype(BF16), (((0,), (0,)), ((), ())),
                           preferred_element_type=F32)


def _mm_exact(a, b):
    return jnp.dot(a, b, preferred_element_type=F32, precision=lax.Precision.HIGHEST)


def _pw1_glu_body(x_ref, g_ref, w_ref, b_ref, o_ref):
    d = o_ref.shape[-1]
    h = _rms(x_ref[...], g_ref[...])
    u = _mm(h, w_ref[...]) + b_ref[...]
    o_ref[...] = u[:, :d] * jax.nn.sigmoid(u[:, d:])


def _pw1_glu(x2d, g, w, b, tm):
    t, d = x2d.shape
    return pl.pallas_call(
        _pw1_glu_body,
        grid=(t // tm,),
        in_specs=[pl.BlockSpec((tm, d), lambda i: (i, 0)),
                  _const_spec((1, d)), _const_spec((d, 2 * d)), _const_spec((1, 2 * d))],
        out_specs=pl.BlockSpec((tm, d), lambda i: (i, 0)),
        out_shape=jax.ShapeDtypeStruct((t, d), F32),
        compiler_params=_params("parallel"),
        name="pw1_glu",
    )(x2d, g, w, b)


def _conf_conv_body(u_ref, up_ref, un_ref, x_ref, dw_ref, dwb_ref, lng_ref, lnb_ref, w2_ref, b2_ref,
                    o_ref, ext_ref, cv_ref):
    s = pl.program_id(1)
    ns = pl.num_programs(1)
    ts, d = cv_ref.shape
    width = dw_ref.shape[0]
    ext_ref[0:CONV_HALO, :] = jnp.where(s > 0, up_ref[0], 0.0)
    ext_ref[CONV_HALO:CONV_HALO + ts, :] = u_ref[0]
    ext_ref[CONV_HALO + ts:, :] = jnp.where(s < ns - 1, un_ref[0], 0.0)
    rb = 128
    base = CONV_HALO - width // 2
    for c in range(d // LANES):
        cs = slice(c * LANES, (c + 1) * LANES)
        for r in range(ts // rb):
            acc = jnp.zeros((rb, LANES), F32)
            for k in range(width):
                acc = acc + dw_ref[k:k + 1, cs] * ext_ref[pl.ds(r * rb + base + k, rb), cs]
            cv_ref[r * rb:(r + 1) * rb, cs] = acc + dwb_ref[:, cs]
    y = cv_ref[...]
    mu = jnp.mean(y, axis=-1, keepdims=True)
    yc = y - mu
    yn = yc * lax.rsqrt(jnp.mean(yc * yc, axis=-1, keepdims=True) + LN_EPS) * lng_ref[...] + lnb_ref[...]
    o_ref[0] = x_ref[0] + _mm(_silu(yn), w2_ref[...]) + b2_ref[...]


def _conf_conv(u3, x3, dw_w, dw_b, ln_g, ln_b, w2, b2, ts):
    bsz, s, d = u3.shape
    width = dw_w.shape[0]
    hb = ts // CONV_HALO
    nhb = s // CONV_HALO
    return pl.pallas_call(
        _conf_conv_body,
        grid=(bsz, s // ts),
        in_specs=[pl.BlockSpec((1, ts, d), lambda b, i: (b, i, 0)),
                  pl.BlockSpec((1, CONV_HALO, d), lambda b, i: (b, jnp.maximum(i * hb - 1, 0), 0)),
                  pl.BlockSpec((1, CONV_HALO, d), lambda b, i: (b, jnp.minimum((i + 1) * hb, nhb - 1), 0)),
                  pl.BlockSpec((1, ts, d), lambda b, i: (b, i, 0)),
                  _const_spec((width, d)), _const_spec((1, d)), _const_spec((1, d)), _const_spec((1, d)),
                  _const_spec((d, d)), _const_spec((1, d))],
        out_specs=pl.BlockSpec((1, ts, d), lambda b, i: (b, i, 0)),
        out_shape=jax.ShapeDtypeStruct((bsz, s, d), F32),
        scratch_shapes=[pltpu.VMEM((ts + 2 * CONV_HALO, d), F32), pltpu.VMEM((ts, d), F32)],
        compiler_params=_params("parallel", "parallel"),
        name="conf_conv",
    )(u3, u3, u3, x3, dw_w, dw_b, ln_g, ln_b, w2, b2)


def _ffn_body(x_ref, g_ref, wg_ref, wu_ref, wd_ref, o_ref, *, fc):
    x = x_ref[...]
    h = _rms(x, g_ref[...]).astype(BF16)
    dff = wg_ref.shape[1]
    acc = x
    for c in range(dff // fc):
        cs = slice(c * fc, (c + 1) * fc)
        a = _silu(_mm(h, wg_ref[:, cs])) * _mm(h, wu_ref[:, cs])
        acc = acc + _mm(a, wd_ref[cs, :])
    o_ref[...] = acc


def _ffn(x2d, g, wg, wu, wd, tm, fc):
    t, d = x2d.shape
    dff = wg.shape[1]
    return pl.pallas_call(
        functools.partial(_ffn_body, fc=fc),
        grid=(t // tm,),
        in_specs=[pl.BlockSpec((tm, d), lambda i: (i, 0)), _const_spec((1, d)),
                  _const_spec((d, dff)), _const_spec((d, dff)), _const_spec((dff, d))],
        out_specs=pl.BlockSpec((tm, d), lambda i: (i, 0)),
        out_shape=jax.ShapeDtypeStruct((t, d), F32),
        compiler_params=_params("parallel"),
        name="dense_swiglu",
    )(x2d, g, wg, wu, wd)


def _in_proj_body(x_ref, g_ref, w_ref, wgate_ref, qkv_ref, z_ref, gates_ref):
    nq = qkv_ref.shape[-1]
    h = _rms(x_ref[...], g_ref[...]).astype(BF16)
    p = _mm(h, w_ref[...])
    qkv_ref[...] = p[:, :nq]
    z_ref[...] = p[:, nq:]
    gates_ref[...] = _mm(h, wgate_ref[...])


def _in_proj(x2d, g, w_main, w_gate, tm):
    t, d = x2d.shape
    nmain = w_main.shape[1]
    nq = nmain - d
    return pl.pallas_call(
        _in_proj_body,
        grid=(t // tm,),
        in_specs=[pl.BlockSpec((tm, d), lambda i: (i, 0)), _const_spec((1, d)),
                  _const_spec((d, nmain)), _const_spec((d, LANES))],
        out_specs=[pl.BlockSpec((tm, nq), lambda i: (i, 0)), pl.BlockSpec((tm, d), lambda i: (i, 0)),
                   pl.BlockSpec((tm, LANES), lambda i: (i, 0))],
        out_shape=[jax.ShapeDtypeStruct((t, nq), F32), jax.ShapeDtypeStruct((t, d), F32),
                   jax.ShapeDtypeStruct((t, LANES), F32)],
        compiler_params=_params("parallel"),
        name="gdn_in_proj",
    )(x2d, g, w_main, w_gate)


def _tri_inverse_many(mats):
    c = CHUNK
    ri = lax.broadcasted_iota(jnp.int32, (c, c), 0)
    ci = lax.broadcasted_iota(jnp.int32, (c, c), 1)
    eye = (ri == ci).astype(F32)
    blk = (ri // 8) == (ci // 8)
    l0 = [jnp.where(blk, a, 0.0) for a in mats]
    l2 = [_mm(a, a) for a in l0]
    l4 = [_mm(a, a) for a in l2]
    l3 = [_mm(a, b) for a, b in zip(l0, l2)]
    xs = [eye - a + b - t for a, b, t in zip(l0, l2, l3)]
    x4 = [_mm(x, a) for x, a in zip(xs, l4)]
    xs = [x + t for x, t in zip(xs, x4)]
    size = 8
    while size < c:
        off = ((ri // (2 * size)) == (ci // (2 * size))) & ((ri // size) != (ci // size))
        t1 = [_mm(jnp.where(off, a, 0.0), x) for a, x in zip(mats, xs)]
        t2 = [_mm(x, t) for x, t in zip(xs, t1)]
        xs = [x - t for x, t in zip(xs, t2)]
        size *= 2
    return xs


GDN_HEAD_GROUP = 2


def _gdn_prep_body(qkv_ref, qp_ref, qn_ref, gates_ref, cw_ref, alog_ref, dtb_ref,
                   w_ref, kq_ref, kd_ref, p_ref, egl_ref, ext_ref, act_ref):
    s = pl.program_id(1)
    ns = pl.num_programs(1)
    tg, nq = act_ref.shape
    kw = cw_ref.shape[0]
    dk = N_HEADS * HEAD_DIM
    nchunk = tg // CHUNK

    ext_ref[0:SHORT_HALO, :] = jnp.where(s > 0, qp_ref[0], 0.0)
    ext_ref[SHORT_HALO:SHORT_HALO + tg, :] = qkv_ref[0]
    ext_ref[SHORT_HALO + tg:, :] = jnp.where(s < ns - 1, qn_ref[0], 0.0)
    base = SHORT_HALO - kw // 2
    for c in range(nq // LANES):
        cs = slice(c * LANES, (c + 1) * LANES)
        acc = jnp.zeros((tg, LANES), F32)
        for k in range(kw):
            acc = acc + cw_ref[k:k + 1, cs] * ext_ref[pl.ds(base + k, tg), cs]
        act_ref[:, cs] = _silu(acc)

    gates = gates_ref[0]
    beta = jax.nn.sigmoid(gates)
    xa = gates + dtb_ref[...]
    softplus = jnp.maximum(xa, 0.0) + jnp.log1p(jnp.exp(-jnp.abs(xa)))
    log_a = -jnp.exp(alog_ref[...]) * softplus
    ri = lax.broadcasted_iota(jnp.int32, (tg, tg), 0)
    ci = lax.broadcasted_iota(jnp.int32, (tg, tg), 1)
    same = (ri // CHUNK) == (ci // CHUNK)
    cum_f = _mm_exact((same & (ci <= ri)).astype(F32), log_a)
    cum_b = _mm_exact((same & (ci >= ri)).astype(F32), log_a)
    lane = lax.broadcasted_iota(jnp.int32, (tg, LANES), 1)
    g_all = jnp.where(lane < 3 * N_HEADS, cum_f, cum_b)
    g_all_t = g_all.T

    r64 = lax.broadcasted_iota(jnp.int32, (CHUNK, CHUNK), 0)
    c64 = lax.broadcasted_iota(jnp.int32, (CHUNK, CHUNK), 1)
    incl = (c64 <= r64, c64 >= r64)
    strict = (c64 < r64, c64 > r64)

    for h0 in range(0, N_HEADS, GDN_HEAD_GROUP):
        heads = range(h0, h0 + GDN_HEAD_GROUP)
        qs, ks, vs = {}, {}, {}
        for h in heads:
            q = act_ref[:, h * HEAD_DIM:(h + 1) * HEAD_DIM]
            k = act_ref[:, dk + h * HEAD_DIM:dk + (h + 1) * HEAD_DIM]
            vs[h] = act_ref[:, 2 * dk + h * HEAD_DIM:2 * dk + (h + 1) * HEAD_DIM]
            qs[h] = q * lax.rsqrt(jnp.sum(q * q, axis=-1, keepdims=True) + L2_EPS) * (HEAD_DIM ** -0.5)
            ks[h] = k * lax.rsqrt(jnp.sum(k * k, axis=-1, keepdims=True) + L2_EPS)
        raw = {}
        for h in heads:
            for n in range(nchunk):
                rs = slice(n * CHUNK, (n + 1) * CHUNK)
                raw[h, n] = _mm_nt(jnp.concatenate([ks[h][rs], qs[h][rs]], axis=0), ks[h][rs])
        insts = [(h, dr, n) for h in heads for dr in range(2) for n in range(nchunk)]
        lows, rhss = [], []
        for h, dr, n in insts:
            rs = slice(n * CHUNK, (n + 1) * CHUNK)
            bcol = dr * N_HEADS + h
            gcol = 2 * N_HEADS + dr * N_HEADS + h
            bt = beta[rs, bcol:bcol + 1]
            g_c = g_all[rs, gcol:gcol + 1]
            g_r = g_all_t[gcol:gcol + 1, rs]
            last = n * CHUNK + (CHUNK - 1 if dr == 0 else 0)
            g_l = g_all[last:last + 1, gcol:gcol + 1]
            decay = jnp.where(incl[dr], jnp.exp(jnp.where(incl[dr], g_c - g_r, 0.0)), 0.0)
            eg = jnp.exp(g_c)
            kn = ks[h][rs]
            lows.append(jnp.where(strict[dr], raw[h, n][:CHUNK] * bt * decay, 0.0))
            rhss.append(jnp.concatenate([vs[h][rs] * bt, kn * (bt * eg)], axis=1))
            p_ref[dr, 0, h, rs, :] = (raw[h, n][CHUNK:] * decay).astype(BF16)
            kq_ref[dr, 0, h, n, CHUNK:, :] = (qs[h][rs] * eg).astype(BF16)
            kd_ref[dr, 0, h, rs, :] = (kn * jnp.exp(g_l - g_c)).astype(BF16)
            egl_ref[dr, 0, h, n] = jnp.broadcast_to(jnp.exp(g_l), (1, HEAD_DIM))
        invs = _tri_inverse_many(lows)
        sols = [_mm(inv, rhs) for inv, rhs in zip(invs, rhss)]
        for (h, dr, n), sol in zip(insts, sols):
            rs = slice(n * CHUNK, (n + 1) * CHUNK)
            w_ref[dr, 0, h, rs, :] = sol[:, :HEAD_DIM]
            kq_ref[dr, 0, h, n, :CHUNK, :] = sol[:, HEAD_DIM:].astype(BF16)


def _gdn_prep(qkv3, gates3, conv_w, alog_row, dtb_row, tg):
    bsz, s, nq = qkv3.shape
    hb = tg // SHORT_HALO
    nhb = s // SHORT_HALO
    nchunk = tg // CHUNK
    kw = conv_w.shape[0]

    def rows(dtype, last=HEAD_DIM):
        return (jax.ShapeDtypeStruct((2, bsz, N_HEADS, s, last), dtype),
                pl.BlockSpec((2, 1, N_HEADS, tg, last), lambda b, i: (0, b, 0, i, 0)))

    def per_chunk(dtype, r):
        return (jax.ShapeDtypeStruct((2, bsz, N_HEADS, s // CHUNK, r, HEAD_DIM), dtype),
                pl.BlockSpec((2, 1, N_HEADS, nchunk, r, HEAD_DIM), lambda b, i: (0, b, 0, i, 0, 0)))

    outs = [rows(F32), per_chunk(BF16, 2 * CHUNK), rows(BF16), rows(BF16, CHUNK), per_chunk(F32, 1)]
    return pl.pallas_call(
        _gdn_prep_body,
        grid=(bsz, s // tg),
        in_specs=[pl.BlockSpec((1, tg, nq), lambda b, i: (b, i, 0)),
                  pl.BlockSpec((1, SHORT_HALO, nq), lambda b, i: (b, jnp.maximum(i * hb - 1, 0), 0)),
                  pl.BlockSpec((1, SHORT_HALO, nq), lambda b, i: (b, jnp.minimum((i + 1) * hb, nhb - 1), 0)),
                  pl.BlockSpec((1, tg, LANES), lambda b, i: (b, i, 0)),
                  _const_spec((kw, nq)), _const_spec((1, LANES)), _const_spec((1, LANES))],
        out_specs=[o[1] for o in outs],
        out_shape=[o[0] for o in outs],
        scratch_shapes=[pltpu.VMEM((tg + 2 * SHORT_HALO, nq), F32), pltpu.VMEM((tg, nq), F32)],
        compiler_params=_params("parallel", "parallel"),
        name="gdn_prep",
    )(qkv3, qkv3, qkv3, gates3, conv_w, alog_row, dtb_row)


GDN_SCAN_GROUP = 4


def _gdn_scan_body(wf, kqf, kdf, pf, eglf, wb, kqb, kdb, pb, eglb, of_ref, ob_ref, state_ref):
    n = pl.program_id(0)

    @pl.when(n == 0)
    def _():
        state_ref[...] = jnp.zeros_like(state_ref)

    nbh = state_ref.shape[1]
    dirs = ((wf, kqf, kdf, pf, eglf, of_ref), (wb, kqb, kdb, pb, eglb, ob_ref))

    def body(i, carry):
        chains = [(dr, i * GDN_SCAN_GROUP + j) for j in range(GDN_SCAN_GROUP) for dr in range(2)]
        st = [state_ref[dr, b] for dr, b in chains]
        r = [_mm(dirs[dr][1][0, b, 0], s_) for (dr, b), s_ in zip(chains, st)]
        ub = [(dirs[dr][0][0, b] - r_[:CHUNK]).astype(BF16) for (dr, b), r_ in zip(chains, r)]
        pu = [_mm(dirs[dr][3][0, b], u_) for (dr, b), u_ in zip(chains, ub)]
        ku = [_mm_tn(dirs[dr][2][0, b], u_) for (dr, b), u_ in zip(chains, ub)]
        for (dr, b), r_, pu_, ku_, s_ in zip(chains, r, pu, ku, st):
            dirs[dr][5][b] = r_[CHUNK:] + pu_
            state_ref[dr, b] = s_ * dirs[dr][4][0, b, 0] + ku_
        return carry

    lax.fori_loop(0, nbh // GDN_SCAN_GROUP, body, 0)


def _gdn_scan(w, kq, kd, p, egl):
    _, nbh, s, dh = w.shape
    nchunk = s // CHUNK

    def specs(dr):
        def im(n):
            return (dr, 0, n if dr == 0 else nchunk - 1 - n, 0)

        def im5(n):
            return im(n) + (0,)
        return [pl.BlockSpec((1, nbh, CHUNK, dh), im), pl.BlockSpec((1, nbh, 1, 2 * CHUNK, dh), im5),
                pl.BlockSpec((1, nbh, CHUNK, dh), im), pl.BlockSpec((1, nbh, CHUNK, CHUNK), im),
                pl.BlockSpec((1, nbh, 1, 1, dh), im5)]

    o_shape = jax.ShapeDtypeStruct((nbh, s, dh), F32)
    return pl.pallas_call(
        _gdn_scan_body,
        grid=(nchunk,),
        in_specs=specs(0) + specs(1),
        out_specs=[pl.BlockSpec((nbh, CHUNK, dh), lambda n: (0, n, 0)),
                   pl.BlockSpec((nbh, CHUNK, dh), lambda n: (0, nchunk - 1 - n, 0))],
        out_shape=[o_shape, o_shape],
        scratch_shapes=[pltpu.VMEM((2, nbh, dh, dh), F32)],
        compiler_params=_params("arbitrary"),
        name="gdn_scan",
    )(w, kq, kd, p, egl, w, kq, kd, p, egl)


TOKEN_TILE_ROWS = 8


def _to_token_tiles(dst_ref, val):
    n = val.shape[0]
    for j in range(TOKEN_TILE_ROWS):
        dst_ref[pl.ds(j, n, stride=TOKEN_TILE_ROWS), :] = val[:, j * LANES:(j + 1) * LANES]


def _from_token_tiles(src_ref, n):
    return jnp.concatenate([src_ref[pl.ds(j, n, stride=TOKEN_TILE_ROWS), :] for j in range(TOKEN_TILE_ROWS)],
                           axis=1)


def _gdn_out_body(of_ref, ob_ref, z_ref, x_ref, on_ref, wo_ref, fg_ref, rwt_ref,
                  xo_ref, pos_ref, wcol_ref, cnt_ref, xs_ref,
                  act_ref, tile_ref, zero_ref, carry_ref, pos_smem, cnt_smem, sem, psem, zsem,
                  *, cap, tm):
    g = pl.program_id(0)
    ng = pl.num_programs(0)
    ts = x_ref.shape[0]
    slot = g % 2
    rpt = TOKEN_TILE_ROWS

    def wait_rows(s):
        for _ in range(2):
            pltpu.make_async_copy(tile_ref.at[s], xs_ref.at[pl.ds(0, ts * rpt)], sem.at[s]).wait()

    @pl.when(g == 0)
    def _():
        carry_ref[...] = jnp.zeros_like(carry_ref)

    for h in range(N_HEADS):
        hs = slice(h * HEAD_DIM, (h + 1) * HEAD_DIM)
        o = of_ref[0, h] + ob_ref[0, h]
        o = o * lax.rsqrt(jnp.mean(o * o, axis=-1, keepdims=True) + RMS_EPS) * on_ref[...]
        act_ref[:, hs] = (o * _silu(z_ref[:, hs])).astype(BF16)
    x = x_ref[...] + _mm(act_ref[...], wo_ref[...])
    xo_ref[...] = x
    hn = _rms(x, fg_ref[...])

    logits = lax.dot_general(rwt_ref[...], hn, (((1,), (1,)), ((), ())),
                             preferred_element_type=F32, precision=lax.Precision.HIGHEST)
    eidx = lax.broadcasted_iota(jnp.int32, logits.shape, 0).astype(F32)
    neg = jnp.float32(-jnp.inf)
    m1 = jnp.max(logits, axis=0, keepdims=True)
    i1 = jnp.min(jnp.where(logits == m1, eidx, float(N_EXPERTS)), axis=0, keepdims=True)
    one1 = eidx == i1
    rest = jnp.where(one1, neg, logits)
    m2 = jnp.max(rest, axis=0, keepdims=True)
    i2 = jnp.min(jnp.where(rest == m2, eidx, float(N_EXPERTS)), axis=0, keepdims=True)
    one2 = eidx == i2
    e2 = jnp.exp(m2 - m1)
    w1 = 1.0 / (1.0 + e2)
    w2 = e2 * w1

    chosen = jnp.where(one1 | one2, 1.0, 0.0)
    ri = lax.broadcasted_iota(jnp.int32, (ts, ts), 0)
    ci = lax.broadcasted_iota(jnp.int32, (ts, ts), 1)
    before = jnp.where(ri < ci, 1.0, 0.0).astype(BF16)
    rank = jnp.dot(chosen.astype(BF16), before, preferred_element_type=F32)
    carry = carry_ref[...]
    slot_f = eidx * float(cap) + carry[:, 0:1] + rank
    p1 = jnp.sum(jnp.where(one1, slot_f, 0.0), axis=0, keepdims=True)
    p2 = jnp.sum(jnp.where(one2, slot_f, 0.0), axis=0, keepdims=True)
    carry = carry + jnp.sum(chosen, axis=1, keepdims=True)
    carry_ref[...] = carry
    cnt_ref[...] = carry.astype(jnp.int32)
    row8 = lax.broadcasted_iota(jnp.int32, (N_EXPERTS, ts), 0)
    pos = jnp.where(row8 == 0, p1, jnp.where(row8 == 1, p2, 0.0)).astype(jnp.int32)
    pos_ref[...] = pos
    row128 = lax.broadcasted_iota(jnp.int32, (LANES, ts), 0)
    wcol_ref[...] = jnp.where(row128 == 0, w1, jnp.where(row128 == 1, w2, 0.0)).T

    @pl.when(g >= 2)
    def _():
        wait_rows(slot)

    _to_token_tiles(tile_ref.at[slot], hn)
    cp = pltpu.make_async_copy(pos_ref, pos_smem, psem)
    cp.start()
    cp.wait()

    def issue(t, c):
        src = tile_ref.at[slot, pl.ds(t * rpt, rpt)]
        for k in range(2):
            dst = xs_ref.at[pl.ds(pos_smem[k, t] * rpt, rpt)]
            pltpu.make_async_copy(src, dst, sem.at[slot]).start()
        return c

    lax.fori_loop(0, ts, issue, 0)

    @pl.when(g == ng - 1)
    def _():
        wait_rows(slot)

        @pl.when(g >= 1)
        def _():
            wait_rows(1 - slot)

        zero_ref[...] = jnp.zeros_like(zero_ref)
        cc = pltpu.make_async_copy(cnt_ref, cnt_smem, psem)
        cc.start()
        cc.wait()
        tails = [pltpu.make_async_copy(
            zero_ref, xs_ref.at[pl.ds((e * cap + cnt_smem[e, 0]) * rpt, tm * rpt)], zsem)
            for e in range(N_EXPERTS)]
        for c in tails:
            c.start()
        for c in tails:
            c.wait()


def _gdn_out(o_f, o_b, z2, x2, o_norm, w_out, ffn_g, router_t, ts, tm, cap):
    t, d = x2.shape
    s = o_f.shape[2]
    spb = s // ts
    rpt = TOKEN_TILE_ROWS
    o_spec = pl.BlockSpec((1, N_HEADS, ts, HEAD_DIM), lambda g: (g // spb, 0, g % spb, 0))
    return pl.pallas_call(
        functools.partial(_gdn_out_body, cap=cap, tm=tm),
        grid=(t // ts,),
        in_specs=[o_spec, o_spec,
                  pl.BlockSpec((ts, d), lambda g: (g, 0)), pl.BlockSpec((ts, d), lambda g: (g, 0)),
                  _const_spec((1, HEAD_DIM)), _const_spec((d, d)), _const_spec((1, d)),
                  _const_spec((N_EXPERTS, d))],
        out_specs=[pl.BlockSpec((ts, d), lambda g: (g, 0)),
                   pl.BlockSpec((N_EXPERTS, ts), lambda g: (0, g)),
                   pl.BlockSpec((ts, LANES), lambda g: (g, 0)),
                   pl.BlockSpec((N_EXPERTS, LANES), lambda g: (0, 0)),
                   pl.BlockSpec(memory_space=pl.ANY)],
        out_shape=[jax.ShapeDtypeStruct((t, d), F32),
                   jax.ShapeDtypeStruct((N_EXPERTS, t), jnp.int32),
                   jax.ShapeDtypeStruct((t, LANES), F32),
                   jax.ShapeDtypeStruct((N_EXPERTS, LANES), jnp.int32),
                   jax.ShapeDtypeStruct((N_EXPERTS * cap * rpt, LANES), F32)],
        scratch_shapes=[pltpu.VMEM((ts, d), BF16),
                        pltpu.VMEM((2, ts * rpt, LANES), F32),
                        pltpu.VMEM((tm * rpt, LANES), F32),
                        pltpu.VMEM((N_EXPERTS, LANES), F32),
                        pltpu.SMEM((N_EXPERTS, ts), jnp.int32),
                        pltpu.SMEM((N_EXPERTS, LANES), jnp.int32),
                        pltpu.SemaphoreType.DMA((2,)), pltpu.SemaphoreType.DMA, pltpu.SemaphoreType.DMA],
        compiler_params=_params("arbitrary"),
        name="gdn_out_router",
    )(o_f, o_b, z2, x2, o_norm, w_out, ffn_g, router_t)


def _moe_body(te_ref, tb_ref, nu_ref, xs_ref, wg_ref, wu_ref, wd_ref, ys_ref, xb_ref, acc_ref, *, fc):
    i = pl.program_id(0)
    f = pl.program_id(1)
    tm = xb_ref.shape[0]
    tf = wg_ref.shape[2]

    @pl.when(i < nu_ref[0])
    def _():
        @pl.when(f == 0)
        def _():
            xb_ref[...] = _from_token_tiles(xs_ref, tm).astype(BF16)
            acc_ref[...] = jnp.zeros_like(acc_ref)

        x = xb_ref[...]
        for c in range(tf // fc):
            cs = slice(c * fc, (c + 1) * fc)
            a = _silu(_mm(x, wg_ref[0, :, cs])) * _mm(x, wu_ref[0, :, cs])
            acc_ref[...] += _mm(a, wd_ref[0, cs, :])

        @pl.when(f == pl.num_programs(1) - 1)
        def _():
            _to_token_tiles(ys_ref, acc_ref[...])


def _moe_grouped(xs, tile_expert, tile_block, n_used, wg, wu, wd, tm, tf, fc):
    ne, d, dff = wg.shape
    nt = tile_expert.shape[0]
    nf = dff // tf
    rpt = TOKEN_TILE_ROWS

    def fsel(i, f, nu):
        return jnp.where(i < nu[0], f, nf - 1)

    grid_spec = pltpu.PrefetchScalarGridSpec(
        num_scalar_prefetch=3,
        grid=(nt, nf),
        in_specs=[pl.BlockSpec((tm * rpt, LANES), lambda i, f, te, tb, nu: (tb[i], 0)),
                  pl.BlockSpec((1, d, tf), lambda i, f, te, tb, nu: (te[i], 0, fsel(i, f, nu))),
                  pl.BlockSpec((1, d, tf), lambda i, f, te, tb, nu: (te[i], 0, fsel(i, f, nu))),
                  pl.BlockSpec((1, tf, d), lambda i, f, te, tb, nu: (te[i], fsel(i, f, nu), 0))],
        out_specs=pl.BlockSpec((tm * rpt, LANES), lambda i, f, te, tb, nu: (tb[i], 0)),
        scratch_shapes=[pltpu.VMEM((tm, d), BF16), pltpu.VMEM((tm, d), F32)],
    )
    return pl.pallas_call(
        functools.partial(_moe_body, fc=fc),
        grid_spec=grid_spec,
        out_shape=jax.ShapeDtypeStruct(xs.shape, F32),
        compiler_params=_params("arbitrary", "arbitrary"),
        name="moe_grouped",
    )(tile_expert, tile_block, n_used, xs, wg, wu, wd)


def _combine_body(pos_ref, posn_ref, x_ref, wcol_ref, fn_ref, ys_ref, o_ref, gbuf, pos_smem, sem, psem):
    i = pl.program_id(0)
    n = pl.num_programs(0)
    tc = x_ref.shape[0]
    rpt = TOKEN_TILE_ROWS

    def fetch(p_ref, slot):
        cp = pltpu.make_async_copy(p_ref, pos_smem, psem)
        cp.start()
        cp.wait()

        def issue(t, c):
            for k in range(2):
                src = ys_ref.at[pl.ds(pos_smem[k, t] * rpt, rpt)]
                pltpu.make_async_copy(src, gbuf.at[slot, k, pl.ds(t * rpt, rpt)], sem.at[slot]).start()
            return c

        lax.fori_loop(0, tc, issue, 0)

    @pl.when(i == 0)
    def _():
        fetch(pos_ref, 0)

    @pl.when(i + 1 < n)
    def _():
        fetch(posn_ref, (i + 1) % 2)

    slot = i % 2
    for k in range(2):
        pltpu.make_async_copy(ys_ref.at[pl.ds(0, tc * rpt)], gbuf.at[slot, k], sem.at[slot]).wait()
    wcol = wcol_ref[...]
    y = (x_ref[...] + wcol[:, 0:1] * _from_token_tiles(gbuf.at[slot, 0], tc)
         + wcol[:, 1:2] * _from_token_tiles(gbuf.at[slot, 1], tc))
    o_ref[...] = _rms(y, fn_ref[...])


def _moe_combine(pos, x2, wcol, final_g, ys, tc):
    t, d = x2.shape
    n = t // tc
    rpt = TOKEN_TILE_ROWS
    return pl.pallas_call(
        _combine_body,
        grid=(n,),
        in_specs=[pl.BlockSpec((N_EXPERTS, tc), lambda i: (0, i)),
                  pl.BlockSpec((N_EXPERTS, tc), lambda i: (0, jnp.minimum(i + 1, n - 1))),
                  pl.BlockSpec((tc, d), lambda i: (i, 0)),
                  pl.BlockSpec((tc, LANES), lambda i: (i, 0)),
                  _const_spec((1, d)),
                  pl.BlockSpec(memory_space=pl.ANY)],
        out_specs=pl.BlockSpec((tc, d), lambda i: (i, 0)),
        out_shape=jax.ShapeDtypeStruct((t, d), F32),
        scratch_shapes=[pltpu.VMEM((2, 2, tc * rpt, LANES), F32),
                        pltpu.SMEM((N_EXPERTS, tc), jnp.int32),
                        pltpu.SemaphoreType.DMA((2,)), pltpu.SemaphoreType.DMA],
        compiler_params=_params("arbitrary"),
        name="moe_combine_norm",
    )(pos, pos, x2, wcol, final_g, ys)


def _tile_schedule(counts, tm, cap, nt):
    ntile = (counts + tm - 1) // tm
    ends = jnp.cumsum(ntile)
    n_used = ends[-1]
    i = jnp.minimum(jnp.arange(nt, dtype=jnp.int32), n_used - 1)
    te = jnp.sum((i[:, None] >= ends[None, :]).astype(jnp.int32), axis=1)
    tb = te * (cap // tm) + i - (ends - ntile)[te]
    return te.astype(jnp.int32), tb.astype(jnp.int32), n_used.reshape(1).astype(jnp.int32)


def _row(v):
    return v.reshape(1, -1).astype(F32)


def _pad_cols(w, n):
    return jnp.pad(w, ((0, 0), (0, n - w.shape[1])))


def _conformer_layer(x3, mix_g, ffn_g, pw1_w, pw1_b, dw_w, dw_b, ln_g, ln_b, pw2_w, pw2_b,
                     w_gate, w_up, w_down):
    bsz, s, d = x3.shape
    t = bsz * s
    tm = min(512, t)
    ts = min(512, s)
    u = _pw1_glu(x3.reshape(t, d), _row(mix_g), pw1_w.astype(BF16), _row(pw1_b), tm)
    x3 = _conf_conv(u.reshape(bsz, s, d), x3, dw_w, _row(dw_b), _row(ln_g), _row(ln_b),
                    pw2_w.astype(BF16), _row(pw2_b), ts)
    x2 = _ffn(x3.reshape(t, d), _row(ffn_g), w_gate.astype(BF16), w_up.astype(BF16),
              w_down.astype(BF16), tm, 256)
    return x2.reshape(bsz, s, d)


def _deltanet_moe_layer(x3, mix_g, ffn_g, w_in, conv_w, a_log, dt_bias, o_norm, w_out,
                        router, e_gate, e_up, e_down, final_g):
    bsz, s, d = x3.shape
    t = bsz * s
    tm = min(512, t)
    ts = min(512, s)
    nmain = 4 * d
    qkv, z, gates = _in_proj(x3.reshape(t, d), _row(mix_g), w_in[:, :nmain].astype(BF16),
                             _pad_cols(w_in[:, nmain:], LANES).astype(BF16), tm)
    nq = qkv.shape[1]
    zero16 = jnp.zeros((2 * N_HEADS,), F32)
    alog_row = _row(_pad_cols(jnp.concatenate([zero16, a_log.reshape(-1)])[None], LANES))
    dtb_row = _row(_pad_cols(jnp.concatenate([zero16, dt_bias.reshape(-1)])[None], LANES))
    w, kq, kd, p, egl = _gdn_prep(qkv.reshape(bsz, s, nq), gates.reshape(bsz, s, LANES),
                                  conv_w, alog_row, dtb_row, min(256, s))
    nbh = bsz * N_HEADS
    nck = s // CHUNK
    o_f, o_b = _gdn_scan(w.reshape(2, nbh, s, HEAD_DIM), kq.reshape(2, nbh, nck, 2 * CHUNK, HEAD_DIM),
                         kd.reshape(2, nbh, s, HEAD_DIM), p.reshape(2, nbh, s, CHUNK),
                         egl.reshape(2, nbh, nck, 1, HEAD_DIM))
    cap = t + tm
    x2, pos, wcol, cnt, xs = _gdn_out(o_f.reshape(bsz, N_HEADS, s, HEAD_DIM), o_b.reshape(bsz, N_HEADS, s, HEAD_DIM),
                                      z, x3.reshape(t, d), _row(o_norm), w_out.astype(BF16), _row(ffn_g),
                                      router.T, ts, tm, cap)
    te, tb, n_used = _tile_schedule(cnt[:, 0], tm, cap, 2 * t // tm + N_EXPERTS)
    dffe = e_gate.shape[2]
    tf = dffe // 2 if (dffe // 2) % 256 == 0 else dffe
    ys = _moe_grouped(xs, te, tb, n_used, e_gate.astype(BF16), e_up.astype(BF16), e_down.astype(BF16),
                      tm, tf, 256)
    out = _moe_combine(pos, x2, wcol, _row(final_g), ys, ts)
    return out.reshape(bsz, s, d)


def kernel(x, mix_norm, ffn_norm, cf_pw1_w, cf_pw1_b, cf_dw_w, cf_dw_b, cf_ln_g, cf_ln_b, cf_pw2_w, cf_pw2_b, ffn_w_gate, ffn_w_up, ffn_w_down, gdn_w_in, gdn_conv_w, gdn_a_log, gdn_dt_bias, gdn_o_norm, gdn_w_out, moe_router, moe_w_gate, moe_w_up, moe_w_down, final_norm):
    x = _conformer_layer(x, mix_norm[0], ffn_norm[0], cf_pw1_w[0], cf_pw1_b[0], cf_dw_w[0], cf_dw_b[0],
                         cf_ln_g[0], cf_ln_b[0], cf_pw2_w[0], cf_pw2_b[0],
                         ffn_w_gate[0], ffn_w_up[0], ffn_w_down[0])
    return _deltanet_moe_layer(x, mix_norm[1], ffn_norm[1], gdn_w_in[0], gdn_conv_w[0], gdn_a_log[0],
                               gdn_dt_bias[0], gdn_o_norm[0], gdn_w_out[0], moe_router[0],
                               moe_w_gate[0], moe_w_up[0], moe_w_down[0], final_norm)
```

```python
import functools

import jax
import jax.numpy as jnp
from jax import lax
from jax.experimental import pallas as pl
from jax.experimental.pallas import tpu as pltpu

F32 = jnp.float32
BF16 = jnp.bfloat16

RMS_EPS = 1e-6
LN_EPS = 1e-5
L2_EPS = 1e-6
N_HEADS = 8
HEAD_DIM = 128
CHUNK = 64
N_EXPERTS = 8
LANES = 128
CONV_HALO = 16
SHORT_HALO = 8
VMEM_LIMIT_BYTES = 56 * 1024 * 1024


def _params(*sem):
    return pltpu.CompilerParams(dimension_semantics=sem, vmem_limit_bytes=VMEM_LIMIT_BYTES)


def _const_spec(shape):
    nd = len(shape)
    return pl.BlockSpec(shape, lambda *_: (0,) * nd, pipeline_mode=pl.Buffered(1))


def _rms(x, g):
    return x * lax.rsqrt(jnp.mean(x * x, axis=-1, keepdims=True) + RMS_EPS) * g


def _silu(x):
    return x * jax.nn.sigmoid(x)


def _mm(a, b):
    return jnp.dot(a.astype(BF16), b.astype(BF16), preferred_element_type=F32)


def _mm_nt(a, b):
    return lax.dot_general(a.astype(BF16), b.astype(BF16), (((1,), (1,)), ((), ())),
                           preferred_element_type=F32)


def _mm_tn(a, b):
    return lax.dot_general(a.astype(BF16), b.astype(BF16), (((0,), (0,)), ((), ())),
                           preferred_element_type=F32)


def _mm_exact(a, b):
    return jnp.dot(a, b, preferred_element_type=F32, precision=lax.Precision.HIGHEST)


def _conf_mixer_body(x_ref, xp_ref, xn_ref, g_ref, w1_ref, b1_ref, dw_ref, dwb_ref, lng_ref, lnb_ref,
                     w2_ref, b2_ref, o_ref, xe_ref, ext_ref, cv_ref, *, nc):
    s = pl.program_id(1)
    ns = pl.num_programs(1)
    ts, d = cv_ref.shape
    width = dw_ref.shape[0]
    base = CONV_HALO - width // 2
    span = ts + 8 * ((base + width - 1) // 8)
    xe_ref[0:CONV_HALO, :] = xp_ref[0]
    xe_ref[CONV_HALO:CONV_HALO + ts, :] = x_ref[0]
    xe_ref[CONV_HALO + ts:, :] = xn_ref[0]
    h = _rms(xe_ref[...], g_ref[...]).astype(BF16)
    for c in range(d // nc):
        a = _mm(h, w1_ref[:, c * nc:(c + 1) * nc]) + b1_ref[:, c * nc:(c + 1) * nc]
        b = _mm(h, w1_ref[:, d + c * nc:d + (c + 1) * nc]) + b1_ref[:, d + c * nc:d + (c + 1) * nc]
        ext_ref[0, :, c * nc:(c + 1) * nc] = a * jax.nn.sigmoid(b)

    @pl.when(s == 0)
    def _():
        ext_ref[0, 0:CONV_HALO, :] = jnp.zeros((CONV_HALO, d), F32)

    @pl.when(s == ns - 1)
    def _():
        ext_ref[0, CONV_HALO + ts:, :] = jnp.zeros((CONV_HALO, d), F32)

    for p in range(1, 8):
        ext_ref[p, 0:span, :] = ext_ref[0, pl.ds(p, span), :]
    rb = 128
    for c in range(d // LANES):
        cs = slice(c * LANES, (c + 1) * LANES)
        for r in range(ts // rb):
            acc = jnp.zeros((rb, LANES), F32)
            for k in range(width):
                off = base + k
                acc = acc + dw_ref[k:k + 1, cs] * ext_ref[off % 8, pl.ds(r * rb + 8 * (off // 8), rb), cs]
            cv_ref[r * rb:(r + 1) * rb, cs] = acc + dwb_ref[:, cs]
    y = cv_ref[...]
    mu = jnp.mean(y, axis=-1, keepdims=True)
    yc = y - mu
    yn = yc * lax.rsqrt(jnp.mean(yc * yc, axis=-1, keepdims=True) + LN_EPS) * lng_ref[...] + lnb_ref[...]
    o_ref[0] = x_ref[0] + _mm(_silu(yn), w2_ref[...]) + b2_ref[...]


def _conf_mixer(x3, g, w1, b1, dw_w, dw_b, ln_g, ln_b, w2, b2, ts):
    bsz, s, d = x3.shape
    width = dw_w.shape[0]
    hb = ts // CONV_HALO
    nhb = s // CONV_HALO
    return pl.pallas_call(
        functools.partial(_conf_mixer_body, nc=256),
        grid=(bsz, s // ts),
        in_specs=[pl.BlockSpec((1, ts, d), lambda b, i: (b, i, 0)),
                  pl.BlockSpec((1, CONV_HALO, d), lambda b, i: (b, jnp.maximum(i * hb - 1, 0), 0)),
                  pl.BlockSpec((1, CONV_HALO, d), lambda b, i: (b, jnp.minimum((i + 1) * hb, nhb - 1), 0)),
                  _const_spec((1, d)), _const_spec((d, 2 * d)), _const_spec((1, 2 * d)),
                  _const_spec((width, d)), _const_spec((1, d)), _const_spec((1, d)), _const_spec((1, d)),
                  _const_spec((d, d)), _const_spec((1, d))],
        out_specs=pl.BlockSpec((1, ts, d), lambda b, i: (b, i, 0)),
        out_shape=jax.ShapeDtypeStruct((bsz, s, d), F32),
        scratch_shapes=[pltpu.VMEM((ts + 2 * CONV_HALO, d), F32),
                        pltpu.VMEM((8, ts + 2 * CONV_HALO, d), F32), pltpu.VMEM((ts, d), F32)],
        compiler_params=_params("parallel", "parallel"),
        name="conf_mixer",
    )(x3, x3, x3, g, w1, b1, dw_w, dw_b, ln_g, ln_b, w2, b2)


def _ffn_body(x_ref, g_ref, wg_ref, wu_ref, wd_ref, o_ref, *, fc):
    x = x_ref[...]
    h = _rms(x, g_ref[...]).astype(BF16)
    dff = wg_ref.shape[1]
    acc = x
    for c in range(dff // fc):
        cs = slice(c * fc, (c + 1) * fc)
        a = _silu(_mm(h, wg_ref[:, cs])) * _mm(h, wu_ref[:, cs])
        acc = acc + _mm(a, wd_ref[cs, :])
    o_ref[...] = acc


def _ffn(x2d, g, wg, wu, wd, tm, fc):
    t, d = x2d.shape
    dff = wg.shape[1]
    return pl.pallas_call(
        functools.partial(_ffn_body, fc=fc),
        grid=(t // tm,),
        in_specs=[pl.BlockSpec((tm, d), lambda i: (i, 0)), _const_spec((1, d)),
                  _const_spec((d, dff)), _const_spec((d, dff)), _const_spec((dff, d))],
        out_specs=pl.BlockSpec((tm, d), lambda i: (i, 0)),
        out_shape=jax.ShapeDtypeStruct((t, d), F32),
        compiler_params=_params("parallel"),
        name="dense_swiglu",
    )(x2d, g, wg, wu, wd)


def _in_proj_body(x_ref, xp_ref, xn_ref, g_ref, w_ref, wgate_ref, cw_ref, qkv_ref, z_ref, gates_ref,
                  xe_ref, pe_ref, *, nc):
    s = pl.program_id(1)
    ns = pl.num_programs(1)
    ts, nq = qkv_ref.shape[1:]
    d = x_ref.shape[2]
    kw = cw_ref.shape[0]
    dk = N_HEADS * HEAD_DIM
    xe_ref[0:SHORT_HALO, :] = jnp.where(s > 0, xp_ref[0], 0.0)
    xe_ref[SHORT_HALO:SHORT_HALO + ts, :] = x_ref[0]
    xe_ref[SHORT_HALO + ts:, :] = jnp.where(s < ns - 1, xn_ref[0], 0.0)
    h = _rms(xe_ref[...], g_ref[...]).astype(BF16)
    hm = h[SHORT_HALO:SHORT_HALO + ts]
    gates_ref[0] = _mm(hm, wgate_ref[...])
    z_ref[0] = _mm(hm, w_ref[:, nq:])
    base = SHORT_HALO - kw // 2
    for j in range(nq // nc):
        pe_ref[j % 2] = _mm(h, w_ref[:, j * nc:(j + 1) * nc])
        for c in range(nc // LANES):
            col = j * nc + c * LANES
            acc = jnp.zeros((ts, LANES), F32)
            for k in range(kw):
                acc = acc + (cw_ref[k:k + 1, col:col + LANES]
                             * pe_ref[j % 2, pl.ds(base + k, ts), c * LANES:(c + 1) * LANES])
            a = _silu(acc)
            if col < 2 * dk:
                a = a * lax.rsqrt(jnp.sum(a * a, axis=-1, keepdims=True) + L2_EPS)
            if col < dk:
                a = a * (HEAD_DIM ** -0.5)
            qkv_ref[0, :, col:col + LANES] = a


def _in_proj(x3, g, w_main, w_gate, conv_w, ts):
    bsz, s, d = x3.shape
    nmain = w_main.shape[1]
    nq = nmain - d
    kw = conv_w.shape[0]
    hb = ts // SHORT_HALO
    nhb = s // SHORT_HALO
    nc = 512
    return pl.pallas_call(
        functools.partial(_in_proj_body, nc=nc),
        grid=(bsz, s // ts),
        in_specs=[pl.BlockSpec((1, ts, d), lambda b, i: (b, i, 0)),
                  pl.BlockSpec((1, SHORT_HALO, d), lambda b, i: (b, jnp.maximum(i * hb - 1, 0), 0)),
                  pl.BlockSpec((1, SHORT_HALO, d), lambda b, i: (b, jnp.minimum((i + 1) * hb, nhb - 1), 0)),
                  _const_spec((1, d)), _const_spec((d, nmain)), _const_spec((d, LANES)),
                  _const_spec((kw, nq))],
        out_specs=[pl.BlockSpec((1, ts, nq), lambda b, i: (b, i, 0)),
                   pl.BlockSpec((1, ts, d), lambda b, i: (b, i, 0)),
                   pl.BlockSpec((1, ts, LANES), lambda b, i: (b, i, 0))],
        out_shape=[jax.ShapeDtypeStruct((bsz, s, nq), F32), jax.ShapeDtypeStruct((bsz, s, d), F32),
                   jax.ShapeDtypeStruct((bsz, s, LANES), F32)],
        scratch_shapes=[pltpu.VMEM((ts + 2 * SHORT_HALO, d), F32),
                        pltpu.VMEM((2, ts + 2 * SHORT_HALO, nc), F32)],
        compiler_params=_params("parallel", "parallel"),
        name="gdn_in_proj",
    )(x3, x3, x3, g, w_main, w_gate, conv_w)


def _tri_inverse_many(mats):
    c = CHUNK
    ri = lax.broadcasted_iota(jnp.int32, (c, c), 0)
    ci = lax.broadcasted_iota(jnp.int32, (c, c), 1)
    eye = (ri == ci).astype(F32)
    blk = (ri // 8) == (ci // 8)
    l0 = [jnp.where(blk, a, 0.0) for a in mats]
    l0b = [a.astype(BF16) for a in l0]
    l2 = [_mm(a, a) for a in l0b]
    l2b = [a.astype(BF16) for a in l2]
    l4 = [_mm(a, a) for a in l2b]
    l3 = [_mm(a, b) for a, b in zip(l0b, l2b)]
    xs = [eye - a + b - t for a, b, t in zip(l0, l2, l3)]
    x4 = [_mm(x, a) for x, a in zip(xs, l4)]
    xs = [x + t for x, t in zip(xs, x4)]
    size = 8
    while size < c:
        off = ((ri // (2 * size)) == (ci // (2 * size))) & ((ri // size) != (ci // size))
        xb = [x.astype(BF16) for x in xs]
        t1 = [_mm(jnp.where(off, a, 0.0), x) for a, x in zip(mats, xb)]
        t2 = [_mm(x, t) for x, t in zip(xb, t1)]
        xs = [x - t for x, t in zip(xs, t2)]
        size *= 2
    return xs


GDN_HEAD_GROUP = 2


def _gdn_prep_body(act_ref, gates_ref, alog_ref, dtb_ref, w_ref, kq_ref, kd_ref, p_ref, egl_ref):
    tg = act_ref.shape[1]
    dk = N_HEADS * HEAD_DIM
    nchunk = tg // CHUNK

    gates = gates_ref[0]
    beta = jax.nn.sigmoid(gates)
    xa = gates + dtb_ref[...]
    softplus = jnp.maximum(xa, 0.0) + jnp.log1p(jnp.exp(-jnp.abs(xa)))
    log_a = -jnp.exp(alog_ref[...]) * softplus
    ri = lax.broadcasted_iota(jnp.int32, (tg, tg), 0)
    ci = lax.broadcasted_iota(jnp.int32, (tg, tg), 1)
    same = (ri // CHUNK) == (ci // CHUNK)
    cum_f = _mm_exact((same & (ci <= ri)).astype(F32), log_a)
    cum_b = _mm_exact((same & (ci >= ri)).astype(F32), log_a)
    lane = lax.broadcasted_iota(jnp.int32, (tg, LANES), 1)
    g_all = jnp.where(lane < 3 * N_HEADS, cum_f, cum_b)
    g_all_t = g_all.T

    r64 = lax.broadcasted_iota(jnp.int32, (CHUNK, CHUNK), 0)
    c64 = lax.broadcasted_iota(jnp.int32, (CHUNK, CHUNK), 1)
    incl = (c64 <= r64, c64 >= r64)
    strict = (c64 < r64, c64 > r64)

    for h0 in range(0, N_HEADS, GDN_HEAD_GROUP):
        heads = range(h0, h0 + GDN_HEAD_GROUP)
        qs, ks, vs = {}, {}, {}
        for h in heads:
            qs[h] = act_ref[0, :, h * HEAD_DIM:(h + 1) * HEAD_DIM]
            ks[h] = act_ref[0, :, dk + h * HEAD_DIM:dk + (h + 1) * HEAD_DIM]
            vs[h] = act_ref[0, :, 2 * dk + h * HEAD_DIM:2 * dk + (h + 1) * HEAD_DIM]
        raw = {}
        for h in heads:
            kb16 = ks[h].astype(BF16)
            qb16 = qs[h].astype(BF16)
            for n in range(nchunk):
                rs = slice(n * CHUNK, (n + 1) * CHUNK)
                raw[h, n] = _mm_nt(jnp.concatenate([kb16[rs], qb16[rs]], axis=0), kb16[rs])
        insts = [(h, dr, n) for h in heads for dr in range(2) for n in range(nchunk)]
        lows, rhss = [], []
        for h, dr, n in insts:
            rs = slice(n * CHUNK, (n + 1) * CHUNK)
            bcol = dr * N_HEADS + h
            gcol = 2 * N_HEADS + dr * N_HEADS + h
            bt = beta[rs, bcol:bcol + 1]
            g_c = g_all[rs, gcol:gcol + 1]
            g_r = g_all_t[gcol:gcol + 1, rs]
            last = n * CHUNK + (CHUNK - 1 if dr == 0 else 0)
            g_l = g_all[last:last + 1, gcol:gcol + 1]
            decay = jnp.where(incl[dr], jnp.exp(jnp.where(incl[dr], g_c - g_r, 0.0)), 0.0)
            eg = jnp.exp(g_c)
            kn = ks[h][rs]
            lows.append(jnp.where(strict[dr], raw[h, n][:CHUNK] * bt * decay, 0.0))
            rhss.append(jnp.concatenate([vs[h][rs] * bt, kn * (bt * eg)], axis=1))
            p_ref[dr, 0, h, rs, :] = (raw[h, n][CHUNK:] * decay).astype(BF16)
            kq_ref[dr, 0, h, n, CHUNK:, :] = (qs[h][rs] * eg).astype(BF16)
            kd_ref[dr, 0, h, rs, :] = (kn * jnp.exp(g_l - g_c)).astype(BF16)
            egl_ref[dr, 0, h, n] = jnp.broadcast_to(jnp.exp(g_l), (1, HEAD_DIM))
        invs = _tri_inverse_many(lows)
        sols = [_mm(inv, rhs) for inv, rhs in zip(invs, rhss)]
        for (h, dr, n), sol in zip(insts, sols):
            rs = slice(n * CHUNK, (n + 1) * CHUNK)
            w_ref[dr, 0, h, rs, :] = sol[:, :HEAD_DIM]
            kq_ref[dr, 0, h, n, :CHUNK, :] = sol[:, HEAD_DIM:].astype(BF16)


def _gdn_prep(qkv3, gates3, alog_row, dtb_row, tg):
    bsz, s, nq = qkv3.shape
    nchunk = tg // CHUNK

    def rows(dtype, last=HEAD_DIM):
        return (jax.ShapeDtypeStruct((2, bsz, N_HEADS, s, last), dtype),
                pl.BlockSpec((2, 1, N_HEADS, tg, last), lambda b, i: (0, b, 0, i, 0)))

    def per_chunk(dtype, r):
        return (jax.ShapeDtypeStruct((2, bsz, N_HEADS, s // CHUNK, r, HEAD_DIM), dtype),
                pl.BlockSpec((2, 1, N_HEADS, nchunk, r, HEAD_DIM), lambda b, i: (0, b, 0, i, 0, 0)))

    outs = [rows(F32), per_chunk(BF16, 2 * CHUNK), rows(BF16), rows(BF16, CHUNK), per_chunk(F32, 1)]
    return pl.pallas_call(
        _gdn_prep_body,
        grid=(bsz, s // tg),
        in_specs=[pl.BlockSpec((1, tg, nq), lambda b, i: (b, i, 0)),
                  pl.BlockSpec((1, tg, LANES), lambda b, i: (b, i, 0)),
                  _const_spec((1, LANES)), _const_spec((1, LANES))],
        out_specs=[o[1] for o in outs],
        out_shape=[o[0] for o in outs],
        compiler_params=_params("parallel", "parallel"),
        name="gdn_prep",
    )(qkv3, gates3, alog_row, dtb_row)


GDN_SCAN_GROUP = 4


def _gdn_scan_body(wf, kqf, kdf, pf, eglf, wb, kqb, kdb, pb, eglb, of_ref, ob_ref, state_ref):
    n = pl.program_id(0)

    @pl.when(n == 0)
    def _():
        state_ref[...] = jnp.zeros_like(state_ref)

    nbh = state_ref.shape[1]
    dirs = ((wf, kqf, kdf, pf, eglf, of_ref), (wb, kqb, kdb, pb, eglb, ob_ref))

    def body(i, carry):
        chains = [(dr, i * GDN_SCAN_GROUP + j) for j in range(GDN_SCAN_GROUP) for dr in range(2)]
        st = [state_ref[dr, b] for dr, b in chains]
        r = [_mm(dirs[dr][1][0, b, 0], s_) for (dr, b), s_ in zip(chains, st)]
        ub = [(dirs[dr][0][0, b] - r_[:CHUNK]).astype(BF16) for (dr, b), r_ in zip(chains, r)]
        pu = [_mm(dirs[dr][3][0, b], u_) for (dr, b), u_ in zip(chains, ub)]
        ku = [_mm_tn(dirs[dr][2][0, b], u_) for (dr, b), u_ in zip(chains, ub)]
        for (dr, b), r_, pu_, ku_, s_ in zip(chains, r, pu, ku, st):
            dirs[dr][5][b] = r_[CHUNK:] + pu_
            state_ref[dr, b] = s_ * dirs[dr][4][0, b, 0] + ku_
        return carry

    lax.fori_loop(0, nbh // GDN_SCAN_GROUP, body, 0)


def _gdn_scan(w, kq, kd, p, egl):
    _, nbh, s, dh = w.shape
    nchunk = s // CHUNK

    def specs(dr):
        def im(n):
            return (dr, 0, n if dr == 0 else nchunk - 1 - n, 0)

        def im5(n):
            return im(n) + (0,)
        return [pl.BlockSpec((1, nbh, CHUNK, dh), im), pl.BlockSpec((1, nbh, 1, 2 * CHUNK, dh), im5),
                pl.BlockSpec((1, nbh, CHUNK, dh), im), pl.BlockSpec((1, nbh, CHUNK, CHUNK), im),
                pl.BlockSpec((1, nbh, 1, 1, dh), im5)]

    o_shape = jax.ShapeDtypeStruct((nbh, s, dh), F32)
    return pl.pallas_call(
        _gdn_scan_body,
        grid=(nchunk,),
        in_specs=specs(0) + specs(1),
        out_specs=[pl.BlockSpec((nbh, CHUNK, dh), lambda n: (0, n, 0)),
                   pl.BlockSpec((nbh, CHUNK, dh), lambda n: (0, nchunk - 1 - n, 0))],
        out_shape=[o_shape, o_shape],
        scratch_shapes=[pltpu.VMEM((2, nbh, dh, dh), F32)],
        compiler_params=_params("arbitrary"),
        name="gdn_scan",
    )(w, kq, kd, p, egl, w, kq, kd, p, egl)


TOKEN_TILE_ROWS = 8
DMA_ISSUE_UNROLL = 8


def _to_token_tiles(dst_ref, val):
    n = val.shape[0]
    for j in range(TOKEN_TILE_ROWS):
        dst_ref[pl.ds(j, n, stride=TOKEN_TILE_ROWS), :] = val[:, j * LANES:(j + 1) * LANES]


def _from_token_tiles(src_ref, n):
    return jnp.concatenate([src_ref[pl.ds(j, n, stride=TOKEN_TILE_ROWS), :] for j in range(TOKEN_TILE_ROWS)],
                           axis=1)


def _gdn_out_body(of_ref, ob_ref, z_ref, x_ref, on_ref, wo_ref, fg_ref, rwt_ref,
                  xo_ref, pos_ref, wcol_ref, cnt_ref, xs_ref,
                  act_ref, tile_ref, zero_ref, carry_ref, pos_smem, cnt_smem, sem, psem, zsem,
                  *, cap, tm):
    g = pl.program_id(0)
    ng = pl.num_programs(0)
    ts = x_ref.shape[0]
    slot = g % 2
    rpt = TOKEN_TILE_ROWS

    def wait_rows(s):
        for _ in range(2):
            pltpu.make_async_copy(tile_ref.at[s], xs_ref.at[pl.ds(0, ts * rpt)], sem.at[s]).wait()

    @pl.when(g == 0)
    def _():
        carry_ref[...] = jnp.zeros_like(carry_ref)

    for h in range(N_HEADS):
        hs = slice(h * HEAD_DIM, (h + 1) * HEAD_DIM)
        o = of_ref[0, h] + ob_ref[0, h]
        o = o * lax.rsqrt(jnp.mean(o * o, axis=-1, keepdims=True) + RMS_EPS) * on_ref[...]
        act_ref[:, hs] = (o * _silu(z_ref[:, hs])).astype(BF16)
    x = x_ref[...] + _mm(act_ref[...], wo_ref[...])
    xo_ref[...] = x
    hn = _rms(x, fg_ref[...])

    logits = lax.dot_general(rwt_ref[...], hn, (((1,), (1,)), ((), ())),
                             preferred_element_type=F32, precision=lax.Precision.HIGHEST)
    eidx = lax.broadcasted_iota(jnp.int32, logits.shape, 0).astype(F32)
    neg = jnp.float32(-jnp.inf)
    m1 = jnp.max(logits, axis=0, keepdims=True)
    i1 = jnp.min(jnp.where(logits == m1, eidx, float(N_EXPERTS)), axis=0, keepdims=True)
    one1 = eidx == i1
    rest = jnp.where(one1, neg, logits)
    m2 = jnp.max(rest, axis=0, keepdims=True)
    i2 = jnp.min(jnp.where(rest == m2, eidx, float(N_EXPERTS)), axis=0, keepdims=True)
    one2 = eidx == i2
    e2 = jnp.exp(m2 - m1)
    w1 = 1.0 / (1.0 + e2)
    w2 = e2 * w1

    chosen = jnp.where(one1 | one2, 1.0, 0.0)
    ri = lax.broadcasted_iota(jnp.int32, (ts, ts), 0)
    ci = lax.broadcasted_iota(jnp.int32, (ts, ts), 1)
    before = jnp.where(ri < ci, 1.0, 0.0).astype(BF16)
    rank = jnp.dot(chosen.astype(BF16), before, preferred_element_type=F32)
    carry = carry_ref[...]
    slot_f = eidx * float(cap) + carry[:, 0:1] + rank
    p1 = jnp.sum(jnp.where(one1, slot_f, 0.0), axis=0, keepdims=True)
    p2 = jnp.sum(jnp.where(one2, slot_f, 0.0), axis=0, keepdims=True)
    carry = carry + jnp.sum(chosen, axis=1, keepdims=True)
    carry_ref[...] = carry
    cnt_ref[...] = carry.astype(jnp.int32)
    row8 = lax.broadcasted_iota(jnp.int32, (N_EXPERTS, ts), 0)
    pos = jnp.where(row8 == 0, p1, jnp.where(row8 == 1, p2, 0.0)).astype(jnp.int32)
    pos_ref[...] = pos
    row128 = lax.broadcasted_iota(jnp.int32, (LANES, ts), 0)
    wcol_ref[...] = jnp.where(row128 == 0, w1, jnp.where(row128 == 1, w2, 0.0)).T

    cp = pltpu.make_async_copy(pos_ref, pos_smem, psem)
    cp.start()
    cp.wait()
    for sl in range(2):
        @pl.when(slot == sl)
        def _(sl=sl):
            @pl.when(g >= 2)
            def _():
                wait_rows(sl)

            _to_token_tiles(tile_ref.at[sl], hn)

            def issue(t, c):
                src = tile_ref.at[sl, pl.ds(t * rpt, rpt)]
                for k in range(2):
                    dst = xs_ref.at[pl.ds(pos_smem[k, t] * rpt, rpt)]
                    pltpu.make_async_copy(src, dst, sem.at[sl]).start()
                return c

            lax.fori_loop(0, ts, issue, 0, unroll=DMA_ISSUE_UNROLL)

    @pl.when(g == ng - 1)
    def _():
        wait_rows(slot)

        @pl.when(g >= 1)
        def _():
            wait_rows(1 - slot)

        zero_ref[...] = jnp.zeros_like(zero_ref)
        cc = pltpu.make_async_copy(cnt_ref, cnt_smem, psem)
        cc.start()
        cc.wait()
        tails = [pltpu.make_async_copy(
            zero_ref, xs_ref.at[pl.ds((e * cap + cnt_smem[e, 0]) * rpt, tm * rpt)], zsem)
            for e in range(N_EXPERTS)]
        for c in tails:
            c.start()
        for c in tails:
            c.wait()


def _gdn_out(o_f, o_b, z2, x2, o_norm, w_out, ffn_g, router_t, ts, tm, cap):
    t, d = x2.shape
    s = o_f.shape[2]
    spb = s // ts
    rpt = TOKEN_TILE_ROWS
    o_spec = pl.BlockSpec((1, N_HEADS, ts, HEAD_DIM), lambda g: (g // spb, 0, g % spb, 0))
    return pl.pallas_call(
        functools.partial(_gdn_out_body, cap=cap, tm=tm),
        grid=(t // ts,),
        in_specs=[o_spec, o_spec,
                  pl.BlockSpec((ts, d), lambda g: (g, 0)), pl.BlockSpec((ts, d), lambda g: (g, 0)),
                  _const_spec((1, HEAD_DIM)), _const_spec((d, d)), _const_spec((1, d)),
                  _const_spec((N_EXPERTS, d))],
        out_specs=[pl.BlockSpec((ts, d), lambda g: (g, 0)),
                   pl.BlockSpec((N_EXPERTS, ts), lambda g: (0, g)),
                   pl.BlockSpec((ts, LANES), lambda g: (g, 0)),
                   pl.BlockSpec((N_EXPERTS, LANES), lambda g: (0, 0)),
                   pl.BlockSpec(memory_space=pl.ANY)],
        out_shape=[jax.ShapeDtypeStruct((t, d), F32),
                   jax.ShapeDtypeStruct((N_EXPERTS, t), jnp.int32),
                   jax.ShapeDtypeStruct((t, LANES), F32),
                   jax.ShapeDtypeStruct((N_EXPERTS, LANES), jnp.int32),
                   jax.ShapeDtypeStruct((N_EXPERTS * cap * rpt, LANES), F32)],
        scratch_shapes=[pltpu.VMEM((ts, d), BF16),
                        pltpu.VMEM((2, ts * rpt, LANES), F32),
                        pltpu.VMEM((tm * rpt, LANES), F32),
                        pltpu.VMEM((N_EXPERTS, LANES), F32),
                        pltpu.SMEM((N_EXPERTS, ts), jnp.int32),
                        pltpu.SMEM((N_EXPERTS, LANES), jnp.int32),
                        pltpu.SemaphoreType.DMA((2,)), pltpu.SemaphoreType.DMA, pltpu.SemaphoreType.DMA],
        compiler_params=_params("arbitrary"),
        name="gdn_out_router",
    )(o_f, o_b, z2, x2, o_norm, w_out, ffn_g, router_t)


def _moe_body(te_ref, tb_ref, nu_ref, xs_ref, wg_ref, wu_ref, wd_ref, ys_ref, xb_ref, acc_ref, *, fc):
    i = pl.program_id(0)
    f = pl.program_id(1)
    tm = xb_ref.shape[0]
    tf = wg_ref.shape[2]

    @pl.when(i < nu_ref[0])
    def _():
        @pl.when(f == 0)
        def _():
            xb_ref[...] = _from_token_tiles(xs_ref, tm).astype(BF16)
            acc_ref[...] = jnp.zeros_like(acc_ref)

        x = xb_ref[...]
        for c in range(tf // fc):
            cs = slice(c * fc, (c + 1) * fc)
            a = _silu(_mm(x, wg_ref[0, :, cs])) * _mm(x, wu_ref[0, :, cs])
            acc_ref[...] += _mm(a, wd_ref[0, cs, :])

        @pl.when(f == pl.num_programs(1) - 1)
        def _():
            _to_token_tiles(ys_ref, acc_ref[...])


def _moe_grouped(xs, tile_expert, tile_block, n_used, wg, wu, wd, tm, tf, fc):
    ne, d, dff = wg.shape
    nt = tile_expert.shape[0]
    nf = dff // tf
    rpt = TOKEN_TILE_ROWS

    def fsel(i, f, nu):
        return jnp.where(i < nu[0], f, nf - 1)

    grid_spec = pltpu.PrefetchScalarGridSpec(
        num_scalar_prefetch=3,
        grid=(nt, nf),
        in_specs=[pl.BlockSpec((tm * rpt, LANES), lambda i, f, te, tb, nu: (tb[i], 0)),
                  pl.BlockSpec((1, d, tf), lambda i, f, te, tb, nu: (te[i], 0, fsel(i, f, nu))),
                  pl.BlockSpec((1, d, tf), lambda i, f, te, tb, nu: (te[i], 0, fsel(i, f, nu))),
                  pl.BlockSpec((1, tf, d), lambda i, f, te, tb, nu: (te[i], fsel(i, f, nu), 0))],
        out_specs=pl.BlockSpec((tm * rpt, LANES), lambda i, f, te, tb, nu: (tb[i], 0)),
        scratch_shapes=[pltpu.VMEM((tm, d), BF16), pltpu.VMEM((tm, d), F32)],
    )
    return pl.pallas_call(
        functools.partial(_moe_body, fc=fc),
        grid_spec=grid_spec,
        out_shape=jax.ShapeDtypeStruct(xs.shape, F32),
        compiler_params=_params("arbitrary", "arbitrary"),
        name="moe_grouped",
    )(tile_expert, tile_block, n_used, xs, wg, wu, wd)


def _combine_body(pos_ref, posn_ref, x_ref, wcol_ref, fn_ref, ys_ref, o_ref, gbuf, pos_smem, sem, psem):
    i = pl.program_id(0)
    n = pl.num_programs(0)
    tc = x_ref.shape[0]
    rpt = TOKEN_TILE_ROWS

    def fetch(p_ref, sl):
        cp = pltpu.make_async_copy(p_ref, pos_smem, psem)
        cp.start()
        cp.wait()

        def issue(t, c):
            for k in range(2):
                src = ys_ref.at[pl.ds(pos_smem[k, t] * rpt, rpt)]
                pltpu.make_async_copy(src, gbuf.at[sl, k, pl.ds(t * rpt, rpt)], sem.at[sl]).start()
            return c

        lax.fori_loop(0, tc, issue, 0, unroll=DMA_ISSUE_UNROLL)

    @pl.when(i == 0)
    def _():
        fetch(pos_ref, 0)

    for sl in range(2):
        @pl.when((i + 1 < n) & ((i + 1) % 2 == sl))
        def _(sl=sl):
            fetch(posn_ref, sl)

    for sl in range(2):
        @pl.when(i % 2 == sl)
        def _(sl=sl):
            for k in range(2):
                pltpu.make_async_copy(ys_ref.at[pl.ds(0, tc * rpt)], gbuf.at[sl, k], sem.at[sl]).wait()
            wcol = wcol_ref[...]
            y = (x_ref[...] + wcol[:, 0:1] * _from_token_tiles(gbuf.at[sl, 0], tc)
                 + wcol[:, 1:2] * _from_token_tiles(gbuf.at[sl, 1], tc))
            o_ref[...] = _rms(y, fn_ref[...])


def _moe_combine(pos, x2, wcol, final_g, ys, tc):
    t, d = x2.shape
    n = t // tc
    rpt = TOKEN_TILE_ROWS
    return pl.pallas_call(
        _combine_body,
        grid=(n,),
        in_specs=[pl.BlockSpec((N_EXPERTS, tc), lambda i: (0, i)),
                  pl.BlockSpec((N_EXPERTS, tc), lambda i: (0, jnp.minimum(i + 1, n - 1))),
                  pl.BlockSpec((tc, d), lambda i: (i, 0)),
                  pl.BlockSpec((tc, LANES), lambda i: (i, 0)),
                  _const_spec((1, d)),
                  pl.BlockSpec(memory_space=pl.ANY)],
        out_specs=pl.BlockSpec((tc, d), lambda i: (i, 0)),
        out_shape=jax.ShapeDtypeStruct((t, d), F32),
        scratch_shapes=[pltpu.VMEM((2, 2, tc * rpt, LANES), F32),
                        pltpu.SMEM((N_EXPERTS, tc), jnp.int32),
                        pltpu.SemaphoreType.DMA((2,)), pltpu.SemaphoreType.DMA],
        compiler_params=_params("arbitrary"),
        name="moe_combine_norm",
    )(pos, pos, x2, wcol, final_g, ys)


def _tile_schedule(counts, tm, cap, nt):
    ntile = (counts + tm - 1) // tm
    ends = jnp.cumsum(ntile)
    n_used = ends[-1]
    i = jnp.minimum(jnp.arange(nt, dtype=jnp.int32), jnp.maximum(n_used - 1, 0))
    te = jnp.sum((i[:, None] >= ends[None, :]).astype(jnp.int32), axis=1)
    tb = te * (cap // tm) + i - (ends - ntile)[te]
    return te.astype(jnp.int32), tb.astype(jnp.int32), n_used.reshape(1).astype(jnp.int32)


def _row(v):
    return v.reshape(1, -1).astype(F32)


def _pad_cols(w, n):
    return jnp.pad(w, ((0, 0), (0, n - w.shape[1])))


def _conformer_layer(x3, mix_g, ffn_g, pw1_w, pw1_b, dw_w, dw_b, ln_g, ln_b, pw2_w, pw2_b,
                     w_gate, w_up, w_down):
    bsz, s, d = x3.shape
    t = bsz * s
    tm = min(512, t)
    ts = min(512, s)
    x3 = _conf_mixer(x3, _row(mix_g), pw1_w.astype(BF16), _row(pw1_b), dw_w, _row(dw_b), _row(ln_g),
                     _row(ln_b), pw2_w.astype(BF16), _row(pw2_b), ts)
    x2 = _ffn(x3.reshape(t, d), _row(ffn_g), w_gate.astype(BF16), w_up.astype(BF16),
              w_down.astype(BF16), tm, 256)
    return x2.reshape(bsz, s, d)


def _deltanet_moe_layer(x3, mix_g, ffn_g, w_in, conv_w, a_log, dt_bias, o_norm, w_out,
                        router, e_gate, e_up, e_down, final_g):
    bsz, s, d = x3.shape
    t = bsz * s
    tm = min(512, t)
    ts = min(512, s)
    nmain = 4 * d
    qkv, z, gates = _in_proj(x3, _row(mix_g), w_in[:, :nmain].astype(BF16),
                             _pad_cols(w_in[:, nmain:], LANES).astype(BF16), conv_w, ts)
    z = z.reshape(t, d)
    zero16 = jnp.zeros((2 * N_HEADS,), F32)
    alog_row = _row(_pad_cols(jnp.concatenate([zero16, a_log.reshape(-1)])[None], LANES))
    dtb_row = _row(_pad_cols(jnp.concatenate([zero16, dt_bias.reshape(-1)])[None], LANES))
    w, kq, kd, p, egl = _gdn_prep(qkv, gates, alog_row, dtb_row, min(256, s))
    nbh = bsz * N_HEADS
    nck = s // CHUNK
    o_f, o_b = _gdn_scan(w.reshape(2, nbh, s, HEAD_DIM), kq.reshape(2, nbh, nck, 2 * CHUNK, HEAD_DIM),
                         kd.reshape(2, nbh, s, HEAD_DIM), p.reshape(2, nbh, s, CHUNK),
                         egl.reshape(2, nbh, nck, 1, HEAD_DIM))
    cap = t + tm
    x2, pos, wcol, cnt, xs = _gdn_out(o_f.reshape(bsz, N_HEADS, s, HEAD_DIM), o_b.reshape(bsz, N_HEADS, s, HEAD_DIM),
                                      z, x3.reshape(t, d), _row(o_norm), w_out.astype(BF16), _row(ffn_g),
                                      router.T, ts, tm, cap)
    te, tb, n_used = _tile_schedule(cnt[:, 0], tm, cap, 2 * t // tm + N_EXPERTS)
    dffe = e_gate.shape[2]
    tf = dffe // 2 if (dffe // 2) % 256 == 0 else dffe
    ys = _moe_grouped(xs, te, tb, n_used, e_gate.astype(BF16), e_up.astype(BF16), e_down.astype(BF16),
                      tm, tf, 256)
    out = _moe_combine(pos, x2, wcol, _row(final_g), ys, ts)
    return out.reshape(bsz, s, d)


def kernel(x, mix_norm, ffn_norm, cf_pw1_w, cf_pw1_b, cf_dw_w, cf_dw_b, cf_ln_g, cf_ln_b, cf_pw2_w, cf_pw2_b, ffn_w_gate, ffn_w_up, ffn_w_down, gdn_w_in, gdn_conv_w, gdn_a_log, gdn_dt_bias, gdn_o_norm, gdn_w_out, moe_router, moe_w_gate, moe_w_up, moe_w_down, final_norm):
    x = _conformer_layer(x, mix_norm[0], ffn_norm[0], cf_pw1_w[0], cf_pw1_b[0], cf_dw_w[0], cf_dw_b[0],
                         cf_ln_g[0], cf_ln_b[0], cf_pw2_w[0], cf_pw2_b[0],
                         ffn_w_gate[0], ffn_w_up[0], ffn_w_down[0])
    return _deltanet_moe_layer(x, mix_norm[1], ffn_norm[1], gdn_w_in[0], gdn_conv_w[0], gdn_a_log[0],
                               gdn_dt_bias[0], gdn_o_norm[0], gdn_w_out[0], moe_router[0],
                               moe_w_gate[0], moe_w_up[0], moe_w_down[0], final_norm)
```

```python
import functools

import jax
import jax.numpy as jnp
from jax import lax
from jax.experimental import pallas as pl
from jax.experimental.pallas import tpu as pltpu

F32 = jnp.float32
BF16 = jnp.bfloat16

RMS_EPS = 1e-6
LN_EPS = 1e-5
L2_EPS = 1e-6
N_HEADS = 8
HEAD_DIM = 128
CHUNK = 64
N_EXPERTS = 8
LANES = 128
CONV_HALO = 16
SHORT_HALO = 8
VMEM_LIMIT_BYTES = 56 * 1024 * 1024


def _params(*sem):
    return pltpu.CompilerParams(dimension_semantics=sem, vmem_limit_bytes=VMEM_LIMIT_BYTES)


def _const_spec(shape):
    nd = len(shape)
    return pl.BlockSpec(shape, lambda *_: (0,) * nd, pipeline_mode=pl.Buffered(1))


def _rms(x, g):
    return x * lax.rsqrt(jnp.mean(x * x, axis=-1, keepdims=True) + RMS_EPS) * g


def _silu(x):
    return x * jax.nn.sigmoid(x)


def _mm(a, b):
    return jnp.dot(a.astype(BF16), b.astype(BF16), preferred_element_type=F32)


def _mm_nt(a, b):
    return lax.dot_general(a.astype(BF16), b.astype(BF16), (((1,), (1,)), ((), ())),
                           preferred_element_type=F32)


def _mm_tn(a, b):
    return lax.dot_general(a.astype(BF16), b.astype(BF16), (((0,), (0,)), ((), ())),
                           preferred_element_type=F32)


def _mm_exact(a, b):
    return jnp.dot(a, b, preferred_element_type=F32, precision=lax.Precision.HIGHEST)


def _conf_mixer_body(x_ref, xp_ref, xn_ref, g_ref, w1_ref, b1_ref, dw_ref, dwb_ref, lng_ref, lnb_ref,
                     w2_ref, b2_ref, o_ref, xe_ref, ext_ref, cv_ref, *, nc):
    s = pl.program_id(1)
    ns = pl.num_programs(1)
    ts, d = cv_ref.shape
    width = dw_ref.shape[0]
    base = CONV_HALO - width // 2
    span = ts + 8 * ((base + width - 1) // 8)
    xe_ref[0:CONV_HALO, :] = xp_ref[0]
    xe_ref[CONV_HALO:CONV_HALO + ts, :] = x_ref[0]
    xe_ref[CONV_HALO + ts:, :] = xn_ref[0]
    h = _rms(xe_ref[...], g_ref[...]).astype(BF16)
    for c in range(d // nc):
        a = _mm(h, w1_ref[:, c * nc:(c + 1) * nc]) + b1_ref[:, c * nc:(c + 1) * nc]
        b = _mm(h, w1_ref[:, d + c * nc:d + (c + 1) * nc]) + b1_ref[:, d + c * nc:d + (c + 1) * nc]
        ext_ref[0, :, c * nc:(c + 1) * nc] = a * jax.nn.sigmoid(b)

    @pl.when(s == 0)
    def _():
        ext_ref[0, 0:CONV_HALO, :] = jnp.zeros((CONV_HALO, d), F32)

    @pl.when(s == ns - 1)
    def _():
        ext_ref[0, CONV_HALO + ts:, :] = jnp.zeros((CONV_HALO, d), F32)

    for p in range(1, 8):
        ext_ref[p, 0:span, :] = ext_ref[0, pl.ds(p, span), :]
    rb = 128
    for c in range(d // LANES):
        cs = slice(c * LANES, (c + 1) * LANES)
        for r in range(ts // rb):
            acc = jnp.zeros((rb, LANES), F32)
            for k in range(width):
                off = base + k
                acc = acc + dw_ref[k:k + 1, cs] * ext_ref[off % 8, pl.ds(r * rb + 8 * (off // 8), rb), cs]
            cv_ref[r * rb:(r + 1) * rb, cs] = acc + dwb_ref[:, cs]
    y = cv_ref[...]
    mu = jnp.mean(y, axis=-1, keepdims=True)
    yc = y - mu
    yn = yc * lax.rsqrt(jnp.mean(yc * yc, axis=-1, keepdims=True) + LN_EPS) * lng_ref[...] + lnb_ref[...]
    o_ref[0] = x_ref[0] + _mm(_silu(yn), w2_ref[...]) + b2_ref[...]


def _conf_mixer(x3, g, w1, b1, dw_w, dw_b, ln_g, ln_b, w2, b2, ts):
    bsz, s, d = x3.shape
    width = dw_w.shape[0]
    hb = ts // CONV_HALO
    nhb = s // CONV_HALO
    return pl.pallas_call(
        functools.partial(_conf_mixer_body, nc=256),
        grid=(bsz, s // ts),
        in_specs=[pl.BlockSpec((1, ts, d), lambda b, i: (b, i, 0)),
                  pl.BlockSpec((1, CONV_HALO, d), lambda b, i: (b, jnp.maximum(i * hb - 1, 0), 0)),
                  pl.BlockSpec((1, CONV_HALO, d), lambda b, i: (b, jnp.minimum((i + 1) * hb, nhb - 1), 0)),
                  _const_spec((1, d)), _const_spec((d, 2 * d)), _const_spec((1, 2 * d)),
                  _const_spec((width, d)), _const_spec((1, d)), _const_spec((1, d)), _const_spec((1, d)),
                  _const_spec((d, d)), _const_spec((1, d))],
        out_specs=pl.BlockSpec((1, ts, d), lambda b, i: (b, i, 0)),
        out_shape=jax.ShapeDtypeStruct((bsz, s, d), F32),
        scratch_shapes=[pltpu.VMEM((ts + 2 * CONV_HALO, d), F32),
                        pltpu.VMEM((8, ts + 2 * CONV_HALO, d), F32), pltpu.VMEM((ts, d), F32)],
        compiler_params=_params("parallel", "parallel"),
        name="conf_mixer",
    )(x3, x3, x3, g, w1, b1, dw_w, dw_b, ln_g, ln_b, w2, b2)


def _ffn_body(x_ref, g_ref, wg_ref, wu_ref, wd_ref, o_ref, *, fc):
    x = x_ref[...]
    h = _rms(x, g_ref[...]).astype(BF16)
    dff = wg_ref.shape[1]
    acc = x
    for c in range(dff // fc):
        cs = slice(c * fc, (c + 1) * fc)
        a = _silu(_mm(h, wg_ref[:, cs])) * _mm(h, wu_ref[:, cs])
        acc = acc + _mm(a, wd_ref[cs, :])
    o_ref[...] = acc


def _ffn(x2d, g, wg, wu, wd, tm, fc):
    t, d = x2d.shape
    dff = wg.shape[1]
    return pl.pallas_call(
        functools.partial(_ffn_body, fc=fc),
        grid=(t // tm,),
        in_specs=[pl.BlockSpec((tm, d), lambda i: (i, 0)), _const_spec((1, d)),
                  _const_spec((d, dff)), _const_spec((d, dff)), _const_spec((dff, d))],
        out_specs=pl.BlockSpec((tm, d), lambda i: (i, 0)),
        out_shape=jax.ShapeDtypeStruct((t, d), F32),
        compiler_params=_params("parallel"),
        name="dense_swiglu",
    )(x2d, g, wg, wu, wd)


PAIR = 2 * CHUNK


def _block_diag(yb, mk_ref):
    return jnp.concatenate([yb * mk_ref[MASK_LEFT].astype(BF16), yb * mk_ref[MASK_RIGHT].astype(BF16)], axis=0)


def _tri_inverse_pairs(mats, mk_ref):
    def bd(vals):
        return [_block_diag(v.astype(BF16), mk_ref) for v in vals]

    l0 = [a * mk_ref[MASK_BLOCK8] for a in mats]
    l0b = [a.astype(BF16) for a in l0]
    l2 = [_mm(a, d) for a, d in zip(l0b, bd(l0b))]
    l2b = [a.astype(BF16) for a in l2]
    l2d = bd(l2b)
    l4 = [_mm(a, d) for a, d in zip(l2b, l2d)]
    l3 = [_mm(a, d) for a, d in zip(l0b, l2d)]
    xs = [mk_ref[MASK_EYE] - a + b - t for a, b, t in zip(l0, l2, l3)]
    x4 = [_mm(x, d) for x, d in zip(xs, bd(l4))]
    xs = [x + t for x, t in zip(xs, x4)]
    for level in range(MASK_MERGE0, MASK_MERGE0 + 3):
        xb = [x.astype(BF16) for x in xs]
        t1 = [_mm(a * mk_ref[level], d) for a, d in zip(mats, bd(xb))]
        t2 = [_mm(x, d) for x, d in zip(xb, bd(t1))]
        xs = [x - t for x, t in zip(xs, t2)]
    return xs


(MASK_INCL, MASK_INCL_L, MASK_INCL_R, MASK_STRICT_L, MASK_STRICT_R) = (0, 2, 4, 6, 8)
MASK_EYE, MASK_BLOCK8, MASK_MERGE0, MASK_LEFT, MASK_RIGHT = 10, 11, 12, 15, 16
N_MASKS = 17


def _chunk_masks():
    ri = lax.broadcasted_iota(jnp.int32, (CHUNK, PAIR), 0)
    li = lax.broadcasted_iota(jnp.int32, (CHUNK, PAIR), 1)
    ci = li % CHUNK
    left, right = li < CHUNK, li >= CHUNK
    incl = [ci <= ri, ci >= ri]
    strict = [ci < ri, ci > ri]
    masks = (incl + [m & left for m in incl] + [m & right for m in incl]
             + [m & left for m in strict] + [m & right for m in strict]
             + [ci == ri, (ri // 8) == (ci // 8)])
    size = 8
    while size < CHUNK:
        masks.append(((ri // (2 * size)) == (ci // (2 * size))) & ((ri // size) != (ci // size)))
        size *= 2
    masks += [left, right]
    assert len(masks) == N_MASKS
    return jnp.stack(masks).astype(F32)


GDN_HEAD_GROUP = 4


def _gdn_front_body(x_ref, xp_ref, xn_ref, g_ref, w_ref, wgate_ref, cw_ref, alog_ref, dtb_ref, mk_ref,
                    z_ref, w_out_ref, kq_ref, kd_ref, p_ref, egl_ref, xe_ref, pe_ref):
    s = pl.program_id(1)
    ns = pl.num_programs(1)
    tg, d = z_ref.shape[1:]
    kw = cw_ref.shape[0]
    dk = N_HEADS * HEAD_DIM
    npair = tg // PAIR
    gw = GDN_HEAD_GROUP * HEAD_DIM
    ngroup = N_HEADS // GDN_HEAD_GROUP
    base = SHORT_HALO - kw // 2

    xe_ref[0:SHORT_HALO, :] = jnp.where(s > 0, xp_ref[0], 0.0)
    xe_ref[SHORT_HALO:SHORT_HALO + tg, :] = x_ref[0]
    xe_ref[SHORT_HALO + tg:, :] = jnp.where(s < ns - 1, xn_ref[0], 0.0)
    h = _rms(xe_ref[...], g_ref[...]).astype(BF16)
    hm = h[SHORT_HALO:SHORT_HALO + tg]

    gates = _mm(hm, wgate_ref[...])
    beta = jax.nn.sigmoid(gates)
    xa = gates + dtb_ref[...]
    softplus = jnp.maximum(xa, 0.0) + jnp.log1p(jnp.exp(-jnp.abs(xa)))
    log_a = -jnp.exp(alog_ref[...]) * softplus
    ri = lax.broadcasted_iota(jnp.int32, (tg, tg), 0)
    ci = lax.broadcasted_iota(jnp.int32, (tg, tg), 1)
    same = (ri // CHUNK) == (ci // CHUNK)
    cum_f = _mm_exact((same & (ci <= ri)).astype(F32), log_a)
    cum_b = _mm_exact((same & (ci >= ri)).astype(F32), log_a)
    lane = lax.broadcasted_iota(jnp.int32, (tg, LANES), 1)
    g_all = jnp.where(lane < 3 * N_HEADS, cum_f, cum_b)
    g_all_t = g_all.T
    beta_t = beta.T

    def project(grp):
        out = []
        for part in range(3):
            col0 = part * dk + grp * gw
            pe_ref[grp % 2, part] = _mm(h, w_ref[:, col0:col0 + gw])
            heads = []
            for c in range(GDN_HEAD_GROUP):
                col = col0 + c * LANES
                acc = jnp.zeros((tg, LANES), F32)
                for k in range(kw):
                    acc = acc + (cw_ref[k:k + 1, col:col + LANES]
                                 * pe_ref[grp % 2, part, pl.ds(base + k, tg), c * LANES:(c + 1) * LANES])
                a = _silu(acc)
                if part < 2:
                    a = a * lax.rsqrt(jnp.sum(a * a, axis=-1, keepdims=True) + L2_EPS)
                if part == 0:
                    a = a * (HEAD_DIM ** -0.5)
                heads.append(a)
            out.append(heads)
        return out

    def chunk_systems(grp, qkv):
        heads = range(grp * GDN_HEAD_GROUP, (grp + 1) * GDN_HEAD_GROUP)
        qs, ks, vs = (dict(zip(heads, part)) for part in qkv)
        raw, v16 = {}, {}
        for hd in heads:
            k16 = ks[hd].astype(BF16)
            q16 = qs[hd].astype(BF16)
            v16[hd] = vs[hd].astype(BF16)
            for m in range(npair):
                pr = slice(m * PAIR, (m + 1) * PAIR)
                raw[hd, m] = _mm_nt(jnp.concatenate([k16[pr], q16[pr]], axis=0), k16[pr])
        insts = [(hd, dr, m) for hd in heads for dr in range(2) for m in range(npair)]
        zero = jnp.zeros((CHUNK, HEAD_DIM), BF16)
        lows, rhss, beta_rows = [], [], []
        for hd, dr, m in insts:
            pr = slice(m * PAIR, (m + 1) * PAIR)
            bcol = dr * N_HEADS + hd
            gcol = 2 * N_HEADS + dr * N_HEADS + hd
            bt = beta[pr, bcol:bcol + 1]
            beta_rows.append(beta_t[bcol:bcol + 1, pr])
            g_c = g_all[pr, gcol:gcol + 1]
            g_r = g_all_t[gcol:gcol + 1, pr]
            diff = g_c - g_r
            decay = jnp.exp(diff[:CHUNK] * mk_ref[MASK_INCL_L + dr] + diff[CHUNK:] * mk_ref[MASK_INCL_R + dr])
            decay = decay * mk_ref[MASK_INCL + dr]
            kk = raw[hd, m][:PAIR] * bt
            qk = raw[hd, m][PAIR:]
            lows.append((kk[:CHUNK] * mk_ref[MASK_STRICT_L + dr] + kk[CHUNK:] * mk_ref[MASK_STRICT_R + dr]) * decay)
            pq = ((qk[:CHUNK] * mk_ref[MASK_LEFT] + qk[CHUNK:] * mk_ref[MASK_RIGHT]) * decay).astype(BF16)
            eg = jnp.exp(g_c)
            keg = (ks[hd][pr] * eg).astype(BF16)
            qeg = (qs[hd][pr] * eg).astype(BF16)
            vp = v16[hd][pr]
            rhss.append(jnp.concatenate(
                [jnp.concatenate([vp[:CHUNK], keg[:CHUNK], zero, zero], axis=1),
                 jnp.concatenate([zero, zero, vp[CHUNK:], keg[CHUNK:]], axis=1)], axis=0))
            for j in range(2):
                n = 2 * m + j
                rs = slice(n * CHUNK, (n + 1) * CHUNK)
                js = slice(j * CHUNK, (j + 1) * CHUNK)
                last = n * CHUNK + (CHUNK - 1 if dr == 0 else 0)
                g_l = g_all[last:last + 1, gcol:gcol + 1]
                p_ref[dr, 0, hd, rs, :] = pq[:, js]
                kq_ref[dr, 0, hd, n, CHUNK:, :] = qeg[js]
                kd_ref[dr, 0, hd, rs, :] = (ks[hd][rs] * jnp.exp(g_l - g_all[rs, gcol:gcol + 1])).astype(BF16)
                egl_ref[dr, 0, hd, n] = jnp.broadcast_to(jnp.exp(g_l), (1, HEAD_DIM))
        invs = _tri_inverse_pairs(lows, mk_ref)
        sols = [_mm(inv * br, rhs) for inv, br, rhs in zip(invs, beta_rows, rhss)]
        for (hd, dr, m), sol in zip(insts, sols):
            for j in range(2):
                n = 2 * m + j
                rs = slice(n * CHUNK, (n + 1) * CHUNK)
                w_out_ref[dr, 0, hd, rs, :] = sol[:, 2 * j * HEAD_DIM:(2 * j + 1) * HEAD_DIM]
                kq_ref[dr, 0, hd, n, :CHUNK, :] = sol[:, (2 * j + 1) * HEAD_DIM:(2 * j + 2) * HEAD_DIM].astype(BF16)

    qkv = project(0)
    for grp in range(ngroup):
        if grp + 1 < ngroup:
            nxt = project(grp + 1)
        else:
            z_ref[0] = _mm(hm, w_ref[:, 3 * dk:3 * dk + d])
        chunk_systems(grp, qkv)
        qkv = nxt


def _gdn_front(x3, g, w_in, w_gate, conv_w, alog_row, dtb_row, tg):
    bsz, s, d = x3.shape
    nin = w_in.shape[1]
    kw, nq = conv_w.shape
    nchunk = tg // CHUNK
    hb = tg // SHORT_HALO
    nhb = s // SHORT_HALO
    gw = GDN_HEAD_GROUP * HEAD_DIM

    def rows(dtype, last=HEAD_DIM):
        return (jax.ShapeDtypeStruct((2, bsz, N_HEADS, s, last), dtype),
                pl.BlockSpec((2, 1, N_HEADS, tg, last), lambda b, i: (0, b, 0, i, 0)))

    def per_chunk(dtype, r):
        return (jax.ShapeDtypeStruct((2, bsz, N_HEADS, s // CHUNK, r, HEAD_DIM), dtype),
                pl.BlockSpec((2, 1, N_HEADS, nchunk, r, HEAD_DIM), lambda b, i: (0, b, 0, i, 0, 0)))

    outs = [(jax.ShapeDtypeStruct((bsz, s, d), F32), pl.BlockSpec((1, tg, d), lambda b, i: (b, i, 0))),
            rows(F32), per_chunk(BF16, 2 * CHUNK), rows(BF16), rows(BF16, CHUNK), per_chunk(F32, 1)]
    return pl.pallas_call(
        _gdn_front_body,
        grid=(bsz, s // tg),
        in_specs=[pl.BlockSpec((1, tg, d), lambda b, i: (b, i, 0)),
                  pl.BlockSpec((1, SHORT_HALO, d), lambda b, i: (b, jnp.maximum(i * hb - 1, 0), 0)),
                  pl.BlockSpec((1, SHORT_HALO, d), lambda b, i: (b, jnp.minimum((i + 1) * hb, nhb - 1), 0)),
                  _const_spec((1, d)), _const_spec((d, nin)), _const_spec((d, LANES)),
                  _const_spec((kw, nq)), _const_spec((1, LANES)), _const_spec((1, LANES)),
                  _const_spec((N_MASKS, CHUNK, PAIR))],
        out_specs=[o[1] for o in outs],
        out_shape=[o[0] for o in outs],
        scratch_shapes=[pltpu.VMEM((tg + 2 * SHORT_HALO, d), F32),
                        pltpu.VMEM((2, 3, tg + 2 * SHORT_HALO, gw), F32)],
        compiler_params=_params("parallel", "parallel"),
        name="gdn_front",
    )(x3, x3, x3, g, w_in, w_gate, conv_w, alog_row, dtb_row, _chunk_masks())


GDN_SCAN_GROUP = 4


def _gdn_scan_body(wf, kqf, kdf, pf, eglf, wb, kqb, kdb, pb, eglb, of_ref, ob_ref, state_ref):
    n = pl.program_id(0)

    @pl.when(n == 0)
    def _():
        state_ref[...] = jnp.zeros_like(state_ref)

    nbh = state_ref.shape[1]
    dirs = ((wf, kqf, kdf, pf, eglf, of_ref), (wb, kqb, kdb, pb, eglb, ob_ref))

    def body(i, carry):
        chains = [(dr, i * GDN_SCAN_GROUP + j) for j in range(GDN_SCAN_GROUP) for dr in range(2)]
        st = [state_ref[dr, b] for dr, b in chains]
        r = [_mm(dirs[dr][1][0, b, 0], s_) for (dr, b), s_ in zip(chains, st)]
        ub = [(dirs[dr][0][0, b] - r_[:CHUNK]).astype(BF16) for (dr, b), r_ in zip(chains, r)]
        pu = [_mm(dirs[dr][3][0, b], u_) for (dr, b), u_ in zip(chains, ub)]
        ku = [_mm_tn(dirs[dr][2][0, b], u_) for (dr, b), u_ in zip(chains, ub)]
        for (dr, b), r_, pu_, ku_, s_ in zip(chains, r, pu, ku, st):
            dirs[dr][5][b] = r_[CHUNK:] + pu_
            state_ref[dr, b] = s_ * dirs[dr][4][0, b, 0] + ku_
        return carry

    lax.fori_loop(0, nbh // GDN_SCAN_GROUP, body, 0)


def _gdn_scan(w, kq, kd, p, egl):
    _, nbh, s, dh = w.shape
    nchunk = s // CHUNK

    def specs(dr):
        def im(n):
            return (dr, 0, n if dr == 0 else nchunk - 1 - n, 0)

        def im5(n):
            return im(n) + (0,)
        return [pl.BlockSpec((1, nbh, CHUNK, dh), im), pl.BlockSpec((1, nbh, 1, 2 * CHUNK, dh), im5),
                pl.BlockSpec((1, nbh, CHUNK, dh), im), pl.BlockSpec((1, nbh, CHUNK, CHUNK), im),
                pl.BlockSpec((1, nbh, 1, 1, dh), im5)]

    o_shape = jax.ShapeDtypeStruct((nbh, s, dh), F32)
    return pl.pallas_call(
        _gdn_scan_body,
        grid=(nchunk,),
        in_specs=specs(0) + specs(1),
        out_specs=[pl.BlockSpec((nbh, CHUNK, dh), lambda n: (0, n, 0)),
                   pl.BlockSpec((nbh, CHUNK, dh), lambda n: (0, nchunk - 1 - n, 0))],
        out_shape=[o_shape, o_shape],
        scratch_shapes=[pltpu.VMEM((2, nbh, dh, dh), F32)],
        compiler_params=_params("arbitrary"),
        name="gdn_scan",
    )(w, kq, kd, p, egl, w, kq, kd, p, egl)


TOKEN_TILE_ROWS = 8
DMA_ISSUE_UNROLL = 8


def _to_token_tiles(dst_ref, val):
    n = val.shape[0]
    for j in range(TOKEN_TILE_ROWS):
        dst_ref[pl.ds(j, n, stride=TOKEN_TILE_ROWS), :] = val[:, j * LANES:(j + 1) * LANES]


def _from_token_tiles(src_ref, n):
    return jnp.concatenate([src_ref[pl.ds(j, n, stride=TOKEN_TILE_ROWS), :] for j in range(TOKEN_TILE_ROWS)],
                           axis=1)


def _gdn_out_body(of_ref, ob_ref, z_ref, x_ref, on_ref, wo_ref, fg_ref, rwt_ref,
                  xo_ref, pos_ref, wcol_ref, cnt_ref, xs_ref,
                  act_ref, tile_ref, zero_ref, carry_ref, pos_smem, cnt_smem, sem, psem, zsem,
                  *, cap, tm):
    g = pl.program_id(0)
    ng = pl.num_programs(0)
    ts = x_ref.shape[0]
    slot = g % 2
    rpt = TOKEN_TILE_ROWS

    def wait_rows(s):
        for _ in range(2):
            pltpu.make_async_copy(tile_ref.at[s], xs_ref.at[pl.ds(0, ts * rpt)], sem.at[s]).wait()

    @pl.when(g == 0)
    def _():
        carry_ref[...] = jnp.zeros_like(carry_ref)

    for h in range(N_HEADS):
        hs = slice(h * HEAD_DIM, (h + 1) * HEAD_DIM)
        o = of_ref[0, h] + ob_ref[0, h]
        o = o * lax.rsqrt(jnp.mean(o * o, axis=-1, keepdims=True) + RMS_EPS) * on_ref[...]
        act_ref[:, hs] = (o * _silu(z_ref[:, hs])).astype(BF16)
    x = x_ref[...] + _mm(act_ref[...], wo_ref[...])
    xo_ref[...] = x
    hn = _rms(x, fg_ref[...])

    logits = lax.dot_general(rwt_ref[...], hn, (((1,), (1,)), ((), ())),
                             preferred_element_type=F32, precision=lax.Precision.HIGHEST)
    eidx = lax.broadcasted_iota(jnp.int32, logits.shape, 0).astype(F32)
    neg = jnp.float32(-jnp.inf)
    m1 = jnp.max(logits, axis=0, keepdims=True)
    i1 = jnp.min(jnp.where(logits == m1, eidx, float(N_EXPERTS)), axis=0, keepdims=True)
    one1 = eidx == i1
    rest = jnp.where(one1, neg, logits)
    m2 = jnp.max(rest, axis=0, keepdims=True)
    i2 = jnp.min(jnp.where(rest == m2, eidx, float(N_EXPERTS)), axis=0, keepdims=True)
    one2 = eidx == i2
    e2 = jnp.exp(m2 - m1)
    w1 = 1.0 / (1.0 + e2)
    w2 = e2 * w1

    chosen = jnp.where(one1 | one2, 1.0, 0.0)
    ri = lax.broadcasted_iota(jnp.int32, (ts, ts), 0)
    ci = lax.broadcasted_iota(jnp.int32, (ts, ts), 1)
    before = jnp.where(ri < ci, 1.0, 0.0).astype(BF16)
    rank = jnp.dot(chosen.astype(BF16), before, preferred_element_type=F32)
    carry = carry_ref[...]
    slot_f = eidx * float(cap) + carry[:, 0:1] + rank
    p1 = jnp.sum(jnp.where(one1, slot_f, 0.0), axis=0, keepdims=True)
    p2 = jnp.sum(jnp.where(one2, slot_f, 0.0), axis=0, keepdims=True)
    carry = carry + jnp.sum(chosen, axis=1, keepdims=True)
    carry_ref[...] = carry
    cnt_ref[...] = carry.astype(jnp.int32)
    row8 = lax.broadcasted_iota(jnp.int32, (N_EXPERTS, ts), 0)
    pos = jnp.where(row8 == 0, p1, jnp.where(row8 == 1, p2, 0.0)).astype(jnp.int32)
    pos_ref[...] = pos
    row128 = lax.broadcasted_iota(jnp.int32, (LANES, ts), 0)
    wcol_ref[...] = jnp.where(row128 == 0, w1, jnp.where(row128 == 1, w2, 0.0)).T

    cp = pltpu.make_async_copy(pos_ref, pos_smem, psem)
    cp.start()
    cp.wait()
    for sl in range(2):
        @pl.when(slot == sl)
        def _(sl=sl):
            @pl.when(g >= 2)
            def _():
                wait_rows(sl)

            _to_token_tiles(tile_ref.at[sl], hn)

            def issue(t, c):
                src = tile_ref.at[sl, pl.ds(t * rpt, rpt)]
                for k in range(2):
                    dst = xs_ref.at[pl.ds(pos_smem[k, t] * rpt, rpt)]
                    pltpu.make_async_copy(src, dst, sem.at[sl]).start()
                return c

            lax.fori_loop(0, ts, issue, 0, unroll=DMA_ISSUE_UNROLL)

    @pl.when(g == ng - 1)
    def _():
        wait_rows(slot)

        @pl.when(g >= 1)
        def _():
            wait_rows(1 - slot)

        zero_ref[...] = jnp.zeros_like(zero_ref)
        cc = pltpu.make_async_copy(cnt_ref, cnt_smem, psem)
        cc.start()
        cc.wait()
        tails = [pltpu.make_async_copy(
            zero_ref, xs_ref.at[pl.ds((e * cap + cnt_smem[e, 0]) * rpt, tm * rpt)], zsem)
            for e in range(N_EXPERTS)]
        for c in tails:
            c.start()
        for c in tails:
            c.wait()


def _gdn_out(o_f, o_b, z2, x2, o_norm, w_out, ffn_g, router_t, ts, tm, cap):
    t, d = x2.shape
    s = o_f.shape[2]
    spb = s // ts
    rpt = TOKEN_TILE_ROWS
    o_spec = pl.BlockSpec((1, N_HEADS, ts, HEAD_DIM), lambda g: (g // spb, 0, g % spb, 0))
    return pl.pallas_call(
        functools.partial(_gdn_out_body, cap=cap, tm=tm),
        grid=(t // ts,),
        in_specs=[o_spec, o_spec,
                  pl.BlockSpec((ts, d), lambda g: (g, 0)), pl.BlockSpec((ts, d), lambda g: (g, 0)),
                  _const_spec((1, HEAD_DIM)), _const_spec((d, d)), _const_spec((1, d)),
                  _const_spec((N_EXPERTS, d))],
        out_specs=[pl.BlockSpec((ts, d), lambda g: (g, 0)),
                   pl.BlockSpec((N_EXPERTS, ts), lambda g: (0, g)),
                   pl.BlockSpec((ts, LANES), lambda g: (g, 0)),
                   pl.BlockSpec((N_EXPERTS, LANES), lambda g: (0, 0)),
                   pl.BlockSpec(memory_space=pl.ANY)],
        out_shape=[jax.ShapeDtypeStruct((t, d), F32),
                   jax.ShapeDtypeStruct((N_EXPERTS, t), jnp.int32),
                   jax.ShapeDtypeStruct((t, LANES), F32),
                   jax.ShapeDtypeStruct((N_EXPERTS, LANES), jnp.int32),
                   jax.ShapeDtypeStruct((N_EXPERTS * cap * rpt, LANES), F32)],
        scratch_shapes=[pltpu.VMEM((ts, d), BF16),
                        pltpu.VMEM((2, ts * rpt, LANES), F32),
                        pltpu.VMEM((tm * rpt, LANES), F32),
                        pltpu.VMEM((N_EXPERTS, LANES), F32),
                        pltpu.SMEM((N_EXPERTS, ts), jnp.int32),
                        pltpu.SMEM((N_EXPERTS, LANES), jnp.int32),
                        pltpu.SemaphoreType.DMA((2,)), pltpu.SemaphoreType.DMA, pltpu.SemaphoreType.DMA],
        compiler_params=_params("arbitrary"),
        name="gdn_out_router",
    )(o_f, o_b, z2, x2, o_norm, w_out, ffn_g, router_t)


def _moe_body(te_ref, tb_ref, nu_ref, xs_ref, wg_ref, wu_ref, wd_ref, ys_ref, xb_ref, acc_ref, *, fc):
    i = pl.program_id(0)
    f = pl.program_id(1)
    tm = xb_ref.shape[0]
    tf = wg_ref.shape[2]

    @pl.when(i < nu_ref[0])
    def _():
        @pl.when(f == 0)
        def _():
            xb_ref[...] = _from_token_tiles(xs_ref, tm).astype(BF16)
            acc_ref[...] = jnp.zeros_like(acc_ref)

        x = xb_ref[...]
        for c in range(tf // fc):
            cs = slice(c * fc, (c + 1) * fc)
            a = _silu(_mm(x, wg_ref[0, :, cs])) * _mm(x, wu_ref[0, :, cs])
            acc_ref[...] += _mm(a, wd_ref[0, cs, :])

        @pl.when(f == pl.num_programs(1) - 1)
        def _():
            _to_token_tiles(ys_ref, acc_ref[...])


def _moe_grouped(xs, tile_expert, tile_block, n_used, wg, wu, wd, tm, tf, fc):
    ne, d, dff = wg.shape
    nt = tile_expert.shape[0]
    nf = dff // tf
    rpt = TOKEN_TILE_ROWS

    def fsel(i, f, nu):
        return jnp.where(i < nu[0], f, nf - 1)

    grid_spec = pltpu.PrefetchScalarGridSpec(
        num_scalar_prefetch=3,
        grid=(nt, nf),
        in_specs=[pl.BlockSpec((tm * rpt, LANES), lambda i, f, te, tb, nu: (tb[i], 0)),
                  pl.BlockSpec((1, d, tf), lambda i, f, te, tb, nu: (te[i], 0, fsel(i, f, nu))),
                  pl.BlockSpec((1, d, tf), lambda i, f, te, tb, nu: (te[i], 0, fsel(i, f, nu))),
                  pl.BlockSpec((1, tf, d), lambda i, f, te, tb, nu: (te[i], fsel(i, f, nu), 0))],
        out_specs=pl.BlockSpec((tm * rpt, LANES), lambda i, f, te, tb, nu: (tb[i], 0)),
        scratch_shapes=[pltpu.VMEM((tm, d), BF16), pltpu.VMEM((tm, d), F32)],
    )
    return pl.pallas_call(
        functools.partial(_moe_body, fc=fc),
        grid_spec=grid_spec,
        out_shape=jax.ShapeDtypeStruct(xs.shape, F32),
        compiler_params=_params("arbitrary", "arbitrary"),
        name="moe_grouped",
    )(tile_expert, tile_block, n_used, xs, wg, wu, wd)


def _combine_body(pos_ref, posn_ref, x_ref, wcol_ref, fn_ref, ys_ref, o_ref, gbuf, pos_smem, sem, psem):
    i = pl.program_id(0)
    n = pl.num_programs(0)
    tc = x_ref.shape[0]
    rpt = TOKEN_TILE_ROWS

    def fetch(p_ref, sl):
        cp = pltpu.make_async_copy(p_ref, pos_smem, psem)
        cp.start()
        cp.wait()

        def issue(t, c):
            for k in range(2):
                src = ys_ref.at[pl.ds(pos_smem[k, t] * rpt, rpt)]
                pltpu.make_async_copy(src, gbuf.at[sl, k, pl.ds(t * rpt, rpt)], sem.at[sl]).start()
            return c

        lax.fori_loop(0, tc, issue, 0, unroll=DMA_ISSUE_UNROLL)

    @pl.when(i == 0)
    def _():
        fetch(pos_ref, 0)

    for sl in range(2):
        @pl.when((i + 1 < n) & ((i + 1) % 2 == sl))
        def _(sl=sl):
            fetch(posn_ref, sl)

    for sl in range(2):
        @pl.when(i % 2 == sl)
        def _(sl=sl):
            for k in range(2):
                pltpu.make_async_copy(ys_ref.at[pl.ds(0, tc * rpt)], gbuf.at[sl, k], sem.at[sl]).wait()
            wcol = wcol_ref[...]
            y = (x_ref[...] + wcol[:, 0:1] * _from_token_tiles(gbuf.at[sl, 0], tc)
                 + wcol[:, 1:2] * _from_token_tiles(gbuf.at[sl, 1], tc))
            o_ref[...] = _rms(y, fn_ref[...])


def _moe_combine(pos, x2, wcol, final_g, ys, tc):
    t, d = x2.shape
    n = t // tc
    rpt = TOKEN_TILE_ROWS
    return pl.pallas_call(
        _combine_body,
        grid=(n,),
        in_specs=[pl.BlockSpec((N_EXPERTS, tc), lambda i: (0, i)),
                  pl.BlockSpec((N_EXPERTS, tc), lambda i: (0, jnp.minimum(i + 1, n - 1))),
                  pl.BlockSpec((tc, d), lambda i: (i, 0)),
                  pl.BlockSpec((tc, LANES), lambda i: (i, 0)),
                  _const_spec((1, d)),
                  pl.BlockSpec(memory_space=pl.ANY)],
        out_specs=pl.BlockSpec((tc, d), lambda i: (i, 0)),
        out_shape=jax.ShapeDtypeStruct((t, d), F32),
        scratch_shapes=[pltpu.VMEM((2, 2, tc * rpt, LANES), F32),
                        pltpu.SMEM((N_EXPERTS, tc), jnp.int32),
                        pltpu.SemaphoreType.DMA((2,)), pltpu.SemaphoreType.DMA],
        compiler_params=_params("arbitrary"),
        name="moe_combine_norm",
    )(pos, pos, x2, wcol, final_g, ys)


def _tile_schedule(counts, tm, cap, nt):
    ntile = (counts + tm - 1) // tm
    ends = jnp.cumsum(ntile)
    n_used = ends[-1]
    i = jnp.minimum(jnp.arange(nt, dtype=jnp.int32), jnp.maximum(n_used - 1, 0))
    te = jnp.sum((i[:, None] >= ends[None, :]).astype(jnp.int32), axis=1)
    tb = te * (cap // tm) + i - (ends - ntile)[te]
    return te.astype(jnp.int32), tb.astype(jnp.int32), n_used.reshape(1).astype(jnp.int32)


def _row(v):
    return v.reshape(1, -1).astype(F32)


def _pad_cols(w, n):
    return jnp.pad(w, ((0, 0), (0, n - w.shape[1])))


def _conformer_layer(x3, mix_g, ffn_g, pw1_w, pw1_b, dw_w, dw_b, ln_g, ln_b, pw2_w, pw2_b,
                     w_gate, w_up, w_down):
    bsz, s, d = x3.shape
    t = bsz * s
    tm = min(512, t)
    ts = min(512, s)
    x3 = _conf_mixer(x3, _row(mix_g), pw1_w.astype(BF16), _row(pw1_b), dw_w, _row(dw_b), _row(ln_g),
                     _row(ln_b), pw2_w.astype(BF16), _row(pw2_b), ts)
    x2 = _ffn(x3.reshape(t, d), _row(ffn_g), w_gate.astype(BF16), w_up.astype(BF16),
              w_down.astype(BF16), tm, 256)
    return x2.reshape(bsz, s, d)


def _deltanet_moe_layer(x3, mix_g, ffn_g, w_in, conv_w, a_log, dt_bias, o_norm, w_out,
                        router, e_gate, e_up, e_down, final_g):
    bsz, s, d = x3.shape
    t = bsz * s
    tm = min(512, t)
    ts = min(512, s)
    nmain = 4 * d
    zero16 = jnp.zeros((2 * N_HEADS,), F32)
    alog_row = _row(_pad_cols(jnp.concatenate([zero16, a_log.reshape(-1)])[None], LANES))
    dtb_row = _row(_pad_cols(jnp.concatenate([zero16, dt_bias.reshape(-1)])[None], LANES))
    z, w, kq, kd, p, egl = _gdn_front(x3, _row(mix_g), w_in.astype(BF16),
                                      _pad_cols(w_in[:, nmain:], LANES).astype(BF16), conv_w,
                                      alog_row, dtb_row, min(256, s))
    z = z.reshape(t, d)
    nbh = bsz * N_HEADS
    nck = s // CHUNK
    o_f, o_b = _gdn_scan(w.reshape(2, nbh, s, HEAD_DIM), kq.reshape(2, nbh, nck, 2 * CHUNK, HEAD_DIM),
                         kd.reshape(2, nbh, s, HEAD_DIM), p.reshape(2, nbh, s, CHUNK),
                         egl.reshape(2, nbh, nck, 1, HEAD_DIM))
    cap = t + tm
    x2, pos, wcol, cnt, xs = _gdn_out(o_f.reshape(bsz, N_HEADS, s, HEAD_DIM), o_b.reshape(bsz, N_HEADS, s, HEAD_DIM),
                                      z, x3.reshape(t, d), _row(o_norm), w_out.astype(BF16), _row(ffn_g),
                                      router.T, ts, tm, cap)
    te, tb, n_used = _tile_schedule(cnt[:, 0], tm, cap, 2 * t // tm + N_EXPERTS)
    dffe = e_gate.shape[2]
    tf = dffe // 2 if (dffe // 2) % 256 == 0 else dffe
    ys = _moe_grouped(xs, te, tb, n_used, e_gate.astype(BF16), e_up.astype(BF16), e_down.astype(BF16),
                      tm, tf, 256)
    out = _moe_combine(pos, x2, wcol, _row(final_g), ys, ts)
    return out.reshape(bsz, s, d)


def kernel(x, mix_norm, ffn_norm, cf_pw1_w, cf_pw1_b, cf_dw_w, cf_dw_b, cf_ln_g, cf_ln_b, cf_pw2_w, cf_pw2_b, ffn_w_gate, ffn_w_up, ffn_w_down, gdn_w_in, gdn_conv_w, gdn_a_log, gdn_dt_bias, gdn_o_norm, gdn_w_out, moe_router, moe_w_gate, moe_w_up, moe_w_down, final_norm):
    x = _conformer_layer(x, mix_norm[0], ffn_norm[0], cf_pw1_w[0], cf_pw1_b[0], cf_dw_w[0], cf_dw_b[0],
                         cf_ln_g[0], cf_ln_b[0], cf_pw2_w[0], cf_pw2_b[0],
                         ffn_w_gate[0], ffn_w_up[0], ffn_w_down[0])
    return _deltanet_moe_layer(x, mix_norm[1], ffn_norm[1], gdn_w_in[0], gdn_conv_w[0], gdn_a_log[0],
                               gdn_dt_bias[0], gdn_o_norm[0], gdn_w_out[0], moe_router[0],
                               moe_w_gate[0], moe_w_up[0], moe_w_down[0], final_norm)
```

```python
import functools

import jax
import jax.numpy as jnp
from jax import lax
from jax.experimental import pallas as pl
from jax.experimental.pallas import tpu as pltpu

F32 = jnp.float32
BF16 = jnp.bfloat16

RMS_EPS = 1e-6
LN_EPS = 1e-5
L2_EPS = 1e-6
N_HEADS = 8
HEAD_DIM = 128
CHUNK = 64
N_EXPERTS = 8
LANES = 128
CONV_HALO = 16
SHORT_HALO = 8
VMEM_LIMIT_BYTES = 56 * 1024 * 1024


def _params(*sem):
    return pltpu.CompilerParams(dimension_semantics=sem, vmem_limit_bytes=VMEM_LIMIT_BYTES)


def _const_spec(shape):
    nd = len(shape)
    return pl.BlockSpec(shape, lambda *_: (0,) * nd, pipeline_mode=pl.Buffered(1))


def _rms(x, g):
    return x * lax.rsqrt(jnp.mean(x * x, axis=-1, keepdims=True) + RMS_EPS) * g


def _silu(x):
    return x * jax.nn.sigmoid(x)


def _mm(a, b):
    return jnp.dot(a.astype(BF16), b.astype(BF16), preferred_element_type=F32)


def _mm_nt(a, b):
    return lax.dot_general(a.astype(BF16), b.astype(BF16), (((1,), (1,)), ((), ())),
                           preferred_element_type=F32)


def _mm_tn(a, b):
    return lax.dot_general(a.astype(BF16), b.astype(BF16), (((0,), (0,)), ((), ())),
                           preferred_element_type=F32)


def _mm_exact(a, b):
    return jnp.dot(a, b, preferred_element_type=F32, precision=lax.Precision.HIGHEST)


def _conf_mixer_body(x_ref, xp_ref, xn_ref, g_ref, w1_ref, b1_ref, dw_ref, dwb_ref, lng_ref, lnb_ref,
                     w2_ref, b2_ref, o_ref, xe_ref, ext_ref, cv_ref, *, nc):
    s = pl.program_id(1)
    ns = pl.num_programs(1)
    ts, d = cv_ref.shape
    width = dw_ref.shape[0]
    base = CONV_HALO - width // 2
    span = ts + 8 * ((base + width - 1) // 8)
    xe_ref[0:CONV_HALO, :] = xp_ref[0]
    xe_ref[CONV_HALO:CONV_HALO + ts, :] = x_ref[0]
    xe_ref[CONV_HALO + ts:, :] = xn_ref[0]
    h = _rms(xe_ref[...], g_ref[...]).astype(BF16)
    for c in range(d // nc):
        a = _mm(h, w1_ref[:, c * nc:(c + 1) * nc]) + b1_ref[:, c * nc:(c + 1) * nc]
        b = _mm(h, w1_ref[:, d + c * nc:d + (c + 1) * nc]) + b1_ref[:, d + c * nc:d + (c + 1) * nc]
        ext_ref[0, :, c * nc:(c + 1) * nc] = a * jax.nn.sigmoid(b)

    @pl.when(s == 0)
    def _():
        ext_ref[0, 0:CONV_HALO, :] = jnp.zeros((CONV_HALO, d), F32)

    @pl.when(s == ns - 1)
    def _():
        ext_ref[0, CONV_HALO + ts:, :] = jnp.zeros((CONV_HALO, d), F32)

    for p in range(1, 8):
        ext_ref[p, 0:span, :] = ext_ref[0, pl.ds(p, span), :]
    rb = 128
    for c in range(d // LANES):
        cs = slice(c * LANES, (c + 1) * LANES)
        for r in range(ts // rb):
            acc = jnp.zeros((rb, LANES), F32)
            for k in range(width):
                off = base + k
                acc = acc + dw_ref[k:k + 1, cs] * ext_ref[off % 8, pl.ds(r * rb + 8 * (off // 8), rb), cs]
            cv_ref[r * rb:(r + 1) * rb, cs] = acc + dwb_ref[:, cs]
    y = cv_ref[...]
    mu = jnp.mean(y, axis=-1, keepdims=True)
    yc = y - mu
    yn = yc * lax.rsqrt(jnp.mean(yc * yc, axis=-1, keepdims=True) + LN_EPS) * lng_ref[...] + lnb_ref[...]
    o_ref[0] = x_ref[0] + _mm(_silu(yn), w2_ref[...]) + b2_ref[...]


def _conf_mixer(x3, g, w1, b1, dw_w, dw_b, ln_g, ln_b, w2, b2, ts):
    bsz, s, d = x3.shape
    width = dw_w.shape[0]
    hb = ts // CONV_HALO
    nhb = s // CONV_HALO
    return pl.pallas_call(
        functools.partial(_conf_mixer_body, nc=256),
        grid=(bsz, s // ts),
        in_specs=[pl.BlockSpec((1, ts, d), lambda b, i: (b, i, 0)),
                  pl.BlockSpec((1, CONV_HALO, d), lambda b, i: (b, jnp.maximum(i * hb - 1, 0), 0)),
                  pl.BlockSpec((1, CONV_HALO, d), lambda b, i: (b, jnp.minimum((i + 1) * hb, nhb - 1), 0)),
                  _const_spec((1, d)), _const_spec((d, 2 * d)), _const_spec((1, 2 * d)),
                  _const_spec((width, d)), _const_spec((1, d)), _const_spec((1, d)), _const_spec((1, d)),
                  _const_spec((d, d)), _const_spec((1, d))],
        out_specs=pl.BlockSpec((1, ts, d), lambda b, i: (b, i, 0)),
        out_shape=jax.ShapeDtypeStruct((bsz, s, d), F32),
        scratch_shapes=[pltpu.VMEM((ts + 2 * CONV_HALO, d), F32),
                        pltpu.VMEM((8, ts + 2 * CONV_HALO, d), F32), pltpu.VMEM((ts, d), F32)],
        compiler_params=_params("parallel", "parallel"),
        name="conf_mixer",
    )(x3, x3, x3, g, w1, b1, dw_w, dw_b, ln_g, ln_b, w2, b2)


def _ffn_body(x_ref, g_ref, wg_ref, wu_ref, wd_ref, o_ref, a_ref, *, fc):
    x = x_ref[...]
    h = _rms(x, g_ref[...]).astype(BF16)
    dff = wg_ref.shape[1]
    for c in range(dff // fc):
        cs = slice(c * fc, (c + 1) * fc)
        a_ref[:, cs] = (_silu(_mm(h, wg_ref[:, cs])) * _mm(h, wu_ref[:, cs])).astype(BF16)
    o_ref[...] = x + _mm(a_ref[...], wd_ref[...])


def _ffn(x2d, g, wg, wu, wd, tm, fc):
    t, d = x2d.shape
    dff = wg.shape[1]
    return pl.pallas_call(
        functools.partial(_ffn_body, fc=fc),
        grid=(t // tm,),
        in_specs=[pl.BlockSpec((tm, d), lambda i: (i, 0)), _const_spec((1, d)),
                  _const_spec((d, dff)), _const_spec((d, dff)), _const_spec((dff, d))],
        out_specs=pl.BlockSpec((tm, d), lambda i: (i, 0)),
        out_shape=jax.ShapeDtypeStruct((t, d), F32),
        scratch_shapes=[pltpu.VMEM((tm, dff), BF16)],
        compiler_params=_params("parallel"),
        name="dense_swiglu",
    )(x2d, g, wg, wu, wd)


PAIR = 2 * CHUNK


def _block_diag(yb, mk_ref):
    return jnp.concatenate([yb * mk_ref[MASK_LEFT].astype(BF16), yb * mk_ref[MASK_RIGHT].astype(BF16)], axis=0)


def _tri_inverse_pairs(mats, mk_ref):
    def bd(vals):
        return [_block_diag(v.astype(BF16), mk_ref) for v in vals]

    l0 = [a * mk_ref[MASK_BLOCK8] for a in mats]
    l0b = [a.astype(BF16) for a in l0]
    l2 = [_mm(a, d) for a, d in zip(l0b, bd(l0b))]
    l2b = [a.astype(BF16) for a in l2]
    l2d = bd(l2b)
    l4 = [_mm(a, d) for a, d in zip(l2b, l2d)]
    l3 = [_mm(a, d) for a, d in zip(l0b, l2d)]
    xs = [mk_ref[MASK_EYE] - a + b - t for a, b, t in zip(l0, l2, l3)]
    x4 = [_mm(x, d) for x, d in zip(xs, bd(l4))]
    xs = [x + t for x, t in zip(xs, x4)]
    for level in range(MASK_MERGE0, MASK_MERGE0 + 3):
        xb = [x.astype(BF16) for x in xs]
        t1 = [_mm(a * mk_ref[level], d) for a, d in zip(mats, bd(xb))]
        t2 = [_mm(x, d) for x, d in zip(xb, bd(t1))]
        xs = [x - t for x, t in zip(xs, t2)]
    return xs


(MASK_INCL, MASK_INCL_L, MASK_INCL_R, MASK_STRICT_L, MASK_STRICT_R) = (0, 2, 4, 6, 8)
MASK_EYE, MASK_BLOCK8, MASK_MERGE0, MASK_LEFT, MASK_RIGHT = 10, 11, 12, 15, 16
N_MASKS = 17


def _chunk_masks():
    ri = lax.broadcasted_iota(jnp.int32, (CHUNK, PAIR), 0)
    li = lax.broadcasted_iota(jnp.int32, (CHUNK, PAIR), 1)
    ci = li % CHUNK
    left, right = li < CHUNK, li >= CHUNK
    incl = [ci <= ri, ci >= ri]
    strict = [ci < ri, ci > ri]
    masks = (incl + [m & left for m in incl] + [m & right for m in incl]
             + [m & left for m in strict] + [m & right for m in strict]
             + [ci == ri, (ri // 8) == (ci // 8)])
    size = 8
    while size < CHUNK:
        masks.append(((ri // (2 * size)) == (ci // (2 * size))) & ((ri // size) != (ci // size)))
        size *= 2
    masks += [left, right]
    assert len(masks) == N_MASKS
    return jnp.stack(masks).astype(F32)


GDN_HEAD_GROUP = 4


def _gdn_front_body(x_ref, xp_ref, xn_ref, g_ref, w_ref, wgate_ref, cw_ref, alog_ref, dtb_ref, mk_ref,
                    z_ref, w_out_ref, kq_ref, kd_ref, p_ref, egl_ref, xe_ref, pe_ref):
    s = pl.program_id(1)
    ns = pl.num_programs(1)
    tg, d = z_ref.shape[1:]
    kw = cw_ref.shape[0]
    dk = N_HEADS * HEAD_DIM
    npair = tg // PAIR
    gw = GDN_HEAD_GROUP * HEAD_DIM
    ngroup = N_HEADS // GDN_HEAD_GROUP
    base = SHORT_HALO - kw // 2

    xe_ref[0:SHORT_HALO, :] = jnp.where(s > 0, xp_ref[0], 0.0)
    xe_ref[SHORT_HALO:SHORT_HALO + tg, :] = x_ref[0]
    xe_ref[SHORT_HALO + tg:, :] = jnp.where(s < ns - 1, xn_ref[0], 0.0)
    h = _rms(xe_ref[...], g_ref[...]).astype(BF16)
    hm = h[SHORT_HALO:SHORT_HALO + tg]

    gates = _mm(hm, wgate_ref[...])
    beta = jax.nn.sigmoid(gates)
    xa = gates + dtb_ref[...]
    softplus = jnp.maximum(xa, 0.0) + jnp.log1p(jnp.exp(-jnp.abs(xa)))
    log_a = -jnp.exp(alog_ref[...]) * softplus
    ri = lax.broadcasted_iota(jnp.int32, (tg, tg), 0)
    ci = lax.broadcasted_iota(jnp.int32, (tg, tg), 1)
    same = (ri // CHUNK) == (ci // CHUNK)
    cum_f = _mm_exact((same & (ci <= ri)).astype(F32), log_a)
    cum_b = _mm_exact((same & (ci >= ri)).astype(F32), log_a)
    lane = lax.broadcasted_iota(jnp.int32, (tg, LANES), 1)
    g_all = jnp.where(lane < 3 * N_HEADS, cum_f, cum_b)
    g_all_t = g_all.T
    beta_t = beta.T

    def project(grp):
        out = []
        for part in range(3):
            col0 = part * dk + grp * gw
            pe_ref[grp % 2, part] = _mm(h, w_ref[:, col0:col0 + gw])
            heads = []
            for c in range(GDN_HEAD_GROUP):
                col = col0 + c * LANES
                acc = jnp.zeros((tg, LANES), F32)
                for k in range(kw):
                    acc = acc + (cw_ref[k:k + 1, col:col + LANES]
                                 * pe_ref[grp % 2, part, pl.ds(base + k, tg), c * LANES:(c + 1) * LANES])
                a = _silu(acc)
                if part < 2:
                    a = a * lax.rsqrt(jnp.sum(a * a, axis=-1, keepdims=True) + L2_EPS)
                if part == 0:
                    a = a * (HEAD_DIM ** -0.5)
                heads.append(a)
            out.append(heads)
        return out

    def chunk_systems(grp, qkv):
        heads = range(grp * GDN_HEAD_GROUP, (grp + 1) * GDN_HEAD_GROUP)
        qs, ks, vs = (dict(zip(heads, part)) for part in qkv)
        raw, v16 = {}, {}
        for hd in heads:
            k16 = ks[hd].astype(BF16)
            q16 = qs[hd].astype(BF16)
            v16[hd] = vs[hd].astype(BF16)
            for m in range(npair):
                pr = slice(m * PAIR, (m + 1) * PAIR)
                raw[hd, m] = _mm_nt(jnp.concatenate([k16[pr], q16[pr]], axis=0), k16[pr])
        insts = [(hd, dr, m) for hd in heads for dr in range(2) for m in range(npair)]
        zero = jnp.zeros((CHUNK, HEAD_DIM), BF16)
        lows, rhss, beta_rows = [], [], []
        for hd, dr, m in insts:
            pr = slice(m * PAIR, (m + 1) * PAIR)
            bcol = dr * N_HEADS + hd
            gcol = 2 * N_HEADS + dr * N_HEADS + hd
            bt = beta[pr, bcol:bcol + 1]
            beta_rows.append(beta_t[bcol:bcol + 1, pr])
            g_c = g_all[pr, gcol:gcol + 1]
            g_r = g_all_t[gcol:gcol + 1, pr]
            diff = g_c - g_r
            decay = jnp.exp(diff[:CHUNK] * mk_ref[MASK_INCL_L + dr] + diff[CHUNK:] * mk_ref[MASK_INCL_R + dr])
            decay = decay * mk_ref[MASK_INCL + dr]
            kk = raw[hd, m][:PAIR] * bt
            qk = raw[hd, m][PAIR:]
            lows.append((kk[:CHUNK] * mk_ref[MASK_STRICT_L + dr] + kk[CHUNK:] * mk_ref[MASK_STRICT_R + dr]) * decay)
            pq = (qk[:CHUNK] * mk_ref[MASK_LEFT] + qk[CHUNK:] * mk_ref[MASK_RIGHT]) * decay
            half = mk_ref[MASK_LEFT + dr]
            p_tiles = [(t * half).astype(BF16) for t in (pq, pltpu.roll(pq, CHUNK, axis=1))]
            eg = jnp.exp(g_c)
            keg = (ks[hd][pr] * eg).astype(BF16)
            qeg = (qs[hd][pr] * eg).astype(BF16)
            vp = v16[hd][pr]
            rhss.append(jnp.concatenate(
                [jnp.concatenate([vp[:CHUNK], keg[:CHUNK], zero, zero], axis=1),
                 jnp.concatenate([zero, zero, vp[CHUNK:], keg[CHUNK:]], axis=1)], axis=0))
            for j in range(2):
                n = 2 * m + j
                rs = slice(n * CHUNK, (n + 1) * CHUNK)
                js = slice(j * CHUNK, (j + 1) * CHUNK)
                last = n * CHUNK + (CHUNK - 1 if dr == 0 else 0)
                g_l = g_all[last:last + 1, gcol:gcol + 1]
                p_ref[dr, 0, hd, rs, :] = p_tiles[(j + dr) % 2]
                kq_ref[dr, 0, hd, n, CHUNK:, :] = qeg[js]
                kd_ref[dr, 0, hd, rs, :] = (ks[hd][rs] * jnp.exp(g_l - g_all[rs, gcol:gcol + 1])).astype(BF16)
                egl_ref[dr, 0, hd, n] = jnp.broadcast_to(jnp.exp(g_l), (1, HEAD_DIM))
        invs = _tri_inverse_pairs(lows, mk_ref)
        sols = [_mm(inv * br, rhs) for inv, br, rhs in zip(invs, beta_rows, rhss)]
        for (hd, dr, m), sol in zip(insts, sols):
            for j in range(2):
                n = 2 * m + j
                rs = slice(n * CHUNK, (n + 1) * CHUNK)
                w_out_ref[dr, 0, hd, rs, :] = sol[:, 2 * j * HEAD_DIM:(2 * j + 1) * HEAD_DIM]
                kq_ref[dr, 0, hd, n, :CHUNK, :] = sol[:, (2 * j + 1) * HEAD_DIM:(2 * j + 2) * HEAD_DIM].astype(BF16)

    qkv = project(0)
    for grp in range(ngroup):
        if grp + 1 < ngroup:
            nxt = project(grp + 1)
        else:
            z_ref[0] = _mm(hm, w_ref[:, 3 * dk:3 * dk + d])
        chunk_systems(grp, qkv)
        qkv = nxt


def _gdn_front(x3, g, w_in, w_gate, conv_w, alog_row, dtb_row, tg):
    bsz, s, d = x3.shape
    nin = w_in.shape[1]
    kw, nq = conv_w.shape
    nchunk = tg // CHUNK
    hb = tg // SHORT_HALO
    nhb = s // SHORT_HALO
    gw = GDN_HEAD_GROUP * HEAD_DIM

    def rows(dtype, last=HEAD_DIM):
        return (jax.ShapeDtypeStruct((2, bsz, N_HEADS, s, last), dtype),
                pl.BlockSpec((2, 1, N_HEADS, tg, last), lambda b, i: (0, b, 0, i, 0)))

    def per_chunk(dtype, r):
        return (jax.ShapeDtypeStruct((2, bsz, N_HEADS, s // CHUNK, r, HEAD_DIM), dtype),
                pl.BlockSpec((2, 1, N_HEADS, nchunk, r, HEAD_DIM), lambda b, i: (0, b, 0, i, 0, 0)))

    outs = [(jax.ShapeDtypeStruct((bsz, s, d), F32), pl.BlockSpec((1, tg, d), lambda b, i: (b, i, 0))),
            rows(F32), per_chunk(BF16, 2 * CHUNK), rows(BF16), rows(BF16), per_chunk(F32, 1)]
    return pl.pallas_call(
        _gdn_front_body,
        grid=(bsz, s // tg),
        in_specs=[pl.BlockSpec((1, tg, d), lambda b, i: (b, i, 0)),
                  pl.BlockSpec((1, SHORT_HALO, d), lambda b, i: (b, jnp.maximum(i * hb - 1, 0), 0)),
                  pl.BlockSpec((1, SHORT_HALO, d), lambda b, i: (b, jnp.minimum((i + 1) * hb, nhb - 1), 0)),
                  _const_spec((1, d)), _const_spec((d, nin)), _const_spec((d, LANES)),
                  _const_spec((kw, nq)), _const_spec((1, LANES)), _const_spec((1, LANES)),
                  _const_spec((N_MASKS, CHUNK, PAIR))],
        out_specs=[o[1] for o in outs],
        out_shape=[o[0] for o in outs],
        scratch_shapes=[pltpu.VMEM((tg + 2 * SHORT_HALO, d), F32),
                        pltpu.VMEM((2, 3, tg + 2 * SHORT_HALO, gw), F32)],
        compiler_params=_params("parallel", "parallel"),
        name="gdn_front",
    )(x3, x3, x3, g, w_in, w_gate, conv_w, alog_row, dtb_row, _chunk_masks())


GDN_SCAN_GROUP = 8


def _gdn_scan_body(wf, kqf, kdf, pf, eglf, wb, kqb, kdb, pb, eglb, of_ref, ob_ref, state_ref):
    n = pl.program_id(0)

    @pl.when(n == 0)
    def _():
        state_ref[...] = jnp.zeros_like(state_ref)

    nbh = state_ref.shape[1]
    dirs = ((wf, kqf, kdf, pf, eglf, of_ref), (wb, kqb, kdb, pb, eglb, ob_ref))

    def body(i, carry):
        bhs = [i * GDN_SCAN_GROUP + j for j in range(GDN_SCAN_GROUP)]
        zero = jnp.zeros((CHUNK, HEAD_DIM), BF16)
        st = [[state_ref[dr, b] for dr in range(2)] for b in bhs]
        r = [[_mm(dirs[dr][1][0, b, 0], s_[dr]) for dr in range(2)] for b, s_ in zip(bhs, st)]
        ub = [[(dirs[dr][0][0, b] - r_[dr][:CHUNK]).astype(BF16) for dr in range(2)] for b, r_ in zip(bhs, r)]
        ud = [jnp.concatenate([jnp.concatenate([u_[0], zero], axis=1), jnp.concatenate([zero, u_[1]], axis=1)],
                              axis=0) for u_ in ub]
        pu = [_mm(dirs[0][3][0, b] + dirs[1][3][0, b], d_) for b, d_ in zip(bhs, ud)]
        ku = [_mm_tn(jnp.concatenate([dirs[0][2][0, b], dirs[1][2][0, b]], axis=0), d_) for b, d_ in zip(bhs, ud)]
        for b, r_, pu_, ku_, s_ in zip(bhs, r, pu, ku, st):
            for dr in range(2):
                cs = slice(dr * HEAD_DIM, (dr + 1) * HEAD_DIM)
                dirs[dr][5][b] = r_[dr][CHUNK:] + pu_[:, cs]
                state_ref[dr, b] = s_[dr] * dirs[dr][4][0, b, 0] + ku_[:, cs]
        return carry

    lax.fori_loop(0, nbh // GDN_SCAN_GROUP, body, 0)


def _gdn_scan(w, kq, kd, p, egl):
    _, nbh, s, dh = w.shape
    nchunk = s // CHUNK

    def specs(dr):
        def im(n):
            return (dr, 0, n if dr == 0 else nchunk - 1 - n, 0)

        def im5(n):
            return im(n) + (0,)
        return [pl.BlockSpec((1, nbh, CHUNK, dh), im), pl.BlockSpec((1, nbh, 1, 2 * CHUNK, dh), im5),
                pl.BlockSpec((1, nbh, CHUNK, dh), im), pl.BlockSpec((1, nbh, CHUNK, dh), im),
                pl.BlockSpec((1, nbh, 1, 1, dh), im5)]

    o_shape = jax.ShapeDtypeStruct((nbh, s, dh), F32)
    return pl.pallas_call(
        _gdn_scan_body,
        grid=(nchunk,),
        in_specs=specs(0) + specs(1),
        out_specs=[pl.BlockSpec((nbh, CHUNK, dh), lambda n: (0, n, 0)),
                   pl.BlockSpec((nbh, CHUNK, dh), lambda n: (0, nchunk - 1 - n, 0))],
        out_shape=[o_shape, o_shape],
        scratch_shapes=[pltpu.VMEM((2, nbh, dh, dh), F32)],
        compiler_params=_params("arbitrary"),
        name="gdn_scan",
    )(w, kq, kd, p, egl, w, kq, kd, p, egl)


TOKEN_TILE_ROWS = 8
DMA_ISSUE_UNROLL = 8


def _to_token_tiles(dst_ref, val):
    n = val.shape[0]
    for j in range(TOKEN_TILE_ROWS):
        dst_ref[pl.ds(j, n, stride=TOKEN_TILE_ROWS), :] = val[:, j * LANES:(j + 1) * LANES]


def _from_token_tiles(src_ref, n):
    return jnp.concatenate([src_ref[pl.ds(j, n, stride=TOKEN_TILE_ROWS), :] for j in range(TOKEN_TILE_ROWS)],
                           axis=1)


def _gdn_out_body(of_ref, ob_ref, z_ref, x_ref, on_ref, wo_ref, fg_ref, rwt_ref,
                  xo_ref, pos_ref, wcol_ref, cnt_ref, xs_ref,
                  act_ref, tile_ref, zero_ref, carry_ref, pos_smem, cnt_smem, sem, psem, zsem,
                  *, cap, tm):
    g = pl.program_id(0)
    ng = pl.num_programs(0)
    ts = x_ref.shape[0]
    slot = g % 2
    rpt = TOKEN_TILE_ROWS

    def wait_rows(s):
        for _ in range(2):
            pltpu.make_async_copy(tile_ref.at[s], xs_ref.at[pl.ds(0, ts * rpt)], sem.at[s]).wait()

    @pl.when(g == 0)
    def _():
        carry_ref[...] = jnp.zeros_like(carry_ref)

    for h in range(N_HEADS):
        hs = slice(h * HEAD_DIM, (h + 1) * HEAD_DIM)
        o = of_ref[0, h] + ob_ref[0, h]
        o = o * lax.rsqrt(jnp.mean(o * o, axis=-1, keepdims=True) + RMS_EPS) * on_ref[...]
        act_ref[:, hs] = (o * _silu(z_ref[:, hs])).astype(BF16)
    x = x_ref[...] + _mm(act_ref[...], wo_ref[...])
    xo_ref[...] = x
    hn = _rms(x, fg_ref[...])

    logits = lax.dot_general(rwt_ref[...], hn, (((1,), (1,)), ((), ())),
                             preferred_element_type=F32, precision=lax.Precision.HIGHEST)
    eidx = lax.broadcasted_iota(jnp.int32, logits.shape, 0).astype(F32)
    neg = jnp.float32(-jnp.inf)
    m1 = jnp.max(logits, axis=0, keepdims=True)
    i1 = jnp.min(jnp.where(logits == m1, eidx, float(N_EXPERTS)), axis=0, keepdims=True)
    one1 = eidx == i1
    rest = jnp.where(one1, neg, logits)
    m2 = jnp.max(rest, axis=0, keepdims=True)
    i2 = jnp.min(jnp.where(rest == m2, eidx, float(N_EXPERTS)), axis=0, keepdims=True)
    one2 = eidx == i2
    e2 = jnp.exp(m2 - m1)
    w1 = 1.0 / (1.0 + e2)
    w2 = e2 * w1

    chosen = jnp.where(one1 | one2, 1.0, 0.0)
    ri = lax.broadcasted_iota(jnp.int32, (ts, ts), 0)
    ci = lax.broadcasted_iota(jnp.int32, (ts, ts), 1)
    before = jnp.where(ri < ci, 1.0, 0.0).astype(BF16)
    rank = jnp.dot(chosen.astype(BF16), before, preferred_element_type=F32)
    carry = carry_ref[...]
    slot_f = eidx * float(cap) + carry[:, 0:1] + rank
    p1 = jnp.sum(jnp.where(one1, slot_f, 0.0), axis=0, keepdims=True)
    p2 = jnp.sum(jnp.where(one2, slot_f, 0.0), axis=0, keepdims=True)
    carry = carry + jnp.sum(chosen, axis=1, keepdims=True)
    carry_ref[...] = carry
    cnt_ref[...] = carry.astype(jnp.int32)
    row8 = lax.broadcasted_iota(jnp.int32, (N_EXPERTS, ts), 0)
    pos = jnp.where(row8 == 0, p1, jnp.where(row8 == 1, p2, 0.0)).astype(jnp.int32)
    pos_ref[...] = pos
    row128 = lax.broadcasted_iota(jnp.int32, (LANES, ts), 0)
    wcol_ref[...] = jnp.where(row128 == 0, w1, jnp.where(row128 == 1, w2, 0.0)).T

    cp = pltpu.make_async_copy(pos_ref, pos_smem, psem)
    cp.start()
    cp.wait()
    for sl in range(2):
        @pl.when(slot == sl)
        def _(sl=sl):
            @pl.when(g >= 2)
            def _():
                wait_rows(sl)

            _to_token_tiles(tile_ref.at[sl], hn)

            def issue(t, c):
                src = tile_ref.at[sl, pl.ds(t * rpt, rpt)]
                for k in range(2):
                    dst = xs_ref.at[pl.ds(pos_smem[k, t] * rpt, rpt)]
                    pltpu.make_async_copy(src, dst, sem.at[sl]).start()
                return c

            lax.fori_loop(0, ts, issue, 0, unroll=DMA_ISSUE_UNROLL)

    @pl.when(g == ng - 1)
    def _():
        wait_rows(slot)

        @pl.when(g >= 1)
        def _():
            wait_rows(1 - slot)

        zero_ref[...] = jnp.zeros_like(zero_ref)
        cc = pltpu.make_async_copy(cnt_ref, cnt_smem, psem)
        cc.start()
        cc.wait()
        tails = [pltpu.make_async_copy(
            zero_ref, xs_ref.at[pl.ds((e * cap + cnt_smem[e, 0]) * rpt, tm * rpt)], zsem)
            for e in range(N_EXPERTS)]
        for c in tails:
            c.start()
        for c in tails:
            c.wait()


def _gdn_out(o_f, o_b, z2, x2, o_norm, w_out, ffn_g, router_t, ts, tm, cap):
    t, d = x2.shape
    s = o_f.shape[2]
    spb = s // ts
    rpt = TOKEN_TILE_ROWS
    o_spec = pl.BlockSpec((1, N_HEADS, ts, HEAD_DIM), lambda g: (g // spb, 0, g % spb, 0))
    return pl.pallas_call(
        functools.partial(_gdn_out_body, cap=cap, tm=tm),
        grid=(t // ts,),
        in_specs=[o_spec, o_spec,
                  pl.BlockSpec((ts, d), lambda g: (g, 0)), pl.BlockSpec((ts, d), lambda g: (g, 0)),
                  _const_spec((1, HEAD_DIM)), _const_spec((d, d)), _const_spec((1, d)),
                  _const_spec((N_EXPERTS, d))],
        out_specs=[pl.BlockSpec((ts, d), lambda g: (g, 0)),
                   pl.BlockSpec((N_EXPERTS, ts), lambda g: (0, g)),
                   pl.BlockSpec((ts, LANES), lambda g: (g, 0)),
                   pl.BlockSpec((N_EXPERTS, LANES), lambda g: (0, 0)),
                   pl.BlockSpec(memory_space=pl.ANY)],
        out_shape=[jax.ShapeDtypeStruct((t, d), F32),
                   jax.ShapeDtypeStruct((N_EXPERTS, t), jnp.int32),
                   jax.ShapeDtypeStruct((t, LANES), F32),
                   jax.ShapeDtypeStruct((N_EXPERTS, LANES), jnp.int32),
                   jax.ShapeDtypeStruct((N_EXPERTS * cap * rpt, LANES), F32)],
        scratch_shapes=[pltpu.VMEM((ts, d), BF16),
                        pltpu.VMEM((2, ts * rpt, LANES), F32),
                        pltpu.VMEM((tm * rpt, LANES), F32),
                        pltpu.VMEM((N_EXPERTS, LANES), F32),
                        pltpu.SMEM((N_EXPERTS, ts), jnp.int32),
                        pltpu.SMEM((N_EXPERTS, LANES), jnp.int32),
                        pltpu.SemaphoreType.DMA((2,)), pltpu.SemaphoreType.DMA, pltpu.SemaphoreType.DMA],
        compiler_params=_params("arbitrary"),
        name="gdn_out_router",
    )(o_f, o_b, z2, x2, o_norm, w_out, ffn_g, router_t)


def _moe_body(te_ref, tb_ref, nu_ref, xs_ref, wg_ref, wu_ref, wd_ref, ys_ref, xb_ref, acc_ref, *, fc):
    i = pl.program_id(0)
    f = pl.program_id(1)
    tm = xb_ref.shape[0]
    tf = wg_ref.shape[2]

    @pl.when((i == 0) & (f == 0))
    def _():
        acc_ref[...] = jnp.zeros_like(acc_ref)

    @pl.when(i < nu_ref[0])
    def _():
        @pl.when(f == 0)
        def _():
            xb_ref[...] = _from_token_tiles(xs_ref, tm).astype(BF16)

        x = xb_ref[...]
        for c in range(tf // fc):
            cs = slice(c * fc, (c + 1) * fc)
            a = _silu(_mm(x, wg_ref[0, :, cs])) * _mm(x, wu_ref[0, :, cs])
            y = _mm(a, wd_ref[0, cs, :])
            if c == 0:
                acc_ref[...] = jnp.where(f > 0, acc_ref[...], 0.0) + y
            else:
                acc_ref[...] += y

        @pl.when(f == pl.num_programs(1) - 1)
        def _():
            _to_token_tiles(ys_ref, acc_ref[...])


def _moe_grouped(xs, tile_expert, tile_block, n_used, wg, wu, wd, tm, tf, fc):
    ne, d, dff = wg.shape
    nt = tile_expert.shape[0]
    nf = dff // tf
    rpt = TOKEN_TILE_ROWS

    def fsel(i, f, nu):
        return jnp.where(i < nu[0], f, nf - 1)

    grid_spec = pltpu.PrefetchScalarGridSpec(
        num_scalar_prefetch=3,
        grid=(nt, nf),
        in_specs=[pl.BlockSpec((tm * rpt, LANES), lambda i, f, te, tb, nu: (tb[i], 0)),
                  pl.BlockSpec((1, d, tf), lambda i, f, te, tb, nu: (te[i], 0, fsel(i, f, nu))),
                  pl.BlockSpec((1, d, tf), lambda i, f, te, tb, nu: (te[i], 0, fsel(i, f, nu))),
                  pl.BlockSpec((1, tf, d), lambda i, f, te, tb, nu: (te[i], fsel(i, f, nu), 0))],
        out_specs=pl.BlockSpec((tm * rpt, LANES), lambda i, f, te, tb, nu: (tb[i], 0)),
        scratch_shapes=[pltpu.VMEM((tm, d), BF16), pltpu.VMEM((tm, d), F32)],
    )
    return pl.pallas_call(
        functools.partial(_moe_body, fc=fc),
        grid_spec=grid_spec,
        out_shape=jax.ShapeDtypeStruct(xs.shape, F32),
        compiler_params=_params("arbitrary", "arbitrary"),
        name="moe_grouped",
    )(tile_expert, tile_block, n_used, xs, wg, wu, wd)


def _combine_body(pos_ref, posn_ref, x_ref, wcol_ref, fn_ref, ys_ref, o_ref, gbuf, pos_smem, sem, psem):
    i = pl.program_id(0)
    n = pl.num_programs(0)
    tc = x_ref.shape[0]
    rpt = TOKEN_TILE_ROWS

    def fetch(p_ref, sl):
        cp = pltpu.make_async_copy(p_ref, pos_smem, psem)
        cp.start()
        cp.wait()

        def issue(t, c):
            for k in range(2):
                src = ys_ref.at[pl.ds(pos_smem[k, t] * rpt, rpt)]
                pltpu.make_async_copy(src, gbuf.at[sl, k, pl.ds(t * rpt, rpt)], sem.at[sl]).start()
            return c

        lax.fori_loop(0, tc, issue, 0, unroll=DMA_ISSUE_UNROLL)

    @pl.when(i == 0)
    def _():
        fetch(pos_ref, 0)

    for sl in range(2):
        @pl.when((i + 1 < n) & ((i + 1) % 2 == sl))
        def _(sl=sl):
            fetch(posn_ref, sl)

    for sl in range(2):
        @pl.when(i % 2 == sl)
        def _(sl=sl):
            for k in range(2):
                pltpu.make_async_copy(ys_ref.at[pl.ds(0, tc * rpt)], gbuf.at[sl, k], sem.at[sl]).wait()
            wcol = wcol_ref[...]
            y = (x_ref[...] + wcol[:, 0:1] * _from_token_tiles(gbuf.at[sl, 0], tc)
                 + wcol[:, 1:2] * _from_token_tiles(gbuf.at[sl, 1], tc))
            o_ref[...] = _rms(y, fn_ref[...])


def _moe_combine(pos, x2, wcol, final_g, ys, tc):
    t, d = x2.shape
    n = t // tc
    rpt = TOKEN_TILE_ROWS
    return pl.pallas_call(
        _combine_body,
        grid=(n,),
        in_specs=[pl.BlockSpec((N_EXPERTS, tc), lambda i: (0, i)),
                  pl.BlockSpec((N_EXPERTS, tc), lambda i: (0, jnp.minimum(i + 1, n - 1))),
                  pl.BlockSpec((tc, d), lambda i: (i, 0)),
                  pl.BlockSpec((tc, LANES), lambda i: (i, 0)),
                  _const_spec((1, d)),
                  pl.BlockSpec(memory_space=pl.ANY)],
        out_specs=pl.BlockSpec((tc, d), lambda i: (i, 0)),
        out_shape=jax.ShapeDtypeStruct((t, d), F32),
        scratch_shapes=[pltpu.VMEM((2, 2, tc * rpt, LANES), F32),
                        pltpu.SMEM((N_EXPERTS, tc), jnp.int32),
                        pltpu.SemaphoreType.DMA((2,)), pltpu.SemaphoreType.DMA],
        compiler_params=_params("arbitrary"),
        name="moe_combine_norm",
    )(pos, pos, x2, wcol, final_g, ys)


def _tile_schedule(counts, tm, cap, nt):
    ntile = (counts + tm - 1) // tm
    ends = jnp.cumsum(ntile)
    n_used = ends[-1]
    i = jnp.minimum(jnp.arange(nt, dtype=jnp.int32), jnp.maximum(n_used - 1, 0))
    te = jnp.sum((i[:, None] >= ends[None, :]).astype(jnp.int32), axis=1)
    tb = te * (cap // tm) + i - (ends - ntile)[te]
    return te.astype(jnp.int32), tb.astype(jnp.int32), n_used.reshape(1).astype(jnp.int32)


def _row(v):
    return v.reshape(1, -1).astype(F32)


def _pad_cols(w, n):
    return jnp.pad(w, ((0, 0), (0, n - w.shape[1])))


def _conformer_layer(x3, mix_g, ffn_g, pw1_w, pw1_b, dw_w, dw_b, ln_g, ln_b, pw2_w, pw2_b,
                     w_gate, w_up, w_down):
    bsz, s, d = x3.shape
    t = bsz * s
    tm = min(512, t)
    ts = min(512, s)
    x3 = _conf_mixer(x3, _row(mix_g), pw1_w.astype(BF16), _row(pw1_b), dw_w, _row(dw_b), _row(ln_g),
                     _row(ln_b), pw2_w.astype(BF16), _row(pw2_b), ts)
    x2 = _ffn(x3.reshape(t, d), _row(ffn_g), w_gate.astype(BF16), w_up.astype(BF16),
              w_down.astype(BF16), tm, 256)
    return x2.reshape(bsz, s, d)


def _deltanet_moe_layer(x3, mix_g, ffn_g, w_in, conv_w, a_log, dt_bias, o_norm, w_out,
                        router, e_gate, e_up, e_down, final_g):
    bsz, s, d = x3.shape
    t = bsz * s
    tm = min(512, t)
    ts = min(512, s)
    nmain = 4 * d
    zero16 = jnp.zeros((2 * N_HEADS,), F32)
    alog_row = _row(_pad_cols(jnp.concatenate([zero16, a_log.reshape(-1)])[None], LANES))
    dtb_row = _row(_pad_cols(jnp.concatenate([zero16, dt_bias.reshape(-1)])[None], LANES))
    z, w, kq, kd, p, egl = _gdn_front(x3, _row(mix_g), w_in.astype(BF16),
                                      _pad_cols(w_in[:, nmain:], LANES).astype(BF16), conv_w,
                                      alog_row, dtb_row, min(256, s))
    z = z.reshape(t, d)
    nbh = bsz * N_HEADS
    nck = s // CHUNK
    o_f, o_b = _gdn_scan(w.reshape(2, nbh, s, HEAD_DIM), kq.reshape(2, nbh, nck, 2 * CHUNK, HEAD_DIM),
                         kd.reshape(2, nbh, s, HEAD_DIM), p.reshape(2, nbh, s, HEAD_DIM),
                         egl.reshape(2, nbh, nck, 1, HEAD_DIM))
    cap = t + tm
    x2, pos, wcol, cnt, xs = _gdn_out(o_f.reshape(bsz, N_HEADS, s, HEAD_DIM), o_b.reshape(bsz, N_HEADS, s, HEAD_DIM),
                                      z, x3.reshape(t, d), _row(o_norm), w_out.astype(BF16), _row(ffn_g),
                                      router.T, ts, tm, cap)
    te, tb, n_used = _tile_schedule(cnt[:, 0], tm, cap, 2 * t // tm + N_EXPERTS)
    dffe = e_gate.shape[2]
    tf = dffe // 2 if (dffe // 2) % 256 == 0 else dffe
    ys = _moe_grouped(xs, te, tb, n_used, e_gate.astype(BF16), e_up.astype(BF16), e_down.astype(BF16),
                      tm, tf, 256)
    out = _moe_combine(pos, x2, wcol, _row(final_g), ys, ts)
    return out.reshape(bsz, s, d)


def kernel(x, mix_norm, ffn_norm, cf_pw1_w, cf_pw1_b, cf_dw_w, cf_dw_b, cf_ln_g, cf_ln_b, cf_pw2_w, cf_pw2_b, ffn_w_gate, ffn_w_up, ffn_w_down, gdn_w_in, gdn_conv_w, gdn_a_log, gdn_dt_bias, gdn_o_norm, gdn_w_out, moe_router, moe_w_gate, moe_w_up, moe_w_down, final_norm):
    x = _conformer_layer(x, mix_norm[0], ffn_norm[0], cf_pw1_w[0], cf_pw1_b[0], cf_dw_w[0], cf_dw_b[0],
                         cf_ln_g[0], cf_ln_b[0], cf_pw2_w[0], cf_pw2_b[0],
                         ffn_w_gate[0], ffn_w_up[0], ffn_w_down[0])
    return _deltanet_moe_layer(x, mix_norm[1], ffn_norm[1], gdn_w_in[0], gdn_conv_w[0], gdn_a_log[0],
                               gdn_dt_bias[0], gdn_o_norm[0], gdn_w_out[0], moe_router[0],
                               moe_w_gate[0], moe_w_up[0], moe_w_down[0], final_norm)
```

```python
import functools

import jax
import jax.numpy as jnp
from jax import lax
from jax.experimental import pallas as pl
from jax.experimental.pallas import tpu as pltpu

F32 = jnp.float32
BF16 = jnp.bfloat16

RMS_EPS = 1e-6
LN_EPS = 1e-5
L2_EPS = 1e-6
N_HEADS = 8
HEAD_DIM = 128
CHUNK = 64
N_EXPERTS = 8
LANES = 128
CONV_HALO = 16
SHORT_HALO = 8
VMEM_LIMIT_BYTES = 56 * 1024 * 1024


def _params(*sem):
    return pltpu.CompilerParams(dimension_semantics=sem, vmem_limit_bytes=VMEM_LIMIT_BYTES)


def _const_spec(shape):
    nd = len(shape)
    return pl.BlockSpec(shape, lambda *_: (0,) * nd, pipeline_mode=pl.Buffered(1))


def _rms(x, g):
    return x * lax.rsqrt(jnp.mean(x * x, axis=-1, keepdims=True) + RMS_EPS) * g


def _silu(x):
    return x * jax.nn.sigmoid(x)


def _mm(a, b):
    return jnp.dot(a.astype(BF16), b.astype(BF16), preferred_element_type=F32)


def _mm_nt(a, b):
    return lax.dot_general(a.astype(BF16), b.astype(BF16), (((1,), (1,)), ((), ())),
                           preferred_element_type=F32)


def _mm_tn(a, b):
    return lax.dot_general(a.astype(BF16), b.astype(BF16), (((0,), (0,)), ((), ())),
                           preferred_element_type=F32)


def _mm_exact(a, b):
    return jnp.dot(a, b, preferred_element_type=F32, precision=lax.Precision.HIGHEST)


def _conf_mixer_body(x_ref, xp_ref, xn_ref, g_ref, w1_ref, b1_ref, dw_ref, dwb_ref, lng_ref, lnb_ref,
                     w2_ref, b2_ref, o_ref, xe_ref, ext_ref, cv_ref, *, nc):
    s = pl.program_id(1)
    ns = pl.num_programs(1)
    ts, d = cv_ref.shape
    width = dw_ref.shape[0]
    base = CONV_HALO - width // 2
    span = ts + 8 * ((base + width - 1) // 8)
    xe_ref[0:CONV_HALO, :] = xp_ref[0]
    xe_ref[CONV_HALO:CONV_HALO + ts, :] = x_ref[0]
    xe_ref[CONV_HALO + ts:, :] = xn_ref[0]
    h = _rms(xe_ref[...], g_ref[...]).astype(BF16)
    for c in range(d // nc):
        a = _mm(h, w1_ref[:, c * nc:(c + 1) * nc]) + b1_ref[:, c * nc:(c + 1) * nc]
        b = _mm(h, w1_ref[:, d + c * nc:d + (c + 1) * nc]) + b1_ref[:, d + c * nc:d + (c + 1) * nc]
        ext_ref[0, :, c * nc:(c + 1) * nc] = a * jax.nn.sigmoid(b)

    @pl.when(s == 0)
    def _():
        ext_ref[0, 0:CONV_HALO, :] = jnp.zeros((CONV_HALO, d), F32)

    @pl.when(s == ns - 1)
    def _():
        ext_ref[0, CONV_HALO + ts:, :] = jnp.zeros((CONV_HALO, d), F32)

    for p in range(1, 8):
        ext_ref[p, 0:span, :] = ext_ref[0, pl.ds(p, span), :]
    rb = 128
    for c in range(d // LANES):
        cs = slice(c * LANES, (c + 1) * LANES)
        for r in range(ts // rb):
            acc = jnp.zeros((rb, LANES), F32)
            for k in range(width):
                off = base + k
                acc = acc + dw_ref[k:k + 1, cs] * ext_ref[off % 8, pl.ds(r * rb + 8 * (off // 8), rb), cs]
            cv_ref[r * rb:(r + 1) * rb, cs] = acc + dwb_ref[:, cs]
    y = cv_ref[...]
    mu = jnp.mean(y, axis=-1, keepdims=True)
    yc = y - mu
    yn = yc * lax.rsqrt(jnp.mean(yc * yc, axis=-1, keepdims=True) + LN_EPS) * lng_ref[...] + lnb_ref[...]
    o_ref[0] = x_ref[0] + _mm(_silu(yn), w2_ref[...]) + b2_ref[...]


def _conf_mixer(x3, g, w1, b1, dw_w, dw_b, ln_g, ln_b, w2, b2, ts):
    bsz, s, d = x3.shape
    width = dw_w.shape[0]
    hb = ts // CONV_HALO
    nhb = s // CONV_HALO
    return pl.pallas_call(
        functools.partial(_conf_mixer_body, nc=256),
        grid=(bsz, s // ts),
        in_specs=[pl.BlockSpec((1, ts, d), lambda b, i: (b, i, 0)),
                  pl.BlockSpec((1, CONV_HALO, d), lambda b, i: (b, jnp.maximum(i * hb - 1, 0), 0)),
                  pl.BlockSpec((1, CONV_HALO, d), lambda b, i: (b, jnp.minimum((i + 1) * hb, nhb - 1), 0)),
                  _const_spec((1, d)), _const_spec((d, 2 * d)), _const_spec((1, 2 * d)),
                  _const_spec((width, d)), _const_spec((1, d)), _const_spec((1, d)), _const_spec((1, d)),
                  _const_spec((d, d)), _const_spec((1, d))],
        out_specs=pl.BlockSpec((1, ts, d), lambda b, i: (b, i, 0)),
        out_shape=jax.ShapeDtypeStruct((bsz, s, d), F32),
        scratch_shapes=[pltpu.VMEM((ts + 2 * CONV_HALO, d), F32),
                        pltpu.VMEM((8, ts + 2 * CONV_HALO, d), F32), pltpu.VMEM((ts, d), F32)],
        compiler_params=_params("parallel", "parallel"),
        name="conf_mixer",
    )(x3, x3, x3, g, w1, b1, dw_w, dw_b, ln_g, ln_b, w2, b2)


def _ffn_body(x_ref, g_ref, wg_ref, wu_ref, wd_ref, o_ref, *, fc):
    x = x_ref[...]
    h = _rms(x, g_ref[...]).astype(BF16)
    dff = wg_ref.shape[1]
    acc = x
    for c in range(dff // fc):
        cs = slice(c * fc, (c + 1) * fc)
        a = _silu(_mm(h, wg_ref[:, cs])) * _mm(h, wu_ref[:, cs])
        acc = acc + _mm(a, wd_ref[cs, :])
    o_ref[...] = acc


def _ffn(x2d, g, wg, wu, wd, tm, fc):
    t, d = x2d.shape
    dff = wg.shape[1]
    return pl.pallas_call(
        functools.partial(_ffn_body, fc=fc),
        grid=(t // tm,),
        in_specs=[pl.BlockSpec((tm, d), lambda i: (i, 0)), _const_spec((1, d)),
                  _const_spec((d, dff)), _const_spec((d, dff)), _const_spec((dff, d))],
        out_specs=pl.BlockSpec((tm, d), lambda i: (i, 0)),
        out_shape=jax.ShapeDtypeStruct((t, d), F32),
        compiler_params=_params("parallel"),
        name="dense_swiglu",
    )(x2d, g, wg, wu, wd)


PAIR = 2 * CHUNK


def _block_diag(yb, mk_ref):
    return jnp.concatenate([yb * mk_ref[MASK_LEFT].astype(BF16), yb * mk_ref[MASK_RIGHT].astype(BF16)], axis=0)


def _tri_inverse_pairs(mats, mk_ref):
    def bd(vals):
        return [_block_diag(v.astype(BF16), mk_ref) for v in vals]

    l0 = [a * mk_ref[MASK_BLOCK8] for a in mats]
    l0b = [a.astype(BF16) for a in l0]
    l2 = [_mm(a, d) for a, d in zip(l0b, bd(l0b))]
    l2b = [a.astype(BF16) for a in l2]
    l2d = bd(l2b)
    l4 = [_mm(a, d) for a, d in zip(l2b, l2d)]
    l3 = [_mm(a, d) for a, d in zip(l0b, l2d)]
    xs = [mk_ref[MASK_EYE] - a + b - t for a, b, t in zip(l0, l2, l3)]
    x4 = [_mm(x, d) for x, d in zip(xs, bd(l4))]
    xs = [x + t for x, t in zip(xs, x4)]
    for level in range(MASK_MERGE0, MASK_MERGE0 + 3):
        xb = [x.astype(BF16) for x in xs]
        t1 = [_mm(a * mk_ref[level], d) for a, d in zip(mats, bd(xb))]
        t2 = [_mm(x, d) for x, d in zip(xb, bd(t1))]
        xs = [x - t for x, t in zip(xs, t2)]
    return xs


(MASK_INCL, MASK_INCL_L, MASK_INCL_R, MASK_STRICT_L, MASK_STRICT_R) = (0, 2, 4, 6, 8)
MASK_EYE, MASK_BLOCK8, MASK_MERGE0, MASK_LEFT, MASK_RIGHT = 10, 11, 12, 15, 16
N_MASKS = 17


def _chunk_masks():
    ri = lax.broadcasted_iota(jnp.int32, (CHUNK, PAIR), 0)
    li = lax.broadcasted_iota(jnp.int32, (CHUNK, PAIR), 1)
    ci = li % CHUNK
    left, right = li < CHUNK, li >= CHUNK
    incl = [ci <= ri, ci >= ri]
    strict = [ci < ri, ci > ri]
    masks = (incl + [m & left for m in incl] + [m & right for m in incl]
             + [m & left for m in strict] + [m & right for m in strict]
             + [ci == ri, (ri // 8) == (ci // 8)])
    size = 8
    while size < CHUNK:
        masks.append(((ri // (2 * size)) == (ci // (2 * size))) & ((ri // size) != (ci // size)))
        size *= 2
    masks += [left, right]
    assert len(masks) == N_MASKS
    return jnp.stack(masks).astype(F32)


GDN_HEAD_GROUP = 4


def _gdn_front_body(x_ref, xp_ref, xn_ref, g_ref, w_ref, wgate_ref, cw_ref, alog_ref, dtb_ref, mk_ref,
                    z_ref, w_out_ref, kq_ref, kd_ref, p_ref, egl_ref, xe_ref, pe_ref):
    s = pl.program_id(1)
    ns = pl.num_programs(1)
    tg, d = z_ref.shape[1:]
    kw = cw_ref.shape[0]
    dk = N_HEADS * HEAD_DIM
    npair = tg // PAIR
    gw = GDN_HEAD_GROUP * HEAD_DIM
    ngroup = N_HEADS // GDN_HEAD_GROUP
    base = SHORT_HALO - kw // 2

    xe_ref[0:SHORT_HALO, :] = jnp.where(s > 0, xp_ref[0], 0.0)
    xe_ref[SHORT_HALO:SHORT_HALO + tg, :] = x_ref[0]
    xe_ref[SHORT_HALO + tg:, :] = jnp.where(s < ns - 1, xn_ref[0], 0.0)
    h = _rms(xe_ref[...], g_ref[...]).astype(BF16)
    hm = h[SHORT_HALO:SHORT_HALO + tg]

    gates = _mm(hm, wgate_ref[...])
    beta = jax.nn.sigmoid(gates)
    xa = gates + dtb_ref[...]
    softplus = jnp.maximum(xa, 0.0) + jnp.log1p(jnp.exp(-jnp.abs(xa)))
    log_a = -jnp.exp(alog_ref[...]) * softplus
    ri = lax.broadcasted_iota(jnp.int32, (tg, tg), 0)
    ci = lax.broadcasted_iota(jnp.int32, (tg, tg), 1)
    same = (ri // CHUNK) == (ci // CHUNK)
    cum_f = _mm_exact((same & (ci <= ri)).astype(F32), log_a)
    cum_b = _mm_exact((same & (ci >= ri)).astype(F32), log_a)
    lane = lax.broadcasted_iota(jnp.int32, (tg, LANES), 1)
    g_all = jnp.where(lane < 3 * N_HEADS, cum_f, cum_b)
    g_all_t = g_all.T
    beta_t = beta.T

    def project(grp):
        out = []
        for part in range(3):
            col0 = part * dk + grp * gw
            pe_ref[grp % 2, part] = _mm(h, w_ref[:, col0:col0 + gw])
            heads = []
            for c in range(GDN_HEAD_GROUP):
                col = col0 + c * LANES
                acc = jnp.zeros((tg, LANES), F32)
                for k in range(kw):
                    acc = acc + (cw_ref[k:k + 1, col:col + LANES]
                                 * pe_ref[grp % 2, part, pl.ds(base + k, tg), c * LANES:(c + 1) * LANES])
                a = _silu(acc)
                if part < 2:
                    a = a * lax.rsqrt(jnp.sum(a * a, axis=-1, keepdims=True) + L2_EPS)
                if part == 0:
                    a = a * (HEAD_DIM ** -0.5)
                heads.append(a)
            out.append(heads)
        return out

    def chunk_systems(grp, qkv):
        heads = range(grp * GDN_HEAD_GROUP, (grp + 1) * GDN_HEAD_GROUP)
        qs, ks, vs = (dict(zip(heads, part)) for part in qkv)
        raw, v16 = {}, {}
        for hd in heads:
            k16 = ks[hd].astype(BF16)
            q16 = qs[hd].astype(BF16)
            v16[hd] = vs[hd].astype(BF16)
            for m in range(npair):
                pr = slice(m * PAIR, (m + 1) * PAIR)
                raw[hd, m] = _mm_nt(jnp.concatenate([k16[pr], q16[pr]], axis=0), k16[pr])
        insts = [(hd, dr, m) for hd in heads for dr in range(2) for m in range(npair)]
        zero = jnp.zeros((CHUNK, HEAD_DIM), BF16)
        lows, rhss, beta_rows = [], [], []
        for hd, dr, m in insts:
            pr = slice(m * PAIR, (m + 1) * PAIR)
            bcol = dr * N_HEADS + hd
            gcol = 2 * N_HEADS + dr * N_HEADS + hd
            bt = beta[pr, bcol:bcol + 1]
            beta_rows.append(beta_t[bcol:bcol + 1, pr])
            g_c = g_all[pr, gcol:gcol + 1]
            g_r = g_all_t[gcol:gcol + 1, pr]
            diff = g_c - g_r
            decay = jnp.exp(diff[:CHUNK] * mk_ref[MASK_INCL_L + dr] + diff[CHUNK:] * mk_ref[MASK_INCL_R + dr])
            decay = decay * mk_ref[MASK_INCL + dr]
            kk = raw[hd, m][:PAIR] * bt
            qk = raw[hd, m][PAIR:]
            lows.append((kk[:CHUNK] * mk_ref[MASK_STRICT_L + dr] + kk[CHUNK:] * mk_ref[MASK_STRICT_R + dr]) * decay)
            pq = (qk[:CHUNK] * mk_ref[MASK_LEFT] + qk[CHUNK:] * mk_ref[MASK_RIGHT]) * decay
            half = mk_ref[MASK_LEFT + dr]
            p_tiles = [(t * half).astype(BF16) for t in (pq, pltpu.roll(pq, CHUNK, axis=1))]
            eg = jnp.exp(g_c)
            keg = (ks[hd][pr] * eg).astype(BF16)
            qeg = (qs[hd][pr] * eg).astype(BF16)
            vp = v16[hd][pr]
            rhss.append(jnp.concatenate(
                [jnp.concatenate([vp[:CHUNK], keg[:CHUNK], zero, zero], axis=1),
                 jnp.concatenate([zero, zero, vp[CHUNK:], keg[CHUNK:]], axis=1)], axis=0))
            for j in range(2):
                n = 2 * m + j
                rs = slice(n * CHUNK, (n + 1) * CHUNK)
                js = slice(j * CHUNK, (j + 1) * CHUNK)
                last = n * CHUNK + (CHUNK - 1 if dr == 0 else 0)
                g_l = g_all[last:last + 1, gcol:gcol + 1]
                p_ref[dr, 0, hd, rs, :] = p_tiles[(j + dr) % 2]
                kq_ref[dr, 0, hd, n, CHUNK:, :] = qeg[js]
                kd_ref[dr, 0, hd, rs, :] = (ks[hd][rs] * jnp.exp(g_l - g_all[rs, gcol:gcol + 1])).astype(BF16)
                egl_ref[dr, 0, hd, n] = jnp.broadcast_to(jnp.exp(g_l), (1, HEAD_DIM))
        invs = _tri_inverse_pairs(lows, mk_ref)
        sols = [_mm(inv * br, rhs) for inv, br, rhs in zip(invs, beta_rows, rhss)]
        for (hd, dr, m), sol in zip(insts, sols):
            for j in range(2):
                n = 2 * m + j
                rs = slice(n * CHUNK, (n + 1) * CHUNK)
                w_out_ref[dr, 0, hd, rs, :] = sol[:, 2 * j * HEAD_DIM:(2 * j + 1) * HEAD_DIM]
                kq_ref[dr, 0, hd, n, :CHUNK, :] = sol[:, (2 * j + 1) * HEAD_DIM:(2 * j + 2) * HEAD_DIM].astype(BF16)

    qkv = project(0)
    for grp in range(ngroup):
        if grp + 1 < ngroup:
            nxt = project(grp + 1)
        else:
            z_ref[0] = _mm(hm, w_ref[:, 3 * dk:3 * dk + d])
        chunk_systems(grp, qkv)
        qkv = nxt


def _gdn_front(x3, g, w_in, w_gate, conv_w, alog_row, dtb_row, tg):
    bsz, s, d = x3.shape
    nin = w_in.shape[1]
    kw, nq = conv_w.shape
    nchunk = tg // CHUNK
    hb = tg // SHORT_HALO
    nhb = s // SHORT_HALO
    gw = GDN_HEAD_GROUP * HEAD_DIM

    def rows(dtype, last=HEAD_DIM):
        return (jax.ShapeDtypeStruct((2, bsz, N_HEADS, s, last), dtype),
                pl.BlockSpec((2, 1, N_HEADS, tg, last), lambda b, i: (0, b, 0, i, 0)))

    def per_chunk(dtype, r):
        return (jax.ShapeDtypeStruct((2, bsz, N_HEADS, s // CHUNK, r, HEAD_DIM), dtype),
                pl.BlockSpec((2, 1, N_HEADS, nchunk, r, HEAD_DIM), lambda b, i: (0, b, 0, i, 0, 0)))

    outs = [(jax.ShapeDtypeStruct((bsz, s, d), F32), pl.BlockSpec((1, tg, d), lambda b, i: (b, i, 0))),
            rows(F32), per_chunk(BF16, 2 * CHUNK), rows(BF16), rows(BF16), per_chunk(F32, 1)]
    return pl.pallas_call(
        _gdn_front_body,
        grid=(bsz, s // tg),
        in_specs=[pl.BlockSpec((1, tg, d), lambda b, i: (b, i, 0)),
                  pl.BlockSpec((1, SHORT_HALO, d), lambda b, i: (b, jnp.maximum(i * hb - 1, 0), 0)),
                  pl.BlockSpec((1, SHORT_HALO, d), lambda b, i: (b, jnp.minimum((i + 1) * hb, nhb - 1), 0)),
                  _const_spec((1, d)), _const_spec((d, nin)), _const_spec((d, LANES)),
                  _const_spec((kw, nq)), _const_spec((1, LANES)), _const_spec((1, LANES)),
                  _const_spec((N_MASKS, CHUNK, PAIR))],
        out_specs=[o[1] for o in outs],
        out_shape=[o[0] for o in outs],
        scratch_shapes=[pltpu.VMEM((tg + 2 * SHORT_HALO, d), F32),
                        pltpu.VMEM((2, 3, tg + 2 * SHORT_HALO, gw), F32)],
        compiler_params=_params("parallel", "parallel"),
        name="gdn_front",
    )(x3, x3, x3, g, w_in, w_gate, conv_w, alog_row, dtb_row, _chunk_masks())


GDN_SCAN_GROUP = 8


def _gdn_scan_body(wf, kqf, kdf, pf, eglf, wb, kqb, kdb, pb, eglb, of_ref, ob_ref, state_ref):
    n = pl.program_id(0)

    @pl.when(n == 0)
    def _():
        state_ref[...] = jnp.zeros_like(state_ref)

    nbh = state_ref.shape[1]
    dirs = ((wf, kqf, kdf, pf, eglf, of_ref), (wb, kqb, kdb, pb, eglb, ob_ref))

    def body(i, carry):
        bhs = [i * GDN_SCAN_GROUP + j for j in range(GDN_SCAN_GROUP)]
        zero = jnp.zeros((CHUNK, HEAD_DIM), BF16)
        st = [[state_ref[dr, b] for dr in range(2)] for b in bhs]
        r = [[_mm(dirs[dr][1][0, b, 0], s_[dr]) for dr in range(2)] for b, s_ in zip(bhs, st)]
        ub = [[(dirs[dr][0][0, b] - r_[dr][:CHUNK]).astype(BF16) for dr in range(2)] for b, r_ in zip(bhs, r)]
        ud = [jnp.concatenate([jnp.concatenate([u_[0], zero], axis=1), jnp.concatenate([zero, u_[1]], axis=1)],
                              axis=0) for u_ in ub]
        pu = [_mm(dirs[0][3][0, b] + dirs[1][3][0, b], d_) for b, d_ in zip(bhs, ud)]
        ku = [_mm_tn(jnp.concatenate([dirs[0][2][0, b], dirs[1][2][0, b]], axis=0), d_) for b, d_ in zip(bhs, ud)]
        for b, r_, pu_, ku_, s_ in zip(bhs, r, pu, ku, st):
            for dr in range(2):
                cs = slice(dr * HEAD_DIM, (dr + 1) * HEAD_DIM)
                dirs[dr][5][b] = r_[dr][CHUNK:] + pu_[:, cs]
                state_ref[dr, b] = s_[dr] * dirs[dr][4][0, b, 0] + ku_[:, cs]
        return carry

    lax.fori_loop(0, nbh // GDN_SCAN_GROUP, body, 0)


def _gdn_scan(w, kq, kd, p, egl):
    _, nbh, s, dh = w.shape
    nchunk = s // CHUNK

    def specs(dr):
        def im(n):
            return (dr, 0, n if dr == 0 else nchunk - 1 - n, 0)

        def im5(n):
            return im(n) + (0,)
        return [pl.BlockSpec((1, nbh, CHUNK, dh), im), pl.BlockSpec((1, nbh, 1, 2 * CHUNK, dh), im5),
                pl.BlockSpec((1, nbh, CHUNK, dh), im), pl.BlockSpec((1, nbh, CHUNK, dh), im),
                pl.BlockSpec((1, nbh, 1, 1, dh), im5)]

    o_shape = jax.ShapeDtypeStruct((nbh, s, dh), F32)
    return pl.pallas_call(
        _gdn_scan_body,
        grid=(nchunk,),
        in_specs=specs(0) + specs(1),
        out_specs=[pl.BlockSpec((nbh, CHUNK, dh), lambda n: (0, n, 0)),
                   pl.BlockSpec((nbh, CHUNK, dh), lambda n: (0, nchunk - 1 - n, 0))],
        out_shape=[o_shape, o_shape],
        scratch_shapes=[pltpu.VMEM((2, nbh, dh, dh), F32)],
        compiler_params=_params("arbitrary"),
        name="gdn_scan",
    )(w, kq, kd, p, egl, w, kq, kd, p, egl)


TOKEN_TILE_ROWS = 8
DMA_ISSUE_UNROLL = 8


def _to_token_tiles(dst_ref, val):
    n = val.shape[0]
    for j in range(TOKEN_TILE_ROWS):
        dst_ref[pl.ds(j, n, stride=TOKEN_TILE_ROWS), :] = val[:, j * LANES:(j + 1) * LANES]


def _from_token_tiles(src_ref, n):
    return jnp.concatenate([src_ref[pl.ds(j, n, stride=TOKEN_TILE_ROWS), :] for j in range(TOKEN_TILE_ROWS)],
                           axis=1)


def _gdn_out_body(of_ref, ob_ref, z_ref, x_ref, on_ref, wo_ref, fg_ref, rwt_ref,
                  xo_ref, pos_ref, wcol_ref, cnt_ref, xs_ref,
                  act_ref, tile_ref, zero_ref, carry_ref, pos_smem, cnt_smem, sem, psem, zsem,
                  *, cap, tm):
    g = pl.program_id(0)
    ng = pl.num_programs(0)
    ts = x_ref.shape[0]
    slot = g % 2
    prev = 1 - slot
    rpt = TOKEN_TILE_ROWS

    def wait_rows(s):
        for _ in range(2):
            pltpu.make_async_copy(tile_ref.at[s], xs_ref.at[pl.ds(0, ts * rpt)], sem.at[s]).wait()

    def dispatch_row(t, s):
        src = tile_ref.at[s, pl.ds(t * rpt, rpt)]
        for k in range(2):
            dst = xs_ref.at[pl.ds(pos_smem[s, k, t] * rpt, rpt)]
            pltpu.make_async_copy(src, dst, sem.at[s]).start()

    def tile_step(dispatch_prev):
        if dispatch_prev:
            for t in range(ts):
                dispatch_row(t, prev)
        for h in range(N_HEADS):
            hs = slice(h * HEAD_DIM, (h + 1) * HEAD_DIM)
            o = of_ref[0, h] + ob_ref[0, h]
            o = o * lax.rsqrt(jnp.mean(o * o, axis=-1, keepdims=True) + RMS_EPS) * on_ref[...]
            act_ref[:, hs] = (o * _silu(z_ref[:, hs])).astype(BF16)
        x = x_ref[...] + _mm(act_ref[...], wo_ref[...])
        xo_ref[...] = x
        hn = _rms(x, fg_ref[...])

        logits = lax.dot_general(rwt_ref[...], hn, (((1,), (1,)), ((), ())),
                                 preferred_element_type=F32, precision=lax.Precision.HIGHEST)
        eidx = lax.broadcasted_iota(jnp.int32, logits.shape, 0).astype(F32)
        neg = jnp.float32(-jnp.inf)
        m1 = jnp.max(logits, axis=0, keepdims=True)
        i1 = jnp.min(jnp.where(logits == m1, eidx, float(N_EXPERTS)), axis=0, keepdims=True)
        one1 = eidx == i1
        rest = jnp.where(one1, neg, logits)
        m2 = jnp.max(rest, axis=0, keepdims=True)
        i2 = jnp.min(jnp.where(rest == m2, eidx, float(N_EXPERTS)), axis=0, keepdims=True)
        one2 = eidx == i2
        e2 = jnp.exp(m2 - m1)
        w1 = 1.0 / (1.0 + e2)
        w2 = e2 * w1

        chosen = jnp.where(one1 | one2, 1.0, 0.0)
        ri = lax.broadcasted_iota(jnp.int32, (ts, ts), 0)
        ci = lax.broadcasted_iota(jnp.int32, (ts, ts), 1)
        before = jnp.where(ri < ci, 1.0, 0.0).astype(BF16)
        rank = jnp.dot(chosen.astype(BF16), before, preferred_element_type=F32)
        carry = carry_ref[...]
        slot_f = eidx * float(cap) + carry[:, 0:1] + rank
        p1 = jnp.sum(jnp.where(one1, slot_f, 0.0), axis=0, keepdims=True)
        p2 = jnp.sum(jnp.where(one2, slot_f, 0.0), axis=0, keepdims=True)
        carry = carry + jnp.sum(chosen, axis=1, keepdims=True)
        carry_ref[...] = carry
        cnt_ref[...] = carry.astype(jnp.int32)
        row8 = lax.broadcasted_iota(jnp.int32, (N_EXPERTS, ts), 0)
        pos = jnp.where(row8 == 0, p1, jnp.where(row8 == 1, p2, 0.0)).astype(jnp.int32)
        pos_ref[...] = pos
        row128 = lax.broadcasted_iota(jnp.int32, (LANES, ts), 0)
        wcol_ref[...] = jnp.where(row128 == 0, w1, jnp.where(row128 == 1, w2, 0.0)).T

        _to_token_tiles(tile_ref.at[slot], hn)
        if dispatch_prev:
            wait_rows(prev)
        cp = pltpu.make_async_copy(pos_ref, pos_smem.at[slot], psem)
        cp.start()
        cp.wait()

    @pl.when(g == 0)
    def _():
        carry_ref[...] = jnp.zeros_like(carry_ref)
        tile_step(False)

    @pl.when(g > 0)
    def _():
        tile_step(True)

    @pl.when(g == ng - 1)
    def _():
        lax.fori_loop(0, ts, lambda t, c: (dispatch_row(t, slot), c)[1], 0, unroll=DMA_ISSUE_UNROLL)
        wait_rows(slot)
        zero_ref[...] = jnp.zeros_like(zero_ref)
        cc = pltpu.make_async_copy(cnt_ref, cnt_smem, psem)
        cc.start()
        cc.wait()
        tails = [pltpu.make_async_copy(
            zero_ref, xs_ref.at[pl.ds((e * cap + cnt_smem[e, 0]) * rpt, tm * rpt)], zsem)
            for e in range(N_EXPERTS)]
        for c in tails:
            c.start()
        for c in tails:
            c.wait()


def _gdn_out(o_f, o_b, z2, x2, o_norm, w_out, ffn_g, router_t, ts, tm, cap):
    t, d = x2.shape
    s = o_f.shape[2]
    spb = s // ts
    rpt = TOKEN_TILE_ROWS
    o_spec = pl.BlockSpec((1, N_HEADS, ts, HEAD_DIM), lambda g: (g // spb, 0, g % spb, 0))
    return pl.pallas_call(
        functools.partial(_gdn_out_body, cap=cap, tm=tm),
        grid=(t // ts,),
        in_specs=[o_spec, o_spec,
                  pl.BlockSpec((ts, d), lambda g: (g, 0)), pl.BlockSpec((ts, d), lambda g: (g, 0)),
                  _const_spec((1, HEAD_DIM)), _const_spec((d, d)), _const_spec((1, d)),
                  _const_spec((N_EXPERTS, d))],
        out_specs=[pl.BlockSpec((ts, d), lambda g: (g, 0)),
                   pl.BlockSpec((N_EXPERTS, ts), lambda g: (0, g)),
                   pl.BlockSpec((ts, LANES), lambda g: (g, 0)),
                   pl.BlockSpec((N_EXPERTS, LANES), lambda g: (0, 0)),
                   pl.BlockSpec(memory_space=pl.ANY)],
        out_shape=[jax.ShapeDtypeStruct((t, d), F32),
                   jax.ShapeDtypeStruct((N_EXPERTS, t), jnp.int32),
                   jax.ShapeDtypeStruct((t, LANES), F32),
                   jax.ShapeDtypeStruct((N_EXPERTS, LANES), jnp.int32),
                   jax.ShapeDtypeStruct((N_EXPERTS * cap * rpt, LANES), F32)],
        scratch_shapes=[pltpu.VMEM((ts, d), BF16),
                        pltpu.VMEM((2, ts * rpt, LANES), F32),
                        pltpu.VMEM((tm * rpt, LANES), F32),
                        pltpu.VMEM((N_EXPERTS, LANES), F32),
                        pltpu.SMEM((2, N_EXPERTS, ts), jnp.int32),
                        pltpu.SMEM((N_EXPERTS, LANES), jnp.int32),
                        pltpu.SemaphoreType.DMA((2,)), pltpu.SemaphoreType.DMA, pltpu.SemaphoreType.DMA],
        compiler_params=_params("arbitrary"),
        name="gdn_out_router",
    )(o_f, o_b, z2, x2, o_norm, w_out, ffn_g, router_t)


def _moe_body(te_ref, tb_ref, nu_ref, xs_ref, wg_ref, wu_ref, wd_ref, ys_ref, xb_ref, acc_ref, *, fc):
    i = pl.program_id(0)
    f = pl.program_id(1)
    tm = xb_ref.shape[0]
    tf = wg_ref.shape[2]

    @pl.when((i == 0) & (f == 0))
    def _():
        acc_ref[...] = jnp.zeros_like(acc_ref)

    @pl.when(i < nu_ref[0])
    def _():
        @pl.when(f == 0)
        def _():
            xb_ref[...] = _from_token_tiles(xs_ref, tm).astype(BF16)

        x = xb_ref[...]
        for c in range(tf // fc):
            cs = slice(c * fc, (c + 1) * fc)
            a = _silu(_mm(x, wg_ref[0, :, cs])) * _mm(x, wu_ref[0, :, cs])
            y = _mm(a, wd_ref[0, cs, :])
            if c == 0:
                acc_ref[...] = jnp.where(f > 0, acc_ref[...], 0.0) + y
            else:
                acc_ref[...] += y

        @pl.when(f == pl.num_programs(1) - 1)
        def _():
            _to_token_tiles(ys_ref, acc_ref[...])


def _moe_grouped(xs, tile_expert, tile_block, n_used, wg, wu, wd, tm, tf, fc):
    ne, d, dff = wg.shape
    nt = tile_expert.shape[0]
    nf = dff // tf
    rpt = TOKEN_TILE_ROWS

    def fsel(i, f, nu):
        return jnp.where(i < nu[0], f, nf - 1)

    grid_spec = pltpu.PrefetchScalarGridSpec(
        num_scalar_prefetch=3,
        grid=(nt, nf),
        in_specs=[pl.BlockSpec((tm * rpt, LANES), lambda i, f, te, tb, nu: (tb[i], 0)),
                  pl.BlockSpec((1, d, tf), lambda i, f, te, tb, nu: (te[i], 0, fsel(i, f, nu))),
                  pl.BlockSpec((1, d, tf), lambda i, f, te, tb, nu: (te[i], 0, fsel(i, f, nu))),
                  pl.BlockSpec((1, tf, d), lambda i, f, te, tb, nu: (te[i], fsel(i, f, nu), 0))],
        out_specs=pl.BlockSpec((tm * rpt, LANES), lambda i, f, te, tb, nu: (tb[i], 0)),
        scratch_shapes=[pltpu.VMEM((tm, d), BF16), pltpu.VMEM((tm, d), F32)],
    )
    return pl.pallas_call(
        functools.partial(_moe_body, fc=fc),
        grid_spec=grid_spec,
        out_shape=jax.ShapeDtypeStruct(xs.shape, F32),
        compiler_params=_params("arbitrary", "arbitrary"),
        name="moe_grouped",
    )(tile_expert, tile_block, n_used, xs, wg, wu, wd)


def _combine_body(pos_ref, posn_ref, x_ref, wcol_ref, fn_ref, ys_ref, o_ref, gbuf, pos_smem, sem, psem):
    i = pl.program_id(0)
    n = pl.num_programs(0)
    tc = x_ref.shape[0]
    rpt = TOKEN_TILE_ROWS

    def load_positions(p_ref):
        cp = pltpu.make_async_copy(p_ref, pos_smem, psem)
        cp.start()
        cp.wait()

    def issue(t, sl):
        for k in range(2):
            src = ys_ref.at[pl.ds(pos_smem[k, t] * rpt, rpt)]
            pltpu.make_async_copy(src, gbuf.at[sl, k, pl.ds(t * rpt, rpt)], sem.at[sl]).start()

    def wait_rows(sl):
        for k in range(2):
            pltpu.make_async_copy(ys_ref.at[pl.ds(0, tc * rpt)], gbuf.at[sl, k], sem.at[sl]).wait()

    @pl.when(i == 0)
    def _():
        load_positions(pos_ref)
        lax.fori_loop(0, tc, lambda t, c: (issue(t, 0), c)[1], 0, unroll=DMA_ISSUE_UNROLL)

    for sl in range(2):
        @pl.when(i % 2 == sl)
        def _(sl=sl):
            load_positions(posn_ref)
            wait_rows(sl)
            for t in range(tc):
                issue(t, 1 - sl)
            wcol = wcol_ref[...]
            y = (x_ref[...] + wcol[:, 0:1] * _from_token_tiles(gbuf.at[sl, 0], tc)
                 + wcol[:, 1:2] * _from_token_tiles(gbuf.at[sl, 1], tc))
            o_ref[...] = _rms(y, fn_ref[...])

            @pl.when(i == n - 1)
            def _():
                wait_rows(1 - sl)


def _moe_combine(pos, x2, wcol, final_g, ys, tc):
    t, d = x2.shape
    n = t // tc
    rpt = TOKEN_TILE_ROWS
    return pl.pallas_call(
        _combine_body,
        grid=(n,),
        in_specs=[pl.BlockSpec((N_EXPERTS, tc), lambda i: (0, i)),
                  pl.BlockSpec((N_EXPERTS, tc), lambda i: (0, jnp.minimum(i + 1, n - 1))),
                  pl.BlockSpec((tc, d), lambda i: (i, 0)),
                  pl.BlockSpec((tc, LANES), lambda i: (i, 0)),
                  _const_spec((1, d)),
                  pl.BlockSpec(memory_space=pl.ANY)],
        out_specs=pl.BlockSpec((tc, d), lambda i: (i, 0)),
        out_shape=jax.ShapeDtypeStruct((t, d), F32),
        scratch_shapes=[pltpu.VMEM((2, 2, tc * rpt, LANES), F32),
                        pltpu.SMEM((N_EXPERTS, tc), jnp.int32),
                        pltpu.SemaphoreType.DMA((2,)), pltpu.SemaphoreType.DMA],
        compiler_params=_params("arbitrary"),
        name="moe_combine_norm",
    )(pos, pos, x2, wcol, final_g, ys)


def _tile_schedule(counts, tm, cap, nt):
    ntile = (counts + tm - 1) // tm
    ends = jnp.cumsum(ntile)
    n_used = ends[-1]
    i = jnp.minimum(jnp.arange(nt, dtype=jnp.int32), jnp.maximum(n_used - 1, 0))
    te = jnp.sum((i[:, None] >= ends[None, :]).astype(jnp.int32), axis=1)
    tb = te * (cap // tm) + i - (ends - ntile)[te]
    return te.astype(jnp.int32), tb.astype(jnp.int32), n_used.reshape(1).astype(jnp.int32)


def _row(v):
    return v.reshape(1, -1).astype(F32)


def _pad_cols(w, n):
    return jnp.pad(w, ((0, 0), (0, n - w.shape[1])))


def _conformer_layer(x3, mix_g, ffn_g, pw1_w, pw1_b, dw_w, dw_b, ln_g, ln_b, pw2_w, pw2_b,
                     w_gate, w_up, w_down):
    bsz, s, d = x3.shape
    t = bsz * s
    tm = min(512, t)
    ts = min(512, s)
    x3 = _conf_mixer(x3, _row(mix_g), pw1_w.astype(BF16), _row(pw1_b), dw_w, _row(dw_b), _row(ln_g),
                     _row(ln_b), pw2_w.astype(BF16), _row(pw2_b), ts)
    x2 = _ffn(x3.reshape(t, d), _row(ffn_g), w_gate.astype(BF16), w_up.astype(BF16),
              w_down.astype(BF16), tm, 256)
    return x2.reshape(bsz, s, d)


def _deltanet_moe_layer(x3, mix_g, ffn_g, w_in, conv_w, a_log, dt_bias, o_norm, w_out,
                        router, e_gate, e_up, e_down, final_g):
    bsz, s, d = x3.shape
    t = bsz * s
    tm = min(512, t)
    ts = min(512, s)
    nmain = 4 * d
    zero16 = jnp.zeros((2 * N_HEADS,), F32)
    alog_row = _row(_pad_cols(jnp.concatenate([zero16, a_log.reshape(-1)])[None], LANES))
    dtb_row = _row(_pad_cols(jnp.concatenate([zero16, dt_bias.reshape(-1)])[None], LANES))
    z, w, kq, kd, p, egl = _gdn_front(x3, _row(mix_g), w_in.astype(BF16),
                                      _pad_cols(w_in[:, nmain:], LANES).astype(BF16), conv_w,
                                      alog_row, dtb_row, min(256, s))
    z = z.reshape(t, d)
    nbh = bsz * N_HEADS
    nck = s // CHUNK
    o_f, o_b = _gdn_scan(w.reshape(2, nbh, s, HEAD_DIM), kq.reshape(2, nbh, nck, 2 * CHUNK, HEAD_DIM),
                         kd.reshape(2, nbh, s, HEAD_DIM), p.reshape(2, nbh, s, HEAD_DIM),
                         egl.reshape(2, nbh, nck, 1, HEAD_DIM))
    cap = t + tm
    x2, pos, wcol, cnt, xs = _gdn_out(o_f.reshape(bsz, N_HEADS, s, HEAD_DIM), o_b.reshape(bsz, N_HEADS, s, HEAD_DIM),
                                      z, x3.reshape(t, d), _row(o_norm), w_out.astype(BF16), _row(ffn_g),
                                      router.T, ts, tm, cap)
    te, tb, n_used = _tile_schedule(cnt[:, 0], tm, cap, 2 * t // tm + N_EXPERTS)
    dffe = e_gate.shape[2]
    tf = dffe // 2 if (dffe // 2) % 256 == 0 else dffe
    ys = _moe_grouped(xs, te, tb, n_used, e_gate.astype(BF16), e_up.astype(BF16), e_down.astype(BF16),
                      tm, tf, 256)
    out = _moe_combine(pos, x2, wcol, _row(final_g), ys, ts)
    return out.reshape(bsz, s, d)


def kernel(x, mix_norm, ffn_norm, cf_pw1_w, cf_pw1_b, cf_dw_w, cf_dw_b, cf_ln_g, cf_ln_b, cf_pw2_w, cf_pw2_b, ffn_w_gate, ffn_w_up, ffn_w_down, gdn_w_in, gdn_conv_w, gdn_a_log, gdn_dt_bias, gdn_o_norm, gdn_w_out, moe_router, moe_w_gate, moe_w_up, moe_w_down, final_norm):
    x = _conformer_layer(x, mix_norm[0], ffn_norm[0], cf_pw1_w[0], cf_pw1_b[0], cf_dw_w[0], cf_dw_b[0],
                         cf_ln_g[0], cf_ln_b[0], cf_pw2_w[0], cf_pw2_b[0],
                         ffn_w_gate[0], ffn_w_up[0], ffn_w_down[0])
    return _deltanet_moe_layer(x, mix_norm[1], ffn_norm[1], gdn_w_in[0], gdn_conv_w[0], gdn_a_log[0],
                               gdn_dt_bias[0], gdn_o_norm[0], gdn_w_out[0], moe_router[0],
                               moe_w_gate[0], moe_w_up[0], moe_w_down[0], final_norm)
```

```python
import functools

import jax
import jax.numpy as jnp
from jax import lax
from jax.experimental import pallas as pl
from jax.experimental.pallas import tpu as pltpu

F32 = jnp.float32
BF16 = jnp.bfloat16

RMS_EPS = 1e-6
LN_EPS = 1e-5
L2_EPS = 1e-6
N_HEADS = 8
HEAD_DIM = 128
CHUNK = 64
N_EXPERTS = 8
LANES = 128
CONV_HALO = 16
SHORT_HALO = 8
VMEM_LIMIT_BYTES = 56 * 1024 * 1024


def _params(*sem):
    return pltpu.CompilerParams(dimension_semantics=sem, vmem_limit_bytes=VMEM_LIMIT_BYTES)


def _const_spec(shape):
    nd = len(shape)
    return pl.BlockSpec(shape, lambda *_: (0,) * nd, pipeline_mode=pl.Buffered(1))


def _rms(x, g):
    return x * lax.rsqrt(jnp.mean(x * x, axis=-1, keepdims=True) + RMS_EPS) * g


def _silu(x):
    return x * jax.nn.sigmoid(x)


def _mm(a, b):
    return jnp.dot(a.astype(BF16), b.astype(BF16), preferred_element_type=F32)


def _mm_nt(a, b):
    return lax.dot_general(a.astype(BF16), b.astype(BF16), (((1,), (1,)), ((), ())),
                           preferred_element_type=F32)


def _mm_tn(a, b):
    return lax.dot_general(a.astype(BF16), b.astype(BF16), (((0,), (0,)), ((), ())),
                           preferred_element_type=F32)


def _mm_exact(a, b):
    return jnp.dot(a, b, preferred_element_type=F32, precision=lax.Precision.HIGHEST)


def _conf_mixer_body(x_ref, xp_ref, xn_ref, g_ref, w1_ref, b1_ref, dw_ref, dwb_ref, lng_ref, lnb_ref,
                     w2_ref, b2_ref, o_ref, xe_ref, ext_ref, cv_ref, *, nc):
    s = pl.program_id(1)
    ns = pl.num_programs(1)
    ts, d = cv_ref.shape
    width = dw_ref.shape[0]
    base = CONV_HALO - width // 2
    span = ts + 8 * ((base + width - 1) // 8)
    xe_ref[0:CONV_HALO, :] = xp_ref[0]
    xe_ref[CONV_HALO:CONV_HALO + ts, :] = x_ref[0]
    xe_ref[CONV_HALO + ts:, :] = xn_ref[0]
    h = _rms(xe_ref[...], g_ref[...]).astype(BF16)
    for c in range(d // nc):
        a = _mm(h, w1_ref[:, c * nc:(c + 1) * nc]) + b1_ref[:, c * nc:(c + 1) * nc]
        b = _mm(h, w1_ref[:, d + c * nc:d + (c + 1) * nc]) + b1_ref[:, d + c * nc:d + (c + 1) * nc]
        ext_ref[0, :, c * nc:(c + 1) * nc] = a * jax.nn.sigmoid(b)

    @pl.when(s == 0)
    def _():
        ext_ref[0, 0:CONV_HALO, :] = jnp.zeros((CONV_HALO, d), F32)

    @pl.when(s == ns - 1)
    def _():
        ext_ref[0, CONV_HALO + ts:, :] = jnp.zeros((CONV_HALO, d), F32)

    for p in range(1, 8):
        ext_ref[p, 0:span, :] = ext_ref[0, pl.ds(p, span), :]
    rb = 128
    for c in range(d // LANES):
        cs = slice(c * LANES, (c + 1) * LANES)
        for r in range(ts // rb):
            acc = jnp.zeros((rb, LANES), F32)
            for k in range(width):
                off = base + k
                acc = acc + dw_ref[k:k + 1, cs] * ext_ref[off % 8, pl.ds(r * rb + 8 * (off // 8), rb), cs]
            cv_ref[r * rb:(r + 1) * rb, cs] = acc + dwb_ref[:, cs]
    y = cv_ref[...]
    mu = jnp.mean(y, axis=-1, keepdims=True)
    yc = y - mu
    yn = yc * lax.rsqrt(jnp.mean(yc * yc, axis=-1, keepdims=True) + LN_EPS) * lng_ref[...] + lnb_ref[...]
    o_ref[0] = x_ref[0] + _mm(_silu(yn), w2_ref[...]) + b2_ref[...]


def _conf_mixer(x3, g, w1, b1, dw_w, dw_b, ln_g, ln_b, w2, b2, ts):
    bsz, s, d = x3.shape
    width = dw_w.shape[0]
    hb = ts // CONV_HALO
    nhb = s // CONV_HALO
    return pl.pallas_call(
        functools.partial(_conf_mixer_body, nc=256),
        grid=(bsz, s // ts),
        in_specs=[pl.BlockSpec((1, ts, d), lambda b, i: (b, i, 0)),
                  pl.BlockSpec((1, CONV_HALO, d), lambda b, i: (b, jnp.maximum(i * hb - 1, 0), 0)),
                  pl.BlockSpec((1, CONV_HALO, d), lambda b, i: (b, jnp.minimum((i + 1) * hb, nhb - 1), 0)),
                  _const_spec((1, d)), _const_spec((d, 2 * d)), _const_spec((1, 2 * d)),
                  _const_spec((width, d)), _const_spec((1, d)), _const_spec((1, d)), _const_spec((1, d)),
                  _const_spec((d, d)), _const_spec((1, d))],
        out_specs=pl.BlockSpec((1, ts, d), lambda b, i: (b, i, 0)),
        out_shape=jax.ShapeDtypeStruct((bsz, s, d), F32),
        scratch_shapes=[pltpu.VMEM((ts + 2 * CONV_HALO, d), F32),
                        pltpu.VMEM((8, ts + 2 * CONV_HALO, d), F32), pltpu.VMEM((ts, d), F32)],
        compiler_params=_params("parallel", "parallel"),
        name="conf_mixer",
    )(x3, x3, x3, g, w1, b1, dw_w, dw_b, ln_g, ln_b, w2, b2)


def _ffn_body(x_ref, g_ref, wg_ref, wu_ref, wd_ref, o_ref, *, fc):
    x = x_ref[...]
    h = _rms(x, g_ref[...]).astype(BF16)
    dff = wg_ref.shape[1]
    acc = x
    for c in range(dff // fc):
        cs = slice(c * fc, (c + 1) * fc)
        a = _silu(_mm(h, wg_ref[:, cs])) * _mm(h, wu_ref[:, cs])
        acc = acc + _mm(a, wd_ref[cs, :])
    o_ref[...] = acc


def _ffn(x2d, g, wg, wu, wd, tm, fc):
    t, d = x2d.shape
    dff = wg.shape[1]
    return pl.pallas_call(
        functools.partial(_ffn_body, fc=fc),
        grid=(t // tm,),
        in_specs=[pl.BlockSpec((tm, d), lambda i: (i, 0)), _const_spec((1, d)),
                  _const_spec((d, dff)), _const_spec((d, dff)), _const_spec((dff, d))],
        out_specs=pl.BlockSpec((tm, d), lambda i: (i, 0)),
        out_shape=jax.ShapeDtypeStruct((t, d), F32),
        compiler_params=_params("parallel"),
        name="dense_swiglu",
    )(x2d, g, wg, wu, wd)


PAIR = 2 * CHUNK


def _block_diag(yb, mk_ref):
    return jnp.concatenate([yb * mk_ref[MASK_LEFT].astype(BF16), yb * mk_ref[MASK_RIGHT].astype(BF16)], axis=0)


def _tri_inverse_pairs(mats, mk_ref):
    def bd(vals):
        return [_block_diag(v.astype(BF16), mk_ref) for v in vals]

    l0 = [a * mk_ref[MASK_BLOCK8] for a in mats]
    l0b = [a.astype(BF16) for a in l0]
    l2 = [_mm(a, d) for a, d in zip(l0b, bd(l0b))]
    l2b = [a.astype(BF16) for a in l2]
    l2d = bd(l2b)
    l4 = [_mm(a, d) for a, d in zip(l2b, l2d)]
    l3 = [_mm(a, d) for a, d in zip(l0b, l2d)]
    xs = [mk_ref[MASK_EYE] - a + b - t for a, b, t in zip(l0, l2, l3)]
    x4 = [_mm(x, d) for x, d in zip(xs, bd(l4))]
    xs = [x + t for x, t in zip(xs, x4)]
    for level in range(MASK_MERGE0, MASK_MERGE0 + 3):
        xb = [x.astype(BF16) for x in xs]
        t1 = [_mm(a * mk_ref[level], d) for a, d in zip(mats, bd(xb))]
        t2 = [_mm(x, d) for x, d in zip(xb, bd(t1))]
        xs = [x - t for x, t in zip(xs, t2)]
    return xs


(MASK_INCL, MASK_INCL_L, MASK_INCL_R, MASK_STRICT_L, MASK_STRICT_R) = (0, 2, 4, 6, 8)
MASK_EYE, MASK_BLOCK8, MASK_MERGE0, MASK_LEFT, MASK_RIGHT = 10, 11, 12, 15, 16
N_MASKS = 17


def _chunk_masks():
    ri = lax.broadcasted_iota(jnp.int32, (CHUNK, PAIR), 0)
    li = lax.broadcasted_iota(jnp.int32, (CHUNK, PAIR), 1)
    ci = li % CHUNK
    left, right = li < CHUNK, li >= CHUNK
    incl = [ci <= ri, ci >= ri]
    strict = [ci < ri, ci > ri]
    masks = (incl + [m & left for m in incl] + [m & right for m in incl]
             + [m & left for m in strict] + [m & right for m in strict]
             + [ci == ri, (ri // 8) == (ci // 8)])
    size = 8
    while size < CHUNK:
        masks.append(((ri // (2 * size)) == (ci // (2 * size))) & ((ri // size) != (ci // size)))
        size *= 2
    masks += [left, right]
    assert len(masks) == N_MASKS
    return jnp.stack(masks).astype(F32)


GDN_HEAD_GROUP = 4


def _gdn_front_body(x_ref, xp_ref, xn_ref, g_ref, w_ref, wgate_ref, cw_ref, alog_ref, dtb_ref, mk_ref,
                    z_ref, w_out_ref, kq_ref, kd_ref, p_ref, egl_ref, xe_ref, pe_ref):
    s = pl.program_id(1)
    ns = pl.num_programs(1)
    tg, d = z_ref.shape[1:]
    kw = cw_ref.shape[0]
    dk = N_HEADS * HEAD_DIM
    npair = tg // PAIR
    gw = GDN_HEAD_GROUP * HEAD_DIM
    ngroup = N_HEADS // GDN_HEAD_GROUP
    base = SHORT_HALO - kw // 2

    xe_ref[0:SHORT_HALO, :] = jnp.where(s > 0, xp_ref[0], 0.0)
    xe_ref[SHORT_HALO:SHORT_HALO + tg, :] = x_ref[0]
    xe_ref[SHORT_HALO + tg:, :] = jnp.where(s < ns - 1, xn_ref[0], 0.0)
    h = _rms(xe_ref[...], g_ref[...]).astype(BF16)
    hm = h[SHORT_HALO:SHORT_HALO + tg]

    gates = _mm(hm, wgate_ref[...])
    beta = jax.nn.sigmoid(gates)
    xa = gates + dtb_ref[...]
    softplus = jnp.maximum(xa, 0.0) + jnp.log1p(jnp.exp(-jnp.abs(xa)))
    log_a = -jnp.exp(alog_ref[...]) * softplus
    ri = lax.broadcasted_iota(jnp.int32, (tg, tg), 0)
    ci = lax.broadcasted_iota(jnp.int32, (tg, tg), 1)
    same = (ri // CHUNK) == (ci // CHUNK)
    cum_f = _mm_exact((same & (ci <= ri)).astype(F32), log_a)
    cum_b = _mm_exact((same & (ci >= ri)).astype(F32), log_a)
    lane = lax.broadcasted_iota(jnp.int32, (tg, LANES), 1)
    g_all = jnp.where(lane < 3 * N_HEADS, cum_f, cum_b)
    g_all_t = g_all.T
    beta_t = beta.T

    def project(grp):
        out = []
        for part in range(3):
            col0 = part * dk + grp * gw
            pe_ref[grp % 2, part] = _mm(h, w_ref[:, col0:col0 + gw])
            heads = []
            for c in range(GDN_HEAD_GROUP):
                col = col0 + c * LANES
                acc = jnp.zeros((tg, LANES), F32)
                for k in range(kw):
                    acc = acc + (cw_ref[k:k + 1, col:col + LANES]
                                 * pe_ref[grp % 2, part, pl.ds(base + k, tg), c * LANES:(c + 1) * LANES])
                a = _silu(acc)
                if part < 2:
                    a = a * lax.rsqrt(jnp.sum(a * a, axis=-1, keepdims=True) + L2_EPS)
                if part == 0:
                    a = a * (HEAD_DIM ** -0.5)
                heads.append(a)
            out.append(heads)
        return out

    def chunk_systems(grp, qkv):
        heads = range(grp * GDN_HEAD_GROUP, (grp + 1) * GDN_HEAD_GROUP)
        qs, ks, vs = (dict(zip(heads, part)) for part in qkv)
        raw, v16 = {}, {}
        for hd in heads:
            k16 = ks[hd].astype(BF16)
            q16 = qs[hd].astype(BF16)
            v16[hd] = vs[hd].astype(BF16)
            for m in range(npair):
                pr = slice(m * PAIR, (m + 1) * PAIR)
                raw[hd, m] = _mm_nt(jnp.concatenate([k16[pr], q16[pr]], axis=0), k16[pr])
        insts = [(hd, dr, m) for hd in heads for dr in range(2) for m in range(npair)]
        zero = jnp.zeros((CHUNK, HEAD_DIM), BF16)
        lows, rhss, beta_rows = [], [], []
        for hd, dr, m in insts:
            pr = slice(m * PAIR, (m + 1) * PAIR)
            bcol = dr * N_HEADS + hd
            gcol = 2 * N_HEADS + dr * N_HEADS + hd
            bt = beta[pr, bcol:bcol + 1]
            beta_rows.append(beta_t[bcol:bcol + 1, pr])
            g_c = g_all[pr, gcol:gcol + 1]
            g_r = g_all_t[gcol:gcol + 1, pr]
            diff = g_c - g_r
            decay = jnp.exp(diff[:CHUNK] * mk_ref[MASK_INCL_L + dr] + diff[CHUNK:] * mk_ref[MASK_INCL_R + dr])
            decay = decay * mk_ref[MASK_INCL + dr]
            kk = raw[hd, m][:PAIR] * bt
            qk = raw[hd, m][PAIR:]
            lows.append((kk[:CHUNK] * mk_ref[MASK_STRICT_L + dr] + kk[CHUNK:] * mk_ref[MASK_STRICT_R + dr]) * decay)
            pq = (qk[:CHUNK] * mk_ref[MASK_LEFT] + qk[CHUNK:] * mk_ref[MASK_RIGHT]) * decay
            half = mk_ref[MASK_LEFT + dr]
            p_tiles = [(t * half).astype(BF16) for t in (pq, pltpu.roll(pq, CHUNK, axis=1))]
            eg = jnp.exp(g_c)
            keg = (ks[hd][pr] * eg).astype(BF16)
            qeg = (qs[hd][pr] * eg).astype(BF16)
            vp = v16[hd][pr]
            rhss.append(jnp.concatenate(
                [jnp.concatenate([vp[:CHUNK], keg[:CHUNK], zero, zero], axis=1),
                 jnp.concatenate([zero, zero, vp[CHUNK:], keg[CHUNK:]], axis=1)], axis=0))
            for j in range(2):
                n = 2 * m + j
                rs = slice(n * CHUNK, (n + 1) * CHUNK)
                js = slice(j * CHUNK, (j + 1) * CHUNK)
                last = n * CHUNK + (CHUNK - 1 if dr == 0 else 0)
                g_l = g_all[last:last + 1, gcol:gcol + 1]
                p_ref[dr, 0, hd, rs, :] = p_tiles[(j + dr) % 2]
                kq_ref[dr, 0, hd, n, CHUNK:, :] = qeg[js]
                kd_ref[dr, 0, hd, rs, :] = (ks[hd][rs] * jnp.exp(g_l - g_all[rs, gcol:gcol + 1])).astype(BF16)
                egl_ref[dr, 0, hd, n] = jnp.broadcast_to(jnp.exp(g_l), (1, HEAD_DIM))
        invs = _tri_inverse_pairs(lows, mk_ref)
        sols = [_mm(inv * br, rhs) for inv, br, rhs in zip(invs, beta_rows, rhss)]
        for (hd, dr, m), sol in zip(insts, sols):
            for j in range(2):
                n = 2 * m + j
                rs = slice(n * CHUNK, (n + 1) * CHUNK)
                w_out_ref[dr, 0, hd, rs, :] = sol[:, 2 * j * HEAD_DIM:(2 * j + 1) * HEAD_DIM]
                kq_ref[dr, 0, hd, n, :CHUNK, :] = sol[:, (2 * j + 1) * HEAD_DIM:(2 * j + 2) * HEAD_DIM].astype(BF16)

    qkv = project(0)
    for grp in range(ngroup):
        if grp + 1 < ngroup:
            nxt = project(grp + 1)
        else:
            z_ref[0] = _mm(hm, w_ref[:, 3 * dk:3 * dk + d])
        chunk_systems(grp, qkv)
        qkv = nxt


def _gdn_front(x3, g, w_in, w_gate, conv_w, alog_row, dtb_row, tg):
    bsz, s, d = x3.shape
    nin = w_in.shape[1]
    kw, nq = conv_w.shape
    nchunk = tg // CHUNK
    hb = tg // SHORT_HALO
    nhb = s // SHORT_HALO
    gw = GDN_HEAD_GROUP * HEAD_DIM

    def rows(dtype, last=HEAD_DIM):
        return (jax.ShapeDtypeStruct((2, bsz, N_HEADS, s, last), dtype),
                pl.BlockSpec((2, 1, N_HEADS, tg, last), lambda b, i: (0, b, 0, i, 0)))

    def per_chunk(dtype, r):
        return (jax.ShapeDtypeStruct((2, bsz, N_HEADS, s // CHUNK, r, HEAD_DIM), dtype),
                pl.BlockSpec((2, 1, N_HEADS, nchunk, r, HEAD_DIM), lambda b, i: (0, b, 0, i, 0, 0)))

    outs = [(jax.ShapeDtypeStruct((bsz, s, d), F32), pl.BlockSpec((1, tg, d), lambda b, i: (b, i, 0))),
            rows(F32), per_chunk(BF16, 2 * CHUNK), rows(BF16), rows(BF16), per_chunk(F32, 1)]
    return pl.pallas_call(
        _gdn_front_body,
        grid=(bsz, s // tg),
        in_specs=[pl.BlockSpec((1, tg, d), lambda b, i: (b, i, 0)),
                  pl.BlockSpec((1, SHORT_HALO, d), lambda b, i: (b, jnp.maximum(i * hb - 1, 0), 0)),
                  pl.BlockSpec((1, SHORT_HALO, d), lambda b, i: (b, jnp.minimum((i + 1) * hb, nhb - 1), 0)),
                  _const_spec((1, d)), _const_spec((d, nin)), _const_spec((d, LANES)),
                  _const_spec((kw, nq)), _const_spec((1, LANES)), _const_spec((1, LANES)),
                  _const_spec((N_MASKS, CHUNK, PAIR))],
        out_specs=[o[1] for o in outs],
        out_shape=[o[0] for o in outs],
        scratch_shapes=[pltpu.VMEM((tg + 2 * SHORT_HALO, d), F32),
                        pltpu.VMEM((2, 3, tg + 2 * SHORT_HALO, gw), F32)],
        compiler_params=_params("parallel", "parallel"),
        name="gdn_front",
    )(x3, x3, x3, g, w_in, w_gate, conv_w, alog_row, dtb_row, _chunk_masks())


GDN_SCAN_GROUP = 8


def _gdn_scan_body(wf, kqf, kdf, pf, eglf, wb, kqb, kdb, pb, eglb, of_ref, ob_ref, state_ref):
    n = pl.program_id(0)

    @pl.when(n == 0)
    def _():
        state_ref[...] = jnp.zeros_like(state_ref)

    nbh = state_ref.shape[1]
    dirs = ((wf, kqf, kdf, pf, eglf, of_ref), (wb, kqb, kdb, pb, eglb, ob_ref))

    def body(i, carry):
        bhs = [i * GDN_SCAN_GROUP + j for j in range(GDN_SCAN_GROUP)]
        zero = jnp.zeros((CHUNK, HEAD_DIM), BF16)
        st = [[state_ref[dr, b] for dr in range(2)] for b in bhs]
        r = [[_mm(dirs[dr][1][0, b, 0], s_[dr]) for dr in range(2)] for b, s_ in zip(bhs, st)]
        ub = [[(dirs[dr][0][0, b] - r_[dr][:CHUNK]).astype(BF16) for dr in range(2)] for b, r_ in zip(bhs, r)]
        ud = [jnp.concatenate([jnp.concatenate([u_[0], zero], axis=1), jnp.concatenate([zero, u_[1]], axis=1)],
                              axis=0) for u_ in ub]
        pu = [_mm(dirs[0][3][0, b] + dirs[1][3][0, b], d_) for b, d_ in zip(bhs, ud)]
        ku = [_mm_tn(jnp.concatenate([dirs[0][2][0, b], dirs[1][2][0, b]], axis=0), d_) for b, d_ in zip(bhs, ud)]
        for b, r_, pu_, ku_, s_ in zip(bhs, r, pu, ku, st):
            for dr in range(2):
                cs = slice(dr * HEAD_DIM, (dr + 1) * HEAD_DIM)
                dirs[dr][5][b] = r_[dr][CHUNK:] + pu_[:, cs]
                state_ref[dr, b] = s_[dr] * dirs[dr][4][0, b, 0] + ku_[:, cs]
        return carry

    lax.fori_loop(0, nbh // GDN_SCAN_GROUP, body, 0)


def _gdn_scan(w, kq, kd, p, egl):
    _, nbh, s, dh = w.shape
    nchunk = s // CHUNK

    def specs(dr):
        def im(n):
            return (dr, 0, n if dr == 0 else nchunk - 1 - n, 0)

        def im5(n):
            return im(n) + (0,)
        return [pl.BlockSpec((1, nbh, CHUNK, dh), im), pl.BlockSpec((1, nbh, 1, 2 * CHUNK, dh), im5),
                pl.BlockSpec((1, nbh, CHUNK, dh), im), pl.BlockSpec((1, nbh, CHUNK, dh), im),
                pl.BlockSpec((1, nbh, 1, 1, dh), im5)]

    o_shape = jax.ShapeDtypeStruct((nbh, s, dh), F32)
    return pl.pallas_call(
        _gdn_scan_body,
        grid=(nchunk,),
        in_specs=specs(0) + specs(1),
        out_specs=[pl.BlockSpec((nbh, CHUNK, dh), lambda n: (0, n, 0)),
                   pl.BlockSpec((nbh, CHUNK, dh), lambda n: (0, nchunk - 1 - n, 0))],
        out_shape=[o_shape, o_shape],
        scratch_shapes=[pltpu.VMEM((2, nbh, dh, dh), F32)],
        compiler_params=_params("arbitrary"),
        name="gdn_scan",
    )(w, kq, kd, p, egl, w, kq, kd, p, egl)


TOKEN_TILE_ROWS = 8
DMA_ISSUE_UNROLL = 8


def _to_token_tiles(dst_ref, val):
    n = val.shape[0]
    for j in range(TOKEN_TILE_ROWS):
        dst_ref[pl.ds(j, n, stride=TOKEN_TILE_ROWS), :] = val[:, j * LANES:(j + 1) * LANES]


def _from_token_tiles(src_ref, n):
    return jnp.concatenate([src_ref[pl.ds(j, n, stride=TOKEN_TILE_ROWS), :] for j in range(TOKEN_TILE_ROWS)],
                           axis=1)


def _gdn_out_body(of_ref, ob_ref, z_ref, x_ref, on_ref, wo_ref, fg_ref, rwt_ref,
                  xo_ref, pos_ref, wcol_ref, cnt_ref, xs_ref,
                  act_ref, tile_ref, zero_ref, carry_ref, pos_vmem, pos_smem, cnt_smem, sem, psem, zsem,
                  *, cap, tm):
    g = pl.program_id(0)
    ng = pl.num_programs(0)
    ts = x_ref.shape[0]
    slot = g % 2
    prev = 1 - slot
    rpt = TOKEN_TILE_ROWS

    def wait_rows(s):
        for _ in range(2):
            pltpu.make_async_copy(tile_ref.at[s], xs_ref.at[pl.ds(0, ts * rpt)], sem.at[s]).wait()

    def dispatch_row(t, s):
        src = tile_ref.at[s, pl.ds(t * rpt, rpt)]
        for k in range(2):
            dst = xs_ref.at[pl.ds(pos_smem[s, k, t] * rpt, rpt)]
            pltpu.make_async_copy(src, dst, sem.at[s]).start()

    def positions_to_smem(s):
        return pltpu.make_async_copy(pos_vmem.at[s], pos_smem.at[s], psem)

    def tile_step(dispatch_prev):
        if dispatch_prev:
            positions_to_smem(prev).wait()
            for t in range(ts):
                dispatch_row(t, prev)
        for h in range(N_HEADS):
            hs = slice(h * HEAD_DIM, (h + 1) * HEAD_DIM)
            o = of_ref[0, h] + ob_ref[0, h]
            o = o * lax.rsqrt(jnp.mean(o * o, axis=-1, keepdims=True) + RMS_EPS) * on_ref[...]
            act_ref[:, hs] = (o * _silu(z_ref[:, hs])).astype(BF16)
        x = x_ref[...] + _mm(act_ref[...], wo_ref[...])
        xo_ref[...] = x
        hn = _rms(x, fg_ref[...])

        logits = lax.dot_general(rwt_ref[...], hn, (((1,), (1,)), ((), ())),
                                 preferred_element_type=F32, precision=lax.Precision.HIGHEST)
        eidx = lax.broadcasted_iota(jnp.int32, logits.shape, 0).astype(F32)
        neg = jnp.float32(-jnp.inf)
        m1 = jnp.max(logits, axis=0, keepdims=True)
        i1 = jnp.min(jnp.where(logits == m1, eidx, float(N_EXPERTS)), axis=0, keepdims=True)
        one1 = eidx == i1
        rest = jnp.where(one1, neg, logits)
        m2 = jnp.max(rest, axis=0, keepdims=True)
        i2 = jnp.min(jnp.where(rest == m2, eidx, float(N_EXPERTS)), axis=0, keepdims=True)
        one2 = eidx == i2
        e2 = jnp.exp(m2 - m1)
        w1 = 1.0 / (1.0 + e2)
        w2 = e2 * w1

        chosen = jnp.where(one1 | one2, 1.0, 0.0)
        ri = lax.broadcasted_iota(jnp.int32, (ts, ts), 0)
        ci = lax.broadcasted_iota(jnp.int32, (ts, ts), 1)
        before = jnp.where(ri < ci, 1.0, 0.0).astype(BF16)
        rank = jnp.dot(chosen.astype(BF16), before, preferred_element_type=F32)
        carry = carry_ref[...]
        slot_f = eidx * float(cap) + carry[:, 0:1] + rank
        p1 = jnp.sum(jnp.where(one1, slot_f, 0.0), axis=0, keepdims=True)
        p2 = jnp.sum(jnp.where(one2, slot_f, 0.0), axis=0, keepdims=True)
        carry = carry + jnp.sum(chosen, axis=1, keepdims=True)
        carry_ref[...] = carry
        cnt_ref[...] = carry.astype(jnp.int32)
        row8 = lax.broadcasted_iota(jnp.int32, (N_EXPERTS, ts), 0)
        pos = jnp.where(row8 == 0, p1, jnp.where(row8 == 1, p2, 0.0)).astype(jnp.int32)
        pos_ref[...] = pos
        pos_vmem[slot] = pos
        row128 = lax.broadcasted_iota(jnp.int32, (LANES, ts), 0)
        wcol_ref[...] = jnp.where(row128 == 0, w1, jnp.where(row128 == 1, w2, 0.0)).T

        _to_token_tiles(tile_ref.at[slot], hn)
        if dispatch_prev:
            wait_rows(prev)
        positions_to_smem(slot).start()

    @pl.when(g == 0)
    def _():
        carry_ref[...] = jnp.zeros_like(carry_ref)
        tile_step(False)

    @pl.when(g > 0)
    def _():
        tile_step(True)

    @pl.when(g == ng - 1)
    def _():
        positions_to_smem(slot).wait()
        lax.fori_loop(0, ts, lambda t, c: (dispatch_row(t, slot), c)[1], 0, unroll=DMA_ISSUE_UNROLL)
        wait_rows(slot)
        zero_ref[...] = jnp.zeros_like(zero_ref)
        cc = pltpu.make_async_copy(cnt_ref, cnt_smem, psem)
        cc.start()
        cc.wait()
        tails = [pltpu.make_async_copy(
            zero_ref, xs_ref.at[pl.ds((e * cap + cnt_smem[e, 0]) * rpt, tm * rpt)], zsem)
            for e in range(N_EXPERTS)]
        for c in tails:
            c.start()
        for c in tails:
            c.wait()


def _gdn_out(o_f, o_b, z2, x2, o_norm, w_out, ffn_g, router_t, ts, tm, cap):
    t, d = x2.shape
    s = o_f.shape[2]
    spb = s // ts
    rpt = TOKEN_TILE_ROWS
    o_spec = pl.BlockSpec((1, N_HEADS, ts, HEAD_DIM), lambda g: (g // spb, 0, g % spb, 0))
    return pl.pallas_call(
        functools.partial(_gdn_out_body, cap=cap, tm=tm),
        grid=(t // ts,),
        in_specs=[o_spec, o_spec,
                  pl.BlockSpec((ts, d), lambda g: (g, 0)), pl.BlockSpec((ts, d), lambda g: (g, 0)),
                  _const_spec((1, HEAD_DIM)), _const_spec((d, d)), _const_spec((1, d)),
                  _const_spec((N_EXPERTS, d))],
        out_specs=[pl.BlockSpec((ts, d), lambda g: (g, 0)),
                   pl.BlockSpec((N_EXPERTS, ts), lambda g: (0, g)),
                   pl.BlockSpec((ts, LANES), lambda g: (g, 0)),
                   pl.BlockSpec((N_EXPERTS, LANES), lambda g: (0, 0)),
                   pl.BlockSpec(memory_space=pl.ANY)],
        out_shape=[jax.ShapeDtypeStruct((t, d), F32),
                   jax.ShapeDtypeStruct((N_EXPERTS, t), jnp.int32),
                   jax.ShapeDtypeStruct((t, LANES), F32),
                   jax.ShapeDtypeStruct((N_EXPERTS, LANES), jnp.int32),
                   jax.ShapeDtypeStruct((N_EXPERTS * cap * rpt, LANES), F32)],
        scratch_shapes=[pltpu.VMEM((ts, d), BF16),
                        pltpu.VMEM((2, ts * rpt, LANES), F32),
                        pltpu.VMEM((tm * rpt, LANES), F32),
                        pltpu.VMEM((N_EXPERTS, LANES), F32),
                        pltpu.VMEM((2, N_EXPERTS, ts), jnp.int32),
                        pltpu.SMEM((2, N_EXPERTS, ts), jnp.int32),
                        pltpu.SMEM((N_EXPERTS, LANES), jnp.int32),
                        pltpu.SemaphoreType.DMA((2,)), pltpu.SemaphoreType.DMA, pltpu.SemaphoreType.DMA],
        compiler_params=_params("arbitrary"),
        name="gdn_out_router",
    )(o_f, o_b, z2, x2, o_norm, w_out, ffn_g, router_t)


def _moe_body(te_ref, tb_ref, nu_ref, xs_ref, wg_ref, wu_ref, wd_ref, ys_ref, xb_ref, acc_ref, *, fc):
    i = pl.program_id(0)
    f = pl.program_id(1)
    tm = xb_ref.shape[0]
    tf = wg_ref.shape[2]

    @pl.when((i == 0) & (f == 0))
    def _():
        acc_ref[...] = jnp.zeros_like(acc_ref)

    @pl.when(i < nu_ref[0])
    def _():
        @pl.when(f == 0)
        def _():
            xb_ref[...] = _from_token_tiles(xs_ref, tm).astype(BF16)

        x = xb_ref[...]
        for c in range(tf // fc):
            cs = slice(c * fc, (c + 1) * fc)
            a = _silu(_mm(x, wg_ref[0, :, cs])) * _mm(x, wu_ref[0, :, cs])
            y = _mm(a, wd_ref[0, cs, :])
            if c == 0:
                acc_ref[...] = jnp.where(f > 0, acc_ref[...], 0.0) + y
            else:
                acc_ref[...] += y

        @pl.when(f == pl.num_programs(1) - 1)
        def _():
            _to_token_tiles(ys_ref, acc_ref[...])


def _moe_grouped(xs, tile_expert, tile_block, n_used, wg, wu, wd, tm, tf, fc):
    ne, d, dff = wg.shape
    nt = tile_expert.shape[0]
    nf = dff // tf
    rpt = TOKEN_TILE_ROWS

    def fsel(i, f, nu):
        return jnp.where(i < nu[0], f, nf - 1)

    grid_spec = pltpu.PrefetchScalarGridSpec(
        num_scalar_prefetch=3,
        grid=(nt, nf),
        in_specs=[pl.BlockSpec((tm * rpt, LANES), lambda i, f, te, tb, nu: (tb[i], 0)),
                  pl.BlockSpec((1, d, tf), lambda i, f, te, tb, nu: (te[i], 0, fsel(i, f, nu))),
                  pl.BlockSpec((1, d, tf), lambda i, f, te, tb, nu: (te[i], 0, fsel(i, f, nu))),
                  pl.BlockSpec((1, tf, d), lambda i, f, te, tb, nu: (te[i], fsel(i, f, nu), 0))],
        out_specs=pl.BlockSpec((tm * rpt, LANES), lambda i, f, te, tb, nu: (tb[i], 0)),
        scratch_shapes=[pltpu.VMEM((tm, d), BF16), pltpu.VMEM((tm, d), F32)],
    )
    return pl.pallas_call(
        functools.partial(_moe_body, fc=fc),
        grid_spec=grid_spec,
        out_shape=jax.ShapeDtypeStruct(xs.shape, F32),
        compiler_params=_params("arbitrary", "arbitrary"),
        name="moe_grouped",
    )(tile_expert, tile_block, n_used, xs, wg, wu, wd)


COMBINE_DEPTH = 3


def _combine_body(pos0_ref, pos1_ref, pos2_ref, x_ref, wcol_ref, fn_ref, ys_ref, o_ref, gbuf, pos_smem, sem, psem):
    i = pl.program_id(0)
    n = pl.num_programs(0)
    tc = x_ref.shape[0]
    rpt = TOKEN_TILE_ROWS

    def load_positions(p_ref):
        cp = pltpu.make_async_copy(p_ref, pos_smem, psem)
        cp.start()
        cp.wait()

    def issue(t, sl):
        for k in range(2):
            src = ys_ref.at[pl.ds(pos_smem[k, t] * rpt, rpt)]
            pltpu.make_async_copy(src, gbuf.at[sl, k, pl.ds(t * rpt, rpt)], sem.at[sl]).start()

    def wait_rows(sl):
        for k in range(2):
            pltpu.make_async_copy(ys_ref.at[pl.ds(0, tc * rpt)], gbuf.at[sl, k], sem.at[sl]).wait()

    @pl.when(i == 0)
    def _():
        for sl, p_ref in enumerate((pos0_ref, pos1_ref)):
            load_positions(p_ref)
            lax.fori_loop(0, tc, lambda t, c, sl=sl: (issue(t, sl), c)[1], 0, unroll=DMA_ISSUE_UNROLL)

    for sl in range(COMBINE_DEPTH):
        @pl.when(i % COMBINE_DEPTH == sl)
        def _(sl=sl):
            load_positions(pos2_ref)
            wait_rows(sl)
            for t in range(tc):
                issue(t, (sl + 2) % COMBINE_DEPTH)
            wcol = wcol_ref[...]
            y = (x_ref[...] + wcol[:, 0:1] * _from_token_tiles(gbuf.at[sl, 0], tc)
                 + wcol[:, 1:2] * _from_token_tiles(gbuf.at[sl, 1], tc))
            o_ref[...] = _rms(y, fn_ref[...])

            @pl.when(i == n - 1)
            def _():
                wait_rows((sl + 1) % COMBINE_DEPTH)
                wait_rows((sl + 2) % COMBINE_DEPTH)


def _moe_combine(pos, x2, wcol, final_g, ys, tc):
    t, d = x2.shape
    n = t // tc
    rpt = TOKEN_TILE_ROWS

    def pos_spec(ahead):
        return pl.BlockSpec((N_EXPERTS, tc), lambda i: (0, jnp.minimum(i + ahead, n - 1)))

    return pl.pallas_call(
        _combine_body,
        grid=(n,),
        in_specs=[pos_spec(0), pos_spec(1), pos_spec(2),
                  pl.BlockSpec((tc, d), lambda i: (i, 0)),
                  pl.BlockSpec((tc, LANES), lambda i: (i, 0)),
                  _const_spec((1, d)),
                  pl.BlockSpec(memory_space=pl.ANY)],
        out_specs=pl.BlockSpec((tc, d), lambda i: (i, 0)),
        out_shape=jax.ShapeDtypeStruct((t, d), F32),
        scratch_shapes=[pltpu.VMEM((COMBINE_DEPTH, 2, tc * rpt, LANES), F32),
                        pltpu.SMEM((N_EXPERTS, tc), jnp.int32),
                        pltpu.SemaphoreType.DMA((COMBINE_DEPTH,)), pltpu.SemaphoreType.DMA],
        compiler_params=_params("arbitrary"),
        name="moe_combine_norm",
    )(pos, pos, pos, x2, wcol, final_g, ys)


def _tile_schedule(counts, tm, cap, nt):
    ntile = (counts + tm - 1) // tm
    ends = jnp.cumsum(ntile)
    n_used = ends[-1]
    i = jnp.minimum(jnp.arange(nt, dtype=jnp.int32), jnp.maximum(n_used - 1, 0))
    te = jnp.sum((i[:, None] >= ends[None, :]).astype(jnp.int32), axis=1)
    tb = te * (cap // tm) + i - (ends - ntile)[te]
    return te.astype(jnp.int32), tb.astype(jnp.int32), n_used.reshape(1).astype(jnp.int32)


def _row(v):
    return v.reshape(1, -1).astype(F32)


def _pad_cols(w, n):
    return jnp.pad(w, ((0, 0), (0, n - w.shape[1])))


def _conformer_layer(x3, mix_g, ffn_g, pw1_w, pw1_b, dw_w, dw_b, ln_g, ln_b, pw2_w, pw2_b,
                     w_gate, w_up, w_down):
    bsz, s, d = x3.shape
    t = bsz * s
    tm = min(512, t)
    ts = min(512, s)
    x3 = _conf_mixer(x3, _row(mix_g), pw1_w.astype(BF16), _row(pw1_b), dw_w, _row(dw_b), _row(ln_g),
                     _row(ln_b), pw2_w.astype(BF16), _row(pw2_b), ts)
    x2 = _ffn(x3.reshape(t, d), _row(ffn_g), w_gate.astype(BF16), w_up.astype(BF16),
              w_down.astype(BF16), tm, 256)
    return x2.reshape(bsz, s, d)


def _deltanet_moe_layer(x3, mix_g, ffn_g, w_in, conv_w, a_log, dt_bias, o_norm, w_out,
                        router, e_gate, e_up, e_down, final_g):
    bsz, s, d = x3.shape
    t = bsz * s
    tm = min(512, t)
    ts = min(512, s)
    nmain = 4 * d
    zero16 = jnp.zeros((2 * N_HEADS,), F32)
    alog_row = _row(_pad_cols(jnp.concatenate([zero16, a_log.reshape(-1)])[None], LANES))
    dtb_row = _row(_pad_cols(jnp.concatenate([zero16, dt_bias.reshape(-1)])[None], LANES))
    z, w, kq, kd, p, egl = _gdn_front(x3, _row(mix_g), w_in.astype(BF16),
                                      _pad_cols(w_in[:, nmain:], LANES).astype(BF16), conv_w,
                                      alog_row, dtb_row, min(256, s))
    z = z.reshape(t, d)
    nbh = bsz * N_HEADS
    nck = s // CHUNK
    o_f, o_b = _gdn_scan(w.reshape(2, nbh, s, HEAD_DIM), kq.reshape(2, nbh, nck, 2 * CHUNK, HEAD_DIM),
                         kd.reshape(2, nbh, s, HEAD_DIM), p.reshape(2, nbh, s, HEAD_DIM),
                         egl.reshape(2, nbh, nck, 1, HEAD_DIM))
    cap = t + tm
    x2, pos, wcol, cnt, xs = _gdn_out(o_f.reshape(bsz, N_HEADS, s, HEAD_DIM), o_b.reshape(bsz, N_HEADS, s, HEAD_DIM),
                                      z, x3.reshape(t, d), _row(o_norm), w_out.astype(BF16), _row(ffn_g),
                                      router.T, ts, tm, cap)
    te, tb, n_used = _tile_schedule(cnt[:, 0], tm, cap, 2 * t // tm + N_EXPERTS)
    dffe = e_gate.shape[2]
    tf = dffe // 2 if (dffe // 2) % 256 == 0 else dffe
    ys = _moe_grouped(xs, te, tb, n_used, e_gate.astype(BF16), e_up.astype(BF16), e_down.astype(BF16),
                      tm, tf, 256)
    out = _moe_combine(pos, x2, wcol, _row(final_g), ys, ts)
    return out.reshape(bsz, s, d)


def kernel(x, mix_norm, ffn_norm, cf_pw1_w, cf_pw1_b, cf_dw_w, cf_dw_b, cf_ln_g, cf_ln_b, cf_pw2_w, cf_pw2_b, ffn_w_gate, ffn_w_up, ffn_w_down, gdn_w_in, gdn_conv_w, gdn_a_log, gdn_dt_bias, gdn_o_norm, gdn_w_out, moe_router, moe_w_gate, moe_w_up, moe_w_down, final_norm):
    x = _conformer_layer(x, mix_norm[0], ffn_norm[0], cf_pw1_w[0], cf_pw1_b[0], cf_dw_w[0], cf_dw_b[0],
                         cf_ln_g[0], cf_ln_b[0], cf_pw2_w[0], cf_pw2_b[0],
                         ffn_w_gate[0], ffn_w_up[0], ffn_w_down[0])
    return _deltanet_moe_layer(x, mix_norm[1], ffn_norm[1], gdn_w_in[0], gdn_conv_w[0], gdn_a_log[0],
                               gdn_dt_bias[0], gdn_o_norm[0], gdn_w_out[0], moe_router[0],
                               moe_w_gate[0], moe_w_up[0], moe_w_down[0], final_norm)
```

```python
import functools

import jax
import jax.numpy as jnp
from jax import lax
from jax.experimental import pallas as pl
from jax.experimental.pallas import tpu as pltpu

F32 = jnp.float32
BF16 = jnp.bfloat16

RMS_EPS = 1e-6
LN_EPS = 1e-5
L2_EPS = 1e-6
N_HEADS = 8
HEAD_DIM = 128
CHUNK = 64
N_EXPERTS = 8
LANES = 128
BF16_SUBLANES = 16
CONV_HALO = 16
SHORT_HALO = 8
VMEM_LIMIT_BYTES = 56 * 1024 * 1024


def _params(*sem):
    return pltpu.CompilerParams(dimension_semantics=sem, vmem_limit_bytes=VMEM_LIMIT_BYTES)


def _const_spec(shape):
    nd = len(shape)
    return pl.BlockSpec(shape, lambda *_: (0,) * nd, pipeline_mode=pl.Buffered(1))


def _cast_streams(arrays, grid):
    nsteps = grid[0] * grid[1]
    in_specs, out_specs, out_shapes = [], [], []
    for a in arrays:
        rows, cols = a.shape
        blk = next(r for r in range(BF16_SUBLANES, rows + 1, BF16_SUBLANES)
                   if rows % r == 0 and rows // r <= nsteps)
        nblk = rows // blk

        def imap(b, i, nblk=nblk):
            return (jnp.minimum(b * grid[1] + i, nblk - 1), 0)

        in_specs.append(pl.BlockSpec((blk, cols), imap))
        out_specs.append(pl.BlockSpec((blk, cols), imap))
        out_shapes.append(jax.ShapeDtypeStruct((rows, cols), BF16))
    return in_specs, out_specs, out_shapes


def _with_casts(body, n_in, n_out, n_cast):
    def wrapped(*refs):
        a, b, c = n_in + n_cast, n_in + n_cast + n_out, n_in + 2 * n_cast + n_out
        for src, dst in zip(refs[n_in:a], refs[b:c]):
            dst[...] = src[...].astype(BF16)
        body(*refs[:n_in], *refs[a:b], *refs[c:])
    return wrapped


def _rms(x, g):
    return x * lax.rsqrt(jnp.mean(x * x, axis=-1, keepdims=True) + RMS_EPS) * g


def _silu(x):
    return x * jax.nn.sigmoid(x)


def _mm(a, b):
    return jnp.dot(a.astype(BF16), b.astype(BF16), preferred_element_type=F32)


def _mm_nt(a, b):
    return lax.dot_general(a.astype(BF16), b.astype(BF16), (((1,), (1,)), ((), ())),
                           preferred_element_type=F32)


def _mm_tn(a, b):
    return lax.dot_general(a.astype(BF16), b.astype(BF16), (((0,), (0,)), ((), ())),
                           preferred_element_type=F32)


def _mm_exact(a, b):
    return jnp.dot(a, b, preferred_element_type=F32, precision=lax.Precision.HIGHEST)


def _conf_mixer_body(x_ref, xp_ref, xn_ref, g_ref, w1_ref, b1_ref, dw_ref, dwb_ref, lng_ref, lnb_ref,
                     w2_ref, b2_ref, o_ref, xe_ref, ext_ref, cv_ref, *, nc):
    s = pl.program_id(1)
    ns = pl.num_programs(1)
    ts, d = cv_ref.shape
    width = dw_ref.shape[0]
    base = CONV_HALO - width // 2
    span = ts + 8 * ((base + width - 1) // 8)
    xe_ref[0:CONV_HALO, :] = xp_ref[0]
    xe_ref[CONV_HALO:CONV_HALO + ts, :] = x_ref[0]
    xe_ref[CONV_HALO + ts:, :] = xn_ref[0]
    h = _rms(xe_ref[...], g_ref[...]).astype(BF16)
    for c in range(d // nc):
        a = _mm(h, w1_ref[:, c * nc:(c + 1) * nc]) + b1_ref[:, c * nc:(c + 1) * nc]
        b = _mm(h, w1_ref[:, d + c * nc:d + (c + 1) * nc]) + b1_ref[:, d + c * nc:d + (c + 1) * nc]
        ext_ref[0, :, c * nc:(c + 1) * nc] = a * jax.nn.sigmoid(b)

    @pl.when(s == 0)
    def _():
        ext_ref[0, 0:CONV_HALO, :] = jnp.zeros((CONV_HALO, d), F32)

    @pl.when(s == ns - 1)
    def _():
        ext_ref[0, CONV_HALO + ts:, :] = jnp.zeros((CONV_HALO, d), F32)

    for p in range(1, 8):
        ext_ref[p, 0:span, :] = ext_ref[0, pl.ds(p, span), :]
    rb = 128
    for c in range(d // LANES):
        cs = slice(c * LANES, (c + 1) * LANES)
        for r in range(ts // rb):
            acc = jnp.zeros((rb, LANES), F32)
            for k in range(width):
                off = base + k
                acc = acc + dw_ref[k:k + 1, cs] * ext_ref[off % 8, pl.ds(r * rb + 8 * (off // 8), rb), cs]
            cv_ref[r * rb:(r + 1) * rb, cs] = acc + dwb_ref[:, cs]
    y = cv_ref[...]
    mu = jnp.mean(y, axis=-1, keepdims=True)
    yc = y - mu
    yn = yc * lax.rsqrt(jnp.mean(yc * yc, axis=-1, keepdims=True) + LN_EPS) * lng_ref[...] + lnb_ref[...]
    o_ref[0] = x_ref[0] + _mm(_silu(yn), w2_ref[...]) + b2_ref[...]


def _conf_mixer(x3, g, w1, b1, dw_w, dw_b, ln_g, ln_b, w2, b2, ts, casts):
    bsz, s, d = x3.shape
    width = dw_w.shape[0]
    hb = ts // CONV_HALO
    nhb = s // CONV_HALO
    grid = (bsz, s // ts)
    c_in, c_out, c_shape = _cast_streams(casts, grid)
    res = pl.pallas_call(
        _with_casts(functools.partial(_conf_mixer_body, nc=256), 12, 1, len(casts)),
        grid=grid,
        in_specs=[pl.BlockSpec((1, ts, d), lambda b, i: (b, i, 0)),
                  pl.BlockSpec((1, CONV_HALO, d), lambda b, i: (b, jnp.maximum(i * hb - 1, 0), 0)),
                  pl.BlockSpec((1, CONV_HALO, d), lambda b, i: (b, jnp.minimum((i + 1) * hb, nhb - 1), 0)),
                  _const_spec((1, d)), _const_spec((d, 2 * d)), _const_spec((1, 2 * d)),
                  _const_spec((width, d)), _const_spec((1, d)), _const_spec((1, d)), _const_spec((1, d)),
                  _const_spec((d, d)), _const_spec((1, d))] + c_in,
        out_specs=[pl.BlockSpec((1, ts, d), lambda b, i: (b, i, 0))] + c_out,
        out_shape=[jax.ShapeDtypeStruct((bsz, s, d), F32)] + c_shape,
        scratch_shapes=[pltpu.VMEM((ts + 2 * CONV_HALO, d), F32),
                        pltpu.VMEM((8, ts + 2 * CONV_HALO, d), F32), pltpu.VMEM((ts, d), F32)],
        compiler_params=_params("parallel", "parallel"),
        name="conf_mixer",
    )(x3, x3, x3, g, w1, b1, dw_w, dw_b, ln_g, ln_b, w2, b2, *casts)
    return res[0], res[1:]


def _ffn_body(x_ref, g_ref, wg_ref, wu_ref, wd_ref, o_ref, *, fc):
    x = x_ref[...]
    h = _rms(x, g_ref[...]).astype(BF16)
    dff = wg_ref.shape[1]
    acc = x
    for c in range(dff // fc):
        cs = slice(c * fc, (c + 1) * fc)
        a = _silu(_mm(h, wg_ref[:, cs])) * _mm(h, wu_ref[:, cs])
        acc = acc + _mm(a, wd_ref[cs, :])
    o_ref[...] = acc


def _ffn(x2d, g, wg, wu, wd, tm, fc):
    t, d = x2d.shape
    dff = wg.shape[1]
    return pl.pallas_call(
        functools.partial(_ffn_body, fc=fc),
        grid=(t // tm,),
        in_specs=[pl.BlockSpec((tm, d), lambda i: (i, 0)), _const_spec((1, d)),
                  _const_spec((d, dff)), _const_spec((d, dff)), _const_spec((dff, d))],
        out_specs=pl.BlockSpec((tm, d), lambda i: (i, 0)),
        out_shape=jax.ShapeDtypeStruct((t, d), F32),
        compiler_params=_params("parallel"),
        name="dense_swiglu",
    )(x2d, g, wg, wu, wd)


PAIR = 2 * CHUNK


def _block_diag(yb, mk_ref):
    return jnp.concatenate([yb * mk_ref[MASK_LEFT].astype(BF16), yb * mk_ref[MASK_RIGHT].astype(BF16)], axis=0)


def _tri_inverse_pairs(mats, mk_ref):
    def bd(vals):
        return [_block_diag(v.astype(BF16), mk_ref) for v in vals]

    l0 = [a * mk_ref[MASK_BLOCK8] for a in mats]
    l0b = [a.astype(BF16) for a in l0]
    l2 = [_mm(a, d) for a, d in zip(l0b, bd(l0b))]
    l2b = [a.astype(BF16) for a in l2]
    l2d = bd(l2b)
    l4 = [_mm(a, d) for a, d in zip(l2b, l2d)]
    l3 = [_mm(a, d) for a, d in zip(l0b, l2d)]
    xs = [mk_ref[MASK_EYE] - a + b - t for a, b, t in zip(l0, l2, l3)]
    x4 = [_mm(x, d) for x, d in zip(xs, bd(l4))]
    xs = [x + t for x, t in zip(xs, x4)]
    for level in range(MASK_MERGE0, MASK_MERGE0 + 3):
        xb = [x.astype(BF16) for x in xs]
        t1 = [_mm(a * mk_ref[level], d) for a, d in zip(mats, bd(xb))]
        t2 = [_mm(x, d) for x, d in zip(xb, bd(t1))]
        xs = [x - t for x, t in zip(xs, t2)]
    return xs


(MASK_INCL, MASK_INCL_L, MASK_INCL_R, MASK_STRICT_L, MASK_STRICT_R) = (0, 2, 4, 6, 8)
MASK_EYE, MASK_BLOCK8, MASK_MERGE0, MASK_LEFT, MASK_RIGHT = 10, 11, 12, 15, 16
N_MASKS = 17


def _chunk_masks():
    ri = lax.broadcasted_iota(jnp.int32, (CHUNK, PAIR), 0)
    li = lax.broadcasted_iota(jnp.int32, (CHUNK, PAIR), 1)
    ci = li % CHUNK
    left, right = li < CHUNK, li >= CHUNK
    incl = [ci <= ri, ci >= ri]
    strict = [ci < ri, ci > ri]
    masks = (incl + [m & left for m in incl] + [m & right for m in incl]
             + [m & left for m in strict] + [m & right for m in strict]
             + [ci == ri, (ri // 8) == (ci // 8)])
    size = 8
    while size < CHUNK:
        masks.append(((ri // (2 * size)) == (ci // (2 * size))) & ((ri // size) != (ci // size)))
        size *= 2
    masks += [left, right]
    assert len(masks) == N_MASKS
    return jnp.stack(masks).astype(F32)


GDN_HEAD_GROUP = 4


def _gdn_front_body(x_ref, xp_ref, xn_ref, g_ref, w_ref, wgate_ref, cw_ref, alog_ref, dtb_ref, mk_ref,
                    z_ref, w_out_ref, kq_ref, kd_ref, p_ref, egl_ref, xe_ref, pe_ref):
    s = pl.program_id(1)
    ns = pl.num_programs(1)
    tg, d = z_ref.shape[1:]
    kw = cw_ref.shape[0]
    dk = N_HEADS * HEAD_DIM
    npair = tg // PAIR
    gw = GDN_HEAD_GROUP * HEAD_DIM
    ngroup = N_HEADS // GDN_HEAD_GROUP
    base = SHORT_HALO - kw // 2

    xe_ref[0:SHORT_HALO, :] = jnp.where(s > 0, xp_ref[0], 0.0)
    xe_ref[SHORT_HALO:SHORT_HALO + tg, :] = x_ref[0]
    xe_ref[SHORT_HALO + tg:, :] = jnp.where(s < ns - 1, xn_ref[0], 0.0)
    h = _rms(xe_ref[...], g_ref[...]).astype(BF16)
    hm = h[SHORT_HALO:SHORT_HALO + tg]

    gates = _mm(hm, wgate_ref[...])
    beta = jax.nn.sigmoid(gates)
    xa = gates + dtb_ref[...]
    softplus = jnp.maximum(xa, 0.0) + jnp.log1p(jnp.exp(-jnp.abs(xa)))
    log_a = -jnp.exp(alog_ref[...]) * softplus
    ri = lax.broadcasted_iota(jnp.int32, (tg, tg), 0)
    ci = lax.broadcasted_iota(jnp.int32, (tg, tg), 1)
    same = (ri // CHUNK) == (ci // CHUNK)
    cum_f = _mm_exact((same & (ci <= ri)).astype(F32), log_a)
    cum_b = _mm_exact((same & (ci >= ri)).astype(F32), log_a)
    lane = lax.broadcasted_iota(jnp.int32, (tg, LANES), 1)
    g_all = jnp.where(lane < 3 * N_HEADS, cum_f, cum_b)
    g_all_t = g_all.T
    beta_t = beta.T

    def project(grp):
        out = []
        for part in range(3):
            col0 = part * dk + grp * gw
            pe_ref[grp % 2, part] = _mm(h, w_ref[:, col0:col0 + gw])
            heads = []
            for c in range(GDN_HEAD_GROUP):
                col = col0 + c * LANES
                acc = jnp.zeros((tg, LANES), F32)
                for k in range(kw):
                    acc = acc + (cw_ref[k:k + 1, col:col + LANES]
                                 * pe_ref[grp % 2, part, pl.ds(base + k, tg), c * LANES:(c + 1) * LANES])
                a = _silu(acc)
                if part < 2:
                    a = a * lax.rsqrt(jnp.sum(a * a, axis=-1, keepdims=True) + L2_EPS)
                if part == 0:
                    a = a * (HEAD_DIM ** -0.5)
                heads.append(a)
            out.append(heads)
        return out

    def chunk_systems(grp, qkv):
        heads = range(grp * GDN_HEAD_GROUP, (grp + 1) * GDN_HEAD_GROUP)
        qs, ks, vs = (dict(zip(heads, part)) for part in qkv)
        raw, v16 = {}, {}
        for hd in heads:
            k16 = ks[hd].astype(BF16)
            q16 = qs[hd].astype(BF16)
            v16[hd] = vs[hd].astype(BF16)
            for m in range(npair):
                pr = slice(m * PAIR, (m + 1) * PAIR)
                raw[hd, m] = _mm_nt(jnp.concatenate([k16[pr], q16[pr]], axis=0), k16[pr])
        insts = [(hd, dr, m) for hd in heads for dr in range(2) for m in range(npair)]
        zero = jnp.zeros((CHUNK, HEAD_DIM), BF16)
        lows, rhss, beta_rows = [], [], []
        for hd, dr, m in insts:
            pr = slice(m * PAIR, (m + 1) * PAIR)
            bcol = dr * N_HEADS + hd
            gcol = 2 * N_HEADS + dr * N_HEADS + hd
            bt = beta[pr, bcol:bcol + 1]
            beta_rows.append(beta_t[bcol:bcol + 1, pr])
            g_c = g_all[pr, gcol:gcol + 1]
            g_r = g_all_t[gcol:gcol + 1, pr]
            diff = g_c - g_r
            decay = jnp.exp(diff[:CHUNK] * mk_ref[MASK_INCL_L + dr] + diff[CHUNK:] * mk_ref[MASK_INCL_R + dr])
            decay = decay * mk_ref[MASK_INCL + dr]
            kk = raw[hd, m][:PAIR] * bt
            qk = raw[hd, m][PAIR:]
            lows.append((kk[:CHUNK] * mk_ref[MASK_STRICT_L + dr] + kk[CHUNK:] * mk_ref[MASK_STRICT_R + dr]) * decay)
            pq = (qk[:CHUNK] * mk_ref[MASK_LEFT] + qk[CHUNK:] * mk_ref[MASK_RIGHT]) * decay
            half = mk_ref[MASK_LEFT + dr]
            p_tiles = [(t * half).astype(BF16) for t in (pq, pltpu.roll(pq, CHUNK, axis=1))]
            eg = jnp.exp(g_c)
            keg = (ks[hd][pr] * eg).astype(BF16)
            qeg = (qs[hd][pr] * eg).astype(BF16)
            vp = v16[hd][pr]
            rhss.append(jnp.concatenate(
                [jnp.concatenate([vp[:CHUNK], keg[:CHUNK], zero, zero], axis=1),
                 jnp.concatenate([zero, zero, vp[CHUNK:], keg[CHUNK:]], axis=1)], axis=0))
            for j in range(2):
                n = 2 * m + j
                rs = slice(n * CHUNK, (n + 1) * CHUNK)
                js = slice(j * CHUNK, (j + 1) * CHUNK)
                last = n * CHUNK + (CHUNK - 1 if dr == 0 else 0)
                g_l = g_all[last:last + 1, gcol:gcol + 1]
                p_ref[dr, 0, hd, rs, :] = p_tiles[(j + dr) % 2]
                kq_ref[dr, 0, hd, n, CHUNK:, :] = qeg[js]
                kd_ref[dr, 0, hd, rs, :] = (ks[hd][rs] * jnp.exp(g_l - g_all[rs, gcol:gcol + 1])).astype(BF16)
                egl_ref[dr, 0, hd, n] = jnp.broadcast_to(jnp.exp(g_l), (1, HEAD_DIM))
        invs = _tri_inverse_pairs(lows, mk_ref)
        sols = [_mm(inv * br, rhs) for inv, br, rhs in zip(invs, beta_rows, rhss)]
        for (hd, dr, m), sol in zip(insts, sols):
            for j in range(2):
                n = 2 * m + j
                rs = slice(n * CHUNK, (n + 1) * CHUNK)
                w_out_ref[dr, 0, hd, rs, :] = sol[:, 2 * j * HEAD_DIM:(2 * j + 1) * HEAD_DIM]
                kq_ref[dr, 0, hd, n, :CHUNK, :] = sol[:, (2 * j + 1) * HEAD_DIM:(2 * j + 2) * HEAD_DIM].astype(BF16)

    qkv = project(0)
    for grp in range(ngroup):
        nxt = project(grp + 1) if grp + 1 < ngroup else None
        if nxt is None:
            z_ref[0] = _mm(hm, w_ref[:, 3 * dk:3 * dk + d])
        chunk_systems(grp, qkv)
        qkv = nxt


def _gdn_front(x3, g, w_in, w_gate, conv_w, alog_row, dtb_row, tg, casts):
    bsz, s, d = x3.shape
    nin = w_in.shape[1]
    kw, nq = conv_w.shape
    nchunk = tg // CHUNK
    hb = tg // SHORT_HALO
    nhb = s // SHORT_HALO
    gw = GDN_HEAD_GROUP * HEAD_DIM

    def rows(dtype, last=HEAD_DIM):
        return (jax.ShapeDtypeStruct((2, bsz, N_HEADS, s, last), dtype),
                pl.BlockSpec((2, 1, N_HEADS, tg, last), lambda b, i: (0, b, 0, i, 0)))

    def per_chunk(dtype, r):
        return (jax.ShapeDtypeStruct((2, bsz, N_HEADS, s // CHUNK, r, HEAD_DIM), dtype),
                pl.BlockSpec((2, 1, N_HEADS, nchunk, r, HEAD_DIM), lambda b, i: (0, b, 0, i, 0, 0)))

    outs = [(jax.ShapeDtypeStruct((bsz, s, d), F32), pl.BlockSpec((1, tg, d), lambda b, i: (b, i, 0))),
            rows(F32), per_chunk(BF16, 2 * CHUNK), rows(BF16), rows(BF16), per_chunk(F32, 1)]
    grid = (bsz, s // tg)
    c_in, c_out, c_shape = _cast_streams(casts, grid)
    res = pl.pallas_call(
        _with_casts(_gdn_front_body, 10, len(outs), len(casts)),
        grid=grid,
        in_specs=[pl.BlockSpec((1, tg, d), lambda b, i: (b, i, 0)),
                  pl.BlockSpec((1, SHORT_HALO, d), lambda b, i: (b, jnp.maximum(i * hb - 1, 0), 0)),
                  pl.BlockSpec((1, SHORT_HALO, d), lambda b, i: (b, jnp.minimum((i + 1) * hb, nhb - 1), 0)),
                  _const_spec((1, d)), _const_spec((d, nin)), _const_spec((d, LANES)),
                  _const_spec((kw, nq)), _const_spec((1, LANES)), _const_spec((1, LANES)),
                  _const_spec((N_MASKS, CHUNK, PAIR))] + c_in,
        out_specs=[o[1] for o in outs] + c_out,
        out_shape=[o[0] for o in outs] + c_shape,
        scratch_shapes=[pltpu.VMEM((tg + 2 * SHORT_HALO, d), F32),
                        pltpu.VMEM((2, 3, tg + 2 * SHORT_HALO, gw), F32)],
        compiler_params=_params("parallel", "parallel"),
        name="gdn_front",
    )(x3, x3, x3, g, w_in, w_gate, conv_w, alog_row, dtb_row, _chunk_masks(), *casts)
    return res[:len(outs)], res[len(outs):]


GDN_SCAN_GROUP = 8


def _gdn_scan_body(wf, kqf, kdf, pf, eglf, wb, kqb, kdb, pb, eglb, of_ref, ob_ref, state_ref):
    n = pl.program_id(0)

    @pl.when(n == 0)
    def _():
        state_ref[...] = jnp.zeros_like(state_ref)

    nbh = state_ref.shape[1]
    dirs = ((wf, kqf, kdf, pf, eglf, of_ref), (wb, kqb, kdb, pb, eglb, ob_ref))

    def body(i, carry):
        bhs = [i * GDN_SCAN_GROUP + j for j in range(GDN_SCAN_GROUP)]
        zero = jnp.zeros((CHUNK, HEAD_DIM), BF16)
        st = [[state_ref[dr, b] for dr in range(2)] for b in bhs]
        r = [[_mm(dirs[dr][1][0, b, 0], s_[dr]) for dr in range(2)] for b, s_ in zip(bhs, st)]
        ub = [[(dirs[dr][0][0, b] - r_[dr][:CHUNK]).astype(BF16) for dr in range(2)] for b, r_ in zip(bhs, r)]
        ud = [jnp.concatenate([jnp.concatenate([u_[0], zero], axis=1), jnp.concatenate([zero, u_[1]], axis=1)],
                              axis=0) for u_ in ub]
        pu = [_mm(dirs[0][3][0, b] + dirs[1][3][0, b], d_) for b, d_ in zip(bhs, ud)]
        ku = [_mm_tn(jnp.concatenate([dirs[0][2][0, b], dirs[1][2][0, b]], axis=0), d_) for b, d_ in zip(bhs, ud)]
        for b, r_, pu_, ku_, s_ in zip(bhs, r, pu, ku, st):
            for dr in range(2):
                cs = slice(dr * HEAD_DIM, (dr + 1) * HEAD_DIM)
                dirs[dr][5][b] = r_[dr][CHUNK:] + pu_[:, cs]
                state_ref[dr, b] = s_[dr] * dirs[dr][4][0, b, 0] + ku_[:, cs]
        return carry

    lax.fori_loop(0, nbh // GDN_SCAN_GROUP, body, 0)


def _gdn_scan(w, kq, kd, p, egl):
    _, nbh, s, dh = w.shape
    nchunk = s // CHUNK

    def specs(dr):
        def im(n):
            return (dr, 0, n if dr == 0 else nchunk - 1 - n, 0)

        def im5(n):
            return im(n) + (0,)
        return [pl.BlockSpec((1, nbh, CHUNK, dh), im), pl.BlockSpec((1, nbh, 1, 2 * CHUNK, dh), im5),
                pl.BlockSpec((1, nbh, CHUNK, dh), im), pl.BlockSpec((1, nbh, CHUNK, dh), im),
                pl.BlockSpec((1, nbh, 1, 1, dh), im5)]

    o_shape = jax.ShapeDtypeStruct((nbh, s, dh), F32)
    return pl.pallas_call(
        _gdn_scan_body,
        grid=(nchunk,),
        in_specs=specs(0) + specs(1),
        out_specs=[pl.BlockSpec((nbh, CHUNK, dh), lambda n: (0, n, 0)),
                   pl.BlockSpec((nbh, CHUNK, dh), lambda n: (0, nchunk - 1 - n, 0))],
        out_shape=[o_shape, o_shape],
        scratch_shapes=[pltpu.VMEM((2, nbh, dh, dh), F32)],
        compiler_params=_params("arbitrary"),
        name="gdn_scan",
    )(w, kq, kd, p, egl, w, kq, kd, p, egl)


TOKEN_TILE_ROWS = 8
DMA_ISSUE_UNROLL = 8


def _to_token_tiles(dst_ref, val):
    n = val.shape[0]
    for j in range(TOKEN_TILE_ROWS):
        dst_ref[pl.ds(j, n, stride=TOKEN_TILE_ROWS), :] = val[:, j * LANES:(j + 1) * LANES]


def _from_token_tiles(src_ref, n):
    return jnp.concatenate([src_ref[pl.ds(j, n, stride=TOKEN_TILE_ROWS), :] for j in range(TOKEN_TILE_ROWS)],
                           axis=1)


def _gdn_out_body(of_ref, ob_ref, z_ref, x_ref, on_ref, wo_ref, fg_ref, rwt_ref,
                  xo_ref, pos_ref, wcol_ref, cnt_ref, xs_ref,
                  act_ref, tile_ref, zero_ref, carry_ref, pos_vmem, pos_smem, cnt_smem, sem, psem, zsem,
                  *, cap, tm):
    g = pl.program_id(0)
    ng = pl.num_programs(0)
    ts = x_ref.shape[0]
    slot = g % 2
    prev = 1 - slot
    rpt = TOKEN_TILE_ROWS

    def wait_rows(s):
        for _ in range(2):
            pltpu.make_async_copy(tile_ref.at[s], xs_ref.at[pl.ds(0, ts * rpt)], sem.at[s]).wait()

    def dispatch_row(t, s):
        src = tile_ref.at[s, pl.ds(t * rpt, rpt)]
        for k in range(2):
            dst = xs_ref.at[pl.ds(pos_smem[s, k, t] * rpt, rpt)]
            pltpu.make_async_copy(src, dst, sem.at[s]).start()

    def positions_to_smem(s):
        return pltpu.make_async_copy(pos_vmem.at[s], pos_smem.at[s], psem)

    def tile_step(dispatch_prev):
        if dispatch_prev:
            positions_to_smem(prev).wait()
            for t in range(ts):
                dispatch_row(t, prev)
        for h in range(N_HEADS):
            hs = slice(h * HEAD_DIM, (h + 1) * HEAD_DIM)
            o = of_ref[0, h] + ob_ref[0, h]
            o = o * lax.rsqrt(jnp.mean(o * o, axis=-1, keepdims=True) + RMS_EPS) * on_ref[...]
            act_ref[:, hs] = (o * _silu(z_ref[:, hs])).astype(BF16)
        x = x_ref[...] + _mm(act_ref[...], wo_ref[...])
        xo_ref[...] = x
        hn = _rms(x, fg_ref[...])

        logits = lax.dot_general(rwt_ref[...], hn, (((1,), (1,)), ((), ())),
                                 preferred_element_type=F32, precision=lax.Precision.HIGHEST)
        eidx = lax.broadcasted_iota(jnp.int32, logits.shape, 0).astype(F32)
        neg = jnp.float32(-jnp.inf)
        m1 = jnp.max(logits, axis=0, keepdims=True)
        i1 = jnp.min(jnp.where(logits == m1, eidx, float(N_EXPERTS)), axis=0, keepdims=True)
        one1 = eidx == i1
        rest = jnp.where(one1, neg, logits)
        m2 = jnp.max(rest, axis=0, keepdims=True)
        i2 = jnp.min(jnp.where(rest == m2, eidx, float(N_EXPERTS)), axis=0, keepdims=True)
        one2 = eidx == i2
        e2 = jnp.exp(m2 - m1)
        w1 = 1.0 / (1.0 + e2)
        w2 = e2 * w1

        chosen = jnp.where(one1 | one2, 1.0, 0.0)
        ri = lax.broadcasted_iota(jnp.int32, (ts, ts), 0)
        ci = lax.broadcasted_iota(jnp.int32, (ts, ts), 1)
        before = jnp.where(ri < ci, 1.0, 0.0).astype(BF16)
        rank = jnp.dot(chosen.astype(BF16), before, preferred_element_type=F32)
        carry = carry_ref[...]
        slot_f = eidx * float(cap) + carry[:, 0:1] + rank
        p1 = jnp.sum(jnp.where(one1, slot_f, 0.0), axis=0, keepdims=True)
        p2 = jnp.sum(jnp.where(one2, slot_f, 0.0), axis=0, keepdims=True)
        carry = carry + jnp.sum(chosen, axis=1, keepdims=True)
        carry_ref[...] = carry
        cnt_ref[...] = carry.astype(jnp.int32)
        row8 = lax.broadcasted_iota(jnp.int32, (N_EXPERTS, ts), 0)
        pos = jnp.where(row8 == 0, p1, jnp.where(row8 == 1, p2, 0.0)).astype(jnp.int32)
        pos_ref[...] = pos
        pos_vmem[slot] = pos
        row128 = lax.broadcasted_iota(jnp.int32, (LANES, ts), 0)
        wcol_ref[...] = jnp.where(row128 == 0, w1, jnp.where(row128 == 1, w2, 0.0)).T

        _to_token_tiles(tile_ref.at[slot], hn)
        if dispatch_prev:
            wait_rows(prev)
        positions_to_smem(slot).start()

    @pl.when(g == 0)
    def _():
        carry_ref[...] = jnp.zeros_like(carry_ref)
        tile_step(False)

    @pl.when(g > 0)
    def _():
        tile_step(True)

    @pl.when(g == ng - 1)
    def _():
        positions_to_smem(slot).wait()
        lax.fori_loop(0, ts, lambda t, c: (dispatch_row(t, slot), c)[1], 0, unroll=DMA_ISSUE_UNROLL)
        wait_rows(slot)
        zero_ref[...] = jnp.zeros_like(zero_ref)
        cc = pltpu.make_async_copy(cnt_ref, cnt_smem, psem)
        cc.start()
        cc.wait()
        tails = [pltpu.make_async_copy(
            zero_ref, xs_ref.at[pl.ds((e * cap + cnt_smem[e, 0]) * rpt, tm * rpt)], zsem)
            for e in range(N_EXPERTS)]
        for c in tails:
            c.start()
        for c in tails:
            c.wait()


def _gdn_out(o_f, o_b, z2, x2, o_norm, w_out, ffn_g, router_t, ts, tm, cap):
    t, d = x2.shape
    s = o_f.shape[2]
    spb = s // ts
    rpt = TOKEN_TILE_ROWS
    o_spec = pl.BlockSpec((1, N_HEADS, ts, HEAD_DIM), lambda g: (g // spb, 0, g % spb, 0))
    return pl.pallas_call(
        functools.partial(_gdn_out_body, cap=cap, tm=tm),
        grid=(t // ts,),
        in_specs=[o_spec, o_spec,
                  pl.BlockSpec((ts, d), lambda g: (g, 0)), pl.BlockSpec((ts, d), lambda g: (g, 0)),
                  _const_spec((1, HEAD_DIM)), _const_spec((d, d)), _const_spec((1, d)),
                  _const_spec((N_EXPERTS, d))],
        out_specs=[pl.BlockSpec((ts, d), lambda g: (g, 0)),
                   pl.BlockSpec((N_EXPERTS, ts), lambda g: (0, g)),
                   pl.BlockSpec((ts, LANES), lambda g: (g, 0)),
                   pl.BlockSpec((N_EXPERTS, LANES), lambda g: (0, 0)),
                   pl.BlockSpec(memory_space=pl.ANY)],
        out_shape=[jax.ShapeDtypeStruct((t, d), F32),
                   jax.ShapeDtypeStruct((N_EXPERTS, t), jnp.int32),
                   jax.ShapeDtypeStruct((t, LANES), F32),
                   jax.ShapeDtypeStruct((N_EXPERTS, LANES), jnp.int32),
                   jax.ShapeDtypeStruct((N_EXPERTS * cap * rpt, LANES), F32)],
        scratch_shapes=[pltpu.VMEM((ts, d), BF16),
                        pltpu.VMEM((2, ts * rpt, LANES), F32),
                        pltpu.VMEM((tm * rpt, LANES), F32),
                        pltpu.VMEM((N_EXPERTS, LANES), F32),
                        pltpu.VMEM((2, N_EXPERTS, ts), jnp.int32),
                        pltpu.SMEM((2, N_EXPERTS, ts), jnp.int32),
                        pltpu.SMEM((N_EXPERTS, LANES), jnp.int32),
                        pltpu.SemaphoreType.DMA((2,)), pltpu.SemaphoreType.DMA, pltpu.SemaphoreType.DMA],
        compiler_params=_params("arbitrary"),
        name="gdn_out_router",
    )(o_f, o_b, z2, x2, o_norm, w_out, ffn_g, router_t)


def _moe_body(te_ref, tb_ref, nu_ref, xs_ref, wg_ref, wu_ref, wd_ref, ys_ref, xb_ref, acc_ref, *, fc):
    i = pl.program_id(0)
    f = pl.program_id(1)
    tm = xb_ref.shape[0]
    tf = wg_ref.shape[2]

    @pl.when((i == 0) & (f == 0))
    def _():
        acc_ref[...] = jnp.zeros_like(acc_ref)

    @pl.when(i < nu_ref[0])
    def _():
        @pl.when(f == 0)
        def _():
            xb_ref[...] = _from_token_tiles(xs_ref, tm).astype(BF16)

        x = xb_ref[...]
        for c in range(tf // fc):
            cs = slice(c * fc, (c + 1) * fc)
            a = _silu(_mm(x, wg_ref[0, :, cs])) * _mm(x, wu_ref[0, :, cs])
            y = _mm(a, wd_ref[0, cs, :])
            if c == 0:
                acc_ref[...] = jnp.where(f > 0, acc_ref[...], 0.0) + y
            else:
                acc_ref[...] += y

        @pl.when(f == pl.num_programs(1) - 1)
        def _():
            _to_token_tiles(ys_ref, acc_ref[...])


def _moe_grouped(xs, tile_expert, tile_block, n_used, wg, wu, wd, tm, tf, fc):
    ne, d, dff = wg.shape
    nt = tile_expert.shape[0]
    nf = dff // tf
    rpt = TOKEN_TILE_ROWS

    def fsel(i, f, nu):
        return jnp.where(i < nu[0], f, nf - 1)

    grid_spec = pltpu.PrefetchScalarGridSpec(
        num_scalar_prefetch=3,
        grid=(nt, nf),
        in_specs=[pl.BlockSpec((tm * rpt, LANES), lambda i, f, te, tb, nu: (tb[i], 0)),
                  pl.BlockSpec((1, d, tf), lambda i, f, te, tb, nu: (te[i], 0, fsel(i, f, nu))),
                  pl.BlockSpec((1, d, tf), lambda i, f, te, tb, nu: (te[i], 0, fsel(i, f, nu))),
                  pl.BlockSpec((1, tf, d), lambda i, f, te, tb, nu: (te[i], fsel(i, f, nu), 0))],
        out_specs=pl.BlockSpec((tm * rpt, LANES), lambda i, f, te, tb, nu: (tb[i], 0)),
        scratch_shapes=[pltpu.VMEM((tm, d), BF16), pltpu.VMEM((tm, d), F32)],
    )
    return pl.pallas_call(
        functools.partial(_moe_body, fc=fc),
        grid_spec=grid_spec,
        out_shape=jax.ShapeDtypeStruct(xs.shape, F32),
        compiler_params=_params("arbitrary", "arbitrary"),
        name="moe_grouped",
    )(tile_expert, tile_block, n_used, xs, wg, wu, wd)


COMBINE_DEPTH = 3


def _combine_body(pos0_ref, pos1_ref, pos2_ref, x_ref, wcol_ref, fn_ref, ys_ref, o_ref, gbuf, pos_smem, sem, psem):
    i = pl.program_id(0)
    n = pl.num_programs(0)
    tc = x_ref.shape[0]
    rpt = TOKEN_TILE_ROWS

    def load_positions(p_ref):
        cp = pltpu.make_async_copy(p_ref, pos_smem, psem)
        cp.start()
        cp.wait()

    def issue(t, sl):
        for k in range(2):
            src = ys_ref.at[pl.ds(pos_smem[k, t] * rpt, rpt)]
            pltpu.make_async_copy(src, gbuf.at[sl, k, pl.ds(t * rpt, rpt)], sem.at[sl]).start()

    def wait_rows(sl):
        for k in range(2):
            pltpu.make_async_copy(ys_ref.at[pl.ds(0, tc * rpt)], gbuf.at[sl, k], sem.at[sl]).wait()

    @pl.when(i == 0)
    def _():
        for sl, p_ref in enumerate((pos0_ref, pos1_ref)):
            load_positions(p_ref)
            lax.fori_loop(0, tc, lambda t, c, sl=sl: (issue(t, sl), c)[1], 0, unroll=DMA_ISSUE_UNROLL)

    for sl in range(COMBINE_DEPTH):
        @pl.when(i % COMBINE_DEPTH == sl)
        def _(sl=sl):
            load_positions(pos2_ref)
            wait_rows(sl)
            for t in range(tc):
                issue(t, (sl + 2) % COMBINE_DEPTH)
            wcol = wcol_ref[...]
            y = (x_ref[...] + wcol[:, 0:1] * _from_token_tiles(gbuf.at[sl, 0], tc)
                 + wcol[:, 1:2] * _from_token_tiles(gbuf.at[sl, 1], tc))
            o_ref[...] = _rms(y, fn_ref[...])

            @pl.when(i == n - 1)
            def _():
                wait_rows((sl + 1) % COMBINE_DEPTH)
                wait_rows((sl + 2) % COMBINE_DEPTH)


def _moe_combine(pos, x2, wcol, final_g, ys, tc):
    t, d = x2.shape
    n = t // tc
    rpt = TOKEN_TILE_ROWS

    def pos_spec(ahead):
        return pl.BlockSpec((N_EXPERTS, tc), lambda i: (0, jnp.minimum(i + ahead, n - 1)))

    return pl.pallas_call(
        _combine_body,
        grid=(n,),
        in_specs=[pos_spec(0), pos_spec(1), pos_spec(2),
                  pl.BlockSpec((tc, d), lambda i: (i, 0)),
                  pl.BlockSpec((tc, LANES), lambda i: (i, 0)),
                  _const_spec((1, d)),
                  pl.BlockSpec(memory_space=pl.ANY)],
        out_specs=pl.BlockSpec((tc, d), lambda i: (i, 0)),
        out_shape=jax.ShapeDtypeStruct((t, d), F32),
        scratch_shapes=[pltpu.VMEM((COMBINE_DEPTH, 2, tc * rpt, LANES), F32),
                        pltpu.SMEM((N_EXPERTS, tc), jnp.int32),
                        pltpu.SemaphoreType.DMA((COMBINE_DEPTH,)), pltpu.SemaphoreType.DMA],
        compiler_params=_params("arbitrary"),
        name="moe_combine_norm",
    )(pos, pos, pos, x2, wcol, final_g, ys)


def _tile_schedule(counts, tm, cap, nt):
    ntile = (counts + tm - 1) // tm
    ends = jnp.cumsum(ntile)
    n_used = ends[-1]
    i = jnp.minimum(jnp.arange(nt, dtype=jnp.int32), jnp.maximum(n_used - 1, 0))
    te = jnp.sum((i[:, None] >= ends[None, :]).astype(jnp.int32), axis=1)
    tb = te * (cap // tm) + i - (ends - ntile)[te]
    return te.astype(jnp.int32), tb.astype(jnp.int32), n_used.reshape(1).astype(jnp.int32)


def _row(v):
    return v.reshape(1, -1).astype(F32)


def _pad_cols(w, n):
    return jnp.pad(w, ((0, 0), (0, n - w.shape[1])))


def _conformer_layer(x3, mix_g, ffn_g, pw1_w, pw1_b, dw_w, dw_b, ln_g, ln_b, pw2_w, pw2_b,
                     w_gate, w_up, w_down, later_weights):
    bsz, s, d = x3.shape
    t = bsz * s
    tm = min(512, t)
    ts = min(512, s)
    x3, cast = _conf_mixer(x3, _row(mix_g), pw1_w.astype(BF16), _row(pw1_b), dw_w, _row(dw_b), _row(ln_g),
                           _row(ln_b), pw2_w.astype(BF16), _row(pw2_b), ts, [w_gate, w_up, w_down] + later_weights)
    x2 = _ffn(x3.reshape(t, d), _row(ffn_g), cast[0], cast[1], cast[2], tm, 256)
    return x2.reshape(bsz, s, d), cast[3:]


def _deltanet_moe_layer(x3, mix_g, ffn_g, w_in, w_in_bf16, conv_w, a_log, dt_bias, o_norm, w_out_bf16,
                        router, e_gate, e_up, e_down, final_g):
    bsz, s, d = x3.shape
    t = bsz * s
    tm = min(512, t)
    ts = min(512, s)
    nmain = 4 * d
    ne, _, dffe = e_gate.shape
    zero16 = jnp.zeros((2 * N_HEADS,), F32)
    alog_row = _row(_pad_cols(jnp.concatenate([zero16, a_log.reshape(-1)])[None], LANES))
    dtb_row = _row(_pad_cols(jnp.concatenate([zero16, dt_bias.reshape(-1)])[None], LANES))
    (z, w, kq, kd, p, egl), experts = _gdn_front(
        x3, _row(mix_g), w_in_bf16, _pad_cols(w_in[:, nmain:], LANES).astype(BF16), conv_w,
        alog_row, dtb_row, min(256, s),
        [e_gate.reshape(ne * d, dffe), e_up.reshape(ne * d, dffe), e_down.reshape(ne * dffe, d)])
    z = z.reshape(t, d)
    nbh = bsz * N_HEADS
    nck = s // CHUNK
    o_f, o_b = _gdn_scan(w.reshape(2, nbh, s, HEAD_DIM), kq.reshape(2, nbh, nck, 2 * CHUNK, HEAD_DIM),
                         kd.reshape(2, nbh, s, HEAD_DIM), p.reshape(2, nbh, s, HEAD_DIM),
                         egl.reshape(2, nbh, nck, 1, HEAD_DIM))
    cap = t + tm
    x2, pos, wcol, cnt, xs = _gdn_out(o_f.reshape(bsz, N_HEADS, s, HEAD_DIM), o_b.reshape(bsz, N_HEADS, s, HEAD_DIM),
                                      z, x3.reshape(t, d), _row(o_norm), w_out_bf16, _row(ffn_g),
                                      router.T, ts, tm, cap)
    te, tb, n_used = _tile_schedule(cnt[:, 0], tm, cap, 2 * t // tm + N_EXPERTS)
    tf = dffe // 2 if (dffe // 2) % 256 == 0 else dffe
    ys = _moe_grouped(xs, te, tb, n_used, experts[0].reshape(ne, d, dffe), experts[1].reshape(ne, d, dffe),
                      experts[2].reshape(ne, dffe, d), tm, tf, 256)
    out = _moe_combine(pos, x2, wcol, _row(final_g), ys, ts)
    return out.reshape(bsz, s, d)


def kernel(x, mix_norm, ffn_norm, cf_pw1_w, cf_pw1_b, cf_dw_w, cf_dw_b, cf_ln_g, cf_ln_b, cf_pw2_w, cf_pw2_b, ffn_w_gate, ffn_w_up, ffn_w_down, gdn_w_in, gdn_conv_w, gdn_a_log, gdn_dt_bias, gdn_o_norm, gdn_w_out, moe_router, moe_w_gate, moe_w_up, moe_w_down, final_norm):
    x, (w_in_bf16, w_out_bf16) = _conformer_layer(
        x, mix_norm[0], ffn_norm[0], cf_pw1_w[0], cf_pw1_b[0], cf_dw_w[0], cf_dw_b[0],
        cf_ln_g[0], cf_ln_b[0], cf_pw2_w[0], cf_pw2_b[0],
        ffn_w_gate[0], ffn_w_up[0], ffn_w_down[0], [gdn_w_in[0], gdn_w_out[0]])
    return _deltanet_moe_layer(x, mix_norm[1], ffn_norm[1], gdn_w_in[0], w_in_bf16, gdn_conv_w[0], gdn_a_log[0],
                               gdn_dt_bias[0], gdn_o_norm[0], w_out_bf16, moe_router[0],
                               moe_w_gate[0], moe_w_up[0], moe_w_down[0], final_norm)
```

```python
import functools

import jax
import jax.numpy as jnp
from jax import lax
from jax.experimental import pallas as pl
from jax.experimental.pallas import tpu as pltpu

F32 = jnp.float32
BF16 = jnp.bfloat16

RMS_EPS = 1e-6
LN_EPS = 1e-5
L2_EPS = 1e-6
N_HEADS = 8
HEAD_DIM = 128
CHUNK = 64
N_EXPERTS = 8
LANES = 128
BF16_SUBLANES = 16
CONV_HALO = 16
SHORT_HALO = 8
VMEM_LIMIT_BYTES = 56 * 1024 * 1024


def _params(*sem):
    return pltpu.CompilerParams(dimension_semantics=sem, vmem_limit_bytes=VMEM_LIMIT_BYTES)


def _const_spec(shape):
    nd = len(shape)
    return pl.BlockSpec(shape, lambda *_: (0,) * nd, pipeline_mode=pl.Buffered(1))


def _cast_streams(arrays, grid):
    nsteps = grid[0] * grid[1]
    in_specs, out_specs, out_shapes = [], [], []
    for a in arrays:
        rows, cols = a.shape
        blk = next(r for r in range(BF16_SUBLANES, rows + 1, BF16_SUBLANES)
                   if rows % r == 0 and rows // r <= nsteps)
        nblk = rows // blk

        def imap(b, i, nblk=nblk):
            return (jnp.minimum(b * grid[1] + i, nblk - 1), 0)

        in_specs.append(pl.BlockSpec((blk, cols), imap))
        out_specs.append(pl.BlockSpec((blk, cols), imap))
        out_shapes.append(jax.ShapeDtypeStruct((rows, cols), BF16))
    return in_specs, out_specs, out_shapes


def _with_casts(body, n_in, n_out, n_cast):
    def wrapped(*refs):
        a, b, c = n_in + n_cast, n_in + n_cast + n_out, n_in + 2 * n_cast + n_out
        for src, dst in zip(refs[n_in:a], refs[b:c]):
            dst[...] = src[...].astype(BF16)
        body(*refs[:n_in], *refs[a:b], *refs[c:])
    return wrapped


def _rms(x, g):
    return x * lax.rsqrt(jnp.mean(x * x, axis=-1, keepdims=True) + RMS_EPS) * g


def _silu(x):
    return x * jax.nn.sigmoid(x)


def _mm(a, b):
    return jnp.dot(a.astype(BF16), b.astype(BF16), preferred_element_type=F32)


def _mm_nt(a, b):
    return lax.dot_general(a.astype(BF16), b.astype(BF16), (((1,), (1,)), ((), ())),
                           preferred_element_type=F32)


def _mm_tn(a, b):
    return lax.dot_general(a.astype(BF16), b.astype(BF16), (((0,), (0,)), ((), ())),
                           preferred_element_type=F32)


def _mm_exact(a, b):
    return jnp.dot(a, b, preferred_element_type=F32, precision=lax.Precision.HIGHEST)


def _conf_mixer_body(x_ref, xp_ref, xn_ref, g_ref, w1_ref, b1_ref, dw_ref, dwb_ref, lng_ref, lnb_ref,
                     w2_ref, b2_ref, o_ref, xe_ref, ext_ref, cv_ref, *, nc):
    s = pl.program_id(1)
    ns = pl.num_programs(1)
    ts, d = cv_ref.shape
    width = dw_ref.shape[0]
    base = CONV_HALO - width // 2
    span = ts + 8 * ((base + width - 1) // 8)
    xe_ref[0:CONV_HALO, :] = xp_ref[0]
    xe_ref[CONV_HALO:CONV_HALO + ts, :] = x_ref[0]
    xe_ref[CONV_HALO + ts:, :] = xn_ref[0]
    h = _rms(xe_ref[...], g_ref[...]).astype(BF16)
    for c in range(d // nc):
        a = _mm(h, w1_ref[:, c * nc:(c + 1) * nc]) + b1_ref[:, c * nc:(c + 1) * nc]
        b = _mm(h, w1_ref[:, d + c * nc:d + (c + 1) * nc]) + b1_ref[:, d + c * nc:d + (c + 1) * nc]
        ext_ref[0, :, c * nc:(c + 1) * nc] = a * jax.nn.sigmoid(b)

    @pl.when(s == 0)
    def _():
        ext_ref[0, 0:CONV_HALO, :] = jnp.zeros((CONV_HALO, d), F32)

    @pl.when(s == ns - 1)
    def _():
        ext_ref[0, CONV_HALO + ts:, :] = jnp.zeros((CONV_HALO, d), F32)

    for p in range(1, 8):
        ext_ref[p, 0:span, :] = ext_ref[0, pl.ds(p, span), :]
    rb = 128
    for c in range(d // LANES):
        cs = slice(c * LANES, (c + 1) * LANES)
        for r in range(ts // rb):
            acc = jnp.zeros((rb, LANES), F32)
            for k in range(width):
                off = base + k
                acc = acc + dw_ref[k:k + 1, cs] * ext_ref[off % 8, pl.ds(r * rb + 8 * (off // 8), rb), cs]
            cv_ref[r * rb:(r + 1) * rb, cs] = acc + dwb_ref[:, cs]
    y = cv_ref[...]
    mu = jnp.mean(y, axis=-1, keepdims=True)
    yc = y - mu
    yn = yc * lax.rsqrt(jnp.mean(yc * yc, axis=-1, keepdims=True) + LN_EPS) * lng_ref[...] + lnb_ref[...]
    o_ref[0] = x_ref[0] + _mm(_silu(yn), w2_ref[...]) + b2_ref[...]


def _conf_mixer(x3, g, w1, b1, dw_w, dw_b, ln_g, ln_b, w2, b2, ts, casts):
    bsz, s, d = x3.shape
    width = dw_w.shape[0]
    hb = ts // CONV_HALO
    nhb = s // CONV_HALO
    grid = (bsz, s // ts)
    c_in, c_out, c_shape = _cast_streams(casts, grid)
    res = pl.pallas_call(
        _with_casts(functools.partial(_conf_mixer_body, nc=256), 12, 1, len(casts)),
        grid=grid,
        in_specs=[pl.BlockSpec((1, ts, d), lambda b, i: (b, i, 0)),
                  pl.BlockSpec((1, CONV_HALO, d), lambda b, i: (b, jnp.maximum(i * hb - 1, 0), 0)),
                  pl.BlockSpec((1, CONV_HALO, d), lambda b, i: (b, jnp.minimum((i + 1) * hb, nhb - 1), 0)),
                  _const_spec((1, d)), _const_spec((d, 2 * d)), _const_spec((1, 2 * d)),
                  _const_spec((width, d)), _const_spec((1, d)), _const_spec((1, d)), _const_spec((1, d)),
                  _const_spec((d, d)), _const_spec((1, d))] + c_in,
        out_specs=[pl.BlockSpec((1, ts, d), lambda b, i: (b, i, 0))] + c_out,
        out_shape=[jax.ShapeDtypeStruct((bsz, s, d), F32)] + c_shape,
        scratch_shapes=[pltpu.VMEM((ts + 2 * CONV_HALO, d), F32),
                        pltpu.VMEM((8, ts + 2 * CONV_HALO, d), F32), pltpu.VMEM((ts, d), F32)],
        compiler_params=_params("parallel", "parallel"),
        name="conf_mixer",
    )(x3, x3, x3, g, w1, b1, dw_w, dw_b, ln_g, ln_b, w2, b2, *casts)
    return res[0], res[1:]


def _ffn_body(x_ref, g_ref, wg_ref, wu_ref, wd_ref, o_ref, *, fc):
    x = x_ref[...]
    h = _rms(x, g_ref[...]).astype(BF16)
    dff = wg_ref.shape[1]
    acc = x
    for c in range(dff // fc):
        cs = slice(c * fc, (c + 1) * fc)
        a = _silu(_mm(h, wg_ref[:, cs])) * _mm(h, wu_ref[:, cs])
        acc = acc + _mm(a, wd_ref[cs, :])
    o_ref[...] = acc


def _ffn(x2d, g, wg, wu, wd, tm, fc):
    t, d = x2d.shape
    dff = wg.shape[1]
    return pl.pallas_call(
        functools.partial(_ffn_body, fc=fc),
        grid=(t // tm,),
        in_specs=[pl.BlockSpec((tm, d), lambda i: (i, 0)), _const_spec((1, d)),
                  _const_spec((d, dff)), _const_spec((d, dff)), _const_spec((dff, d))],
        out_specs=pl.BlockSpec((tm, d), lambda i: (i, 0)),
        out_shape=jax.ShapeDtypeStruct((t, d), F32),
        compiler_params=_params("parallel"),
        name="dense_swiglu",
    )(x2d, g, wg, wu, wd)


PAIR = 2 * CHUNK


def _block_diag(yb, mk_ref):
    return jnp.concatenate([yb * mk_ref[MASK_LEFT].astype(BF16), yb * mk_ref[MASK_RIGHT].astype(BF16)], axis=0)


def _tri_inverse_pairs(mats, mk_ref):
    def bd(vals):
        return [_block_diag(v.astype(BF16), mk_ref) for v in vals]

    l0 = [a * mk_ref[MASK_BLOCK8] for a in mats]
    l0b = [a.astype(BF16) for a in l0]
    l2 = [_mm(a, d) for a, d in zip(l0b, bd(l0b))]
    l2b = [a.astype(BF16) for a in l2]
    l2d = bd(l2b)
    l4 = [_mm(a, d) for a, d in zip(l2b, l2d)]
    l3 = [_mm(a, d) for a, d in zip(l0b, l2d)]
    xs = [mk_ref[MASK_EYE] - a + b - t for a, b, t in zip(l0, l2, l3)]
    x4 = [_mm(x, d) for x, d in zip(xs, bd(l4))]
    xs = [x + t for x, t in zip(xs, x4)]
    for level in range(MASK_MERGE0, MASK_MERGE0 + 3):
        xb = [x.astype(BF16) for x in xs]
        t1 = [_mm(a * mk_ref[level], d) for a, d in zip(mats, bd(xb))]
        t2 = [_mm(x, d) for x, d in zip(xb, bd(t1))]
        xs = [x - t for x, t in zip(xs, t2)]
    return xs


(MASK_INCL, MASK_INCL_L, MASK_INCL_R, MASK_STRICT_L, MASK_STRICT_R) = (0, 2, 4, 6, 8)
MASK_EYE, MASK_BLOCK8, MASK_MERGE0, MASK_LEFT, MASK_RIGHT = 10, 11, 12, 15, 16
N_MASKS = 17


def _chunk_masks():
    ri = lax.broadcasted_iota(jnp.int32, (CHUNK, PAIR), 0)
    li = lax.broadcasted_iota(jnp.int32, (CHUNK, PAIR), 1)
    ci = li % CHUNK
    left, right = li < CHUNK, li >= CHUNK
    incl = [ci <= ri, ci >= ri]
    strict = [ci < ri, ci > ri]
    masks = (incl + [m & left for m in incl] + [m & right for m in incl]
             + [m & left for m in strict] + [m & right for m in strict]
             + [ci == ri, (ri // 8) == (ci // 8)])
    size = 8
    while size < CHUNK:
        masks.append(((ri // (2 * size)) == (ci // (2 * size))) & ((ri // size) != (ci // size)))
        size *= 2
    masks += [left, right]
    assert len(masks) == N_MASKS
    return jnp.stack(masks).astype(F32)


GDN_HEAD_GROUP = 4


def _gdn_front_body(x_ref, xp_ref, xn_ref, g_ref, w_ref, wgate_ref, cw_ref, alog_ref, dtb_ref, mk_ref,
                    z_ref, w_out_ref, kq_ref, kd_ref, p_ref, egl_ref, xe_ref, pe_ref):
    s = pl.program_id(1)
    ns = pl.num_programs(1)
    tg, d = z_ref.shape[1:]
    kw = cw_ref.shape[0]
    dk = N_HEADS * HEAD_DIM
    npair = tg // PAIR
    gw = GDN_HEAD_GROUP * HEAD_DIM
    ngroup = N_HEADS // GDN_HEAD_GROUP
    base = SHORT_HALO - kw // 2

    xe_ref[0:SHORT_HALO, :] = jnp.where(s > 0, xp_ref[0], 0.0)
    xe_ref[SHORT_HALO:SHORT_HALO + tg, :] = x_ref[0]
    xe_ref[SHORT_HALO + tg:, :] = jnp.where(s < ns - 1, xn_ref[0], 0.0)
    h = _rms(xe_ref[...], g_ref[...]).astype(BF16)
    hm = h[SHORT_HALO:SHORT_HALO + tg]

    gates = _mm(hm, wgate_ref[...])
    beta = jax.nn.sigmoid(gates)
    xa = gates + dtb_ref[...]
    softplus = jnp.maximum(xa, 0.0) + jnp.log1p(jnp.exp(-jnp.abs(xa)))
    log_a = -jnp.exp(alog_ref[...]) * softplus
    ri = lax.broadcasted_iota(jnp.int32, (tg, tg), 0)
    ci = lax.broadcasted_iota(jnp.int32, (tg, tg), 1)
    same = (ri // CHUNK) == (ci // CHUNK)
    cum_f = _mm_exact((same & (ci <= ri)).astype(F32), log_a)
    cum_b = _mm_exact((same & (ci >= ri)).astype(F32), log_a)
    lane = lax.broadcasted_iota(jnp.int32, (tg, LANES), 1)
    g_all = jnp.where(lane < 3 * N_HEADS, cum_f, cum_b)
    g_all_t = g_all.T
    beta_t = beta.T

    def project(grp):
        out = []
        for part in range(3):
            col0 = part * dk + grp * gw
            pe_ref[grp % 2, part] = _mm(h, w_ref[:, col0:col0 + gw])
            heads = []
            for c in range(GDN_HEAD_GROUP):
                col = col0 + c * LANES
                acc = jnp.zeros((tg, LANES), F32)
                for k in range(kw):
                    acc = acc + (cw_ref[k:k + 1, col:col + LANES]
                                 * pe_ref[grp % 2, part, pl.ds(base + k, tg), c * LANES:(c + 1) * LANES])
                a = _silu(acc)
                if part < 2:
                    a = a * lax.rsqrt(jnp.sum(a * a, axis=-1, keepdims=True) + L2_EPS)
                if part == 0:
                    a = a * (HEAD_DIM ** -0.5)
                heads.append(a)
            out.append(heads)
        return out

    def chunk_systems(grp, qkv):
        heads = range(grp * GDN_HEAD_GROUP, (grp + 1) * GDN_HEAD_GROUP)
        qs, ks, vs = (dict(zip(heads, part)) for part in qkv)
        raw, v16 = {}, {}
        for hd in heads:
            k16 = ks[hd].astype(BF16)
            q16 = qs[hd].astype(BF16)
            v16[hd] = vs[hd].astype(BF16)
            for m in range(npair):
                pr = slice(m * PAIR, (m + 1) * PAIR)
                raw[hd, m] = _mm_nt(jnp.concatenate([k16[pr], q16[pr]], axis=0), k16[pr])
        insts = [(hd, dr, m) for hd in heads for dr in range(2) for m in range(npair)]
        zero = jnp.zeros((CHUNK, HEAD_DIM), BF16)
        lows, rhss, beta_rows = [], [], []
        for hd, dr, m in insts:
            pr = slice(m * PAIR, (m + 1) * PAIR)
            bcol = dr * N_HEADS + hd
            gcol = 2 * N_HEADS + dr * N_HEADS + hd
            bt = beta[pr, bcol:bcol + 1]
            beta_rows.append(beta_t[bcol:bcol + 1, pr])
            g_c = g_all[pr, gcol:gcol + 1]
            g_r = g_all_t[gcol:gcol + 1, pr]
            diff = g_c - g_r
            decay = jnp.exp(diff[:CHUNK] * mk_ref[MASK_INCL_L + dr] + diff[CHUNK:] * mk_ref[MASK_INCL_R + dr])
            decay = decay * mk_ref[MASK_INCL + dr]
            kk = raw[hd, m][:PAIR] * bt
            qk = raw[hd, m][PAIR:]
            lows.append((kk[:CHUNK] * mk_ref[MASK_STRICT_L + dr] + kk[CHUNK:] * mk_ref[MASK_STRICT_R + dr]) * decay)
            pq = (qk[:CHUNK] * mk_ref[MASK_LEFT] + qk[CHUNK:] * mk_ref[MASK_RIGHT]) * decay
            half = mk_ref[MASK_LEFT + dr]
            p_tiles = [(t * half).astype(BF16) for t in (pq, pltpu.roll(pq, CHUNK, axis=1))]
            eg = jnp.exp(g_c)
            keg = (ks[hd][pr] * eg).astype(BF16)
            qeg = (qs[hd][pr] * eg).astype(BF16)
            vp = v16[hd][pr]
            rhss.append(jnp.concatenate(
                [jnp.concatenate([vp[:CHUNK], keg[:CHUNK], zero, zero], axis=1),
                 jnp.concatenate([zero, zero, vp[CHUNK:], keg[CHUNK:]], axis=1)], axis=0))
            for j in range(2):
                n = 2 * m + j
                rs = slice(n * CHUNK, (n + 1) * CHUNK)
                js = slice(j * CHUNK, (j + 1) * CHUNK)
                last = n * CHUNK + (CHUNK - 1 if dr == 0 else 0)
                g_l = g_all[last:last + 1, gcol:gcol + 1]
                p_ref[dr, 0, hd, rs, :] = p_tiles[(j + dr) % 2]
                kq_ref[dr, 0, hd, n, CHUNK:, :] = qeg[js]
                kd_ref[dr, 0, hd, rs, :] = (ks[hd][rs] * jnp.exp(g_l - g_all[rs, gcol:gcol + 1])).astype(BF16)
                egl_ref[dr, 0, hd, n] = jnp.broadcast_to(jnp.exp(g_l), (1, HEAD_DIM))
        invs = _tri_inverse_pairs(lows, mk_ref)
        sols = [_mm(inv * br, rhs) for inv, br, rhs in zip(invs, beta_rows, rhss)]
        for (hd, dr, m), sol in zip(insts, sols):
            for j in range(2):
                n = 2 * m + j
                rs = slice(n * CHUNK, (n + 1) * CHUNK)
                w_out_ref[dr, 0, hd, rs, :] = sol[:, 2 * j * HEAD_DIM:(2 * j + 1) * HEAD_DIM]
                kq_ref[dr, 0, hd, n, :CHUNK, :] = sol[:, (2 * j + 1) * HEAD_DIM:(2 * j + 2) * HEAD_DIM].astype(BF16)

    qkv = project(0)
    for grp in range(ngroup):
        nxt = project(grp + 1) if grp + 1 < ngroup else None
        if nxt is None:
            z_ref[0] = _mm(hm, w_ref[:, 3 * dk:3 * dk + d])
        chunk_systems(grp, qkv)
        qkv = nxt


def _gdn_front(x3, g, w_in, w_gate, conv_w, alog_row, dtb_row, tg, casts):
    bsz, s, d = x3.shape
    nin = w_in.shape[1]
    kw, nq = conv_w.shape
    nchunk = tg // CHUNK
    hb = tg // SHORT_HALO
    nhb = s // SHORT_HALO
    gw = GDN_HEAD_GROUP * HEAD_DIM

    def rows(dtype, last=HEAD_DIM):
        return (jax.ShapeDtypeStruct((2, bsz, N_HEADS, s, last), dtype),
                pl.BlockSpec((2, 1, N_HEADS, tg, last), lambda b, i: (0, b, 0, i, 0)))

    def per_chunk(dtype, r):
        return (jax.ShapeDtypeStruct((2, bsz, N_HEADS, s // CHUNK, r, HEAD_DIM), dtype),
                pl.BlockSpec((2, 1, N_HEADS, nchunk, r, HEAD_DIM), lambda b, i: (0, b, 0, i, 0, 0)))

    outs = [(jax.ShapeDtypeStruct((bsz, s, d), F32), pl.BlockSpec((1, tg, d), lambda b, i: (b, i, 0))),
            rows(F32), per_chunk(BF16, 2 * CHUNK), rows(BF16), rows(BF16), per_chunk(F32, 1)]
    grid = (bsz, s // tg)
    c_in, c_out, c_shape = _cast_streams(casts, grid)
    res = pl.pallas_call(
        _with_casts(_gdn_front_body, 10, len(outs), len(casts)),
        grid=grid,
        in_specs=[pl.BlockSpec((1, tg, d), lambda b, i: (b, i, 0)),
                  pl.BlockSpec((1, SHORT_HALO, d), lambda b, i: (b, jnp.maximum(i * hb - 1, 0), 0)),
                  pl.BlockSpec((1, SHORT_HALO, d), lambda b, i: (b, jnp.minimum((i + 1) * hb, nhb - 1), 0)),
                  _const_spec((1, d)), _const_spec((d, nin)), _const_spec((d, LANES)),
                  _const_spec((kw, nq)), _const_spec((1, LANES)), _const_spec((1, LANES)),
                  _const_spec((N_MASKS, CHUNK, PAIR))] + c_in,
        out_specs=[o[1] for o in outs] + c_out,
        out_shape=[o[0] for o in outs] + c_shape,
        scratch_shapes=[pltpu.VMEM((tg + 2 * SHORT_HALO, d), F32),
                        pltpu.VMEM((2, 3, tg + 2 * SHORT_HALO, gw), F32)],
        compiler_params=_params("parallel", "parallel"),
        name="gdn_front",
    )(x3, x3, x3, g, w_in, w_gate, conv_w, alog_row, dtb_row, _chunk_masks(), *casts)
    return res[:len(outs)], res[len(outs):]


GDN_SCAN_GROUP = 16


def _gdn_scan_body(wf, kqf, kdf, pf, eglf, wb, kqb, kdb, pb, eglb, of_ref, ob_ref, state_ref):
    n = pl.program_id(0)

    @pl.when(n == 0)
    def _():
        state_ref[...] = jnp.zeros_like(state_ref)

    nbh = state_ref.shape[1]
    dirs = ((wf, kqf, kdf, pf, eglf, of_ref), (wb, kqb, kdb, pb, eglb, ob_ref))

    def body(i, carry):
        bhs = [i * GDN_SCAN_GROUP + j for j in range(GDN_SCAN_GROUP)]
        zero = jnp.zeros((CHUNK, HEAD_DIM), BF16)
        st = [[state_ref[dr, b] for dr in range(2)] for b in bhs]
        r = [[_mm(dirs[dr][1][0, b, 0], s_[dr]) for dr in range(2)] for b, s_ in zip(bhs, st)]
        ub = [[(dirs[dr][0][0, b] - r_[dr][:CHUNK]).astype(BF16) for dr in range(2)] for b, r_ in zip(bhs, r)]
        ud = [jnp.concatenate([jnp.concatenate([u_[0], zero], axis=1), jnp.concatenate([zero, u_[1]], axis=1)],
                              axis=0) for u_ in ub]
        pu = [_mm(dirs[0][3][0, b] + dirs[1][3][0, b], d_) for b, d_ in zip(bhs, ud)]
        ku = [_mm_tn(jnp.concatenate([dirs[0][2][0, b], dirs[1][2][0, b]], axis=0), d_) for b, d_ in zip(bhs, ud)]
        for b, r_, pu_, ku_, s_ in zip(bhs, r, pu, ku, st):
            for dr in range(2):
                cs = slice(dr * HEAD_DIM, (dr + 1) * HEAD_DIM)
                dirs[dr][5][b] = r_[dr][CHUNK:] + pu_[:, cs]
                state_ref[dr, b] = s_[dr] * dirs[dr][4][0, b, 0] + ku_[:, cs]
        return carry

    lax.fori_loop(0, nbh // GDN_SCAN_GROUP, body, 0)


def _gdn_scan(w, kq, kd, p, egl):
    _, nbh, s, dh = w.shape
    nchunk = s // CHUNK

    def specs(dr):
        def im(n):
            return (dr, 0, n if dr == 0 else nchunk - 1 - n, 0)

        def im5(n):
            return im(n) + (0,)
        return [pl.BlockSpec((1, nbh, CHUNK, dh), im), pl.BlockSpec((1, nbh, 1, 2 * CHUNK, dh), im5),
                pl.BlockSpec((1, nbh, CHUNK, dh), im), pl.BlockSpec((1, nbh, CHUNK, dh), im),
                pl.BlockSpec((1, nbh, 1, 1, dh), im5)]

    o_shape = jax.ShapeDtypeStruct((nbh, s, dh), F32)
    return pl.pallas_call(
        _gdn_scan_body,
        grid=(nchunk,),
        in_specs=specs(0) + specs(1),
        out_specs=[pl.BlockSpec((nbh, CHUNK, dh), lambda n: (0, n, 0)),
                   pl.BlockSpec((nbh, CHUNK, dh), lambda n: (0, nchunk - 1 - n, 0))],
        out_shape=[o_shape, o_shape],
        scratch_shapes=[pltpu.VMEM((2, nbh, dh, dh), F32)],
        compiler_params=_params("arbitrary"),
        name="gdn_scan",
    )(w, kq, kd, p, egl, w, kq, kd, p, egl)


TOKEN_TILE_ROWS = 8
DMA_ISSUE_UNROLL = 8


def _to_token_tiles(dst_ref, val):
    n = val.shape[0]
    for j in range(TOKEN_TILE_ROWS):
        dst_ref[pl.ds(j, n, stride=TOKEN_TILE_ROWS), :] = val[:, j * LANES:(j + 1) * LANES]


def _from_token_tiles(src_ref, n):
    return jnp.concatenate([src_ref[pl.ds(j, n, stride=TOKEN_TILE_ROWS), :] for j in range(TOKEN_TILE_ROWS)],
                           axis=1)


def _gdn_out_body(of_ref, ob_ref, z_ref, x_ref, on_ref, wo_ref, fg_ref, rwt_ref,
                  xo_ref, pos_ref, wcol_ref, cnt_ref, xs_ref,
                  act_ref, tile_ref, zero_ref, carry_ref, pos_vmem, pos_smem, cnt_smem, sem, psem, zsem,
                  *, cap, tm):
    g = pl.program_id(0)
    ng = pl.num_programs(0)
    ts = x_ref.shape[0]
    slot = g % 2
    prev = 1 - slot
    rpt = TOKEN_TILE_ROWS

    def wait_rows(s):
        for _ in range(2):
            pltpu.make_async_copy(tile_ref.at[s], xs_ref.at[pl.ds(0, ts * rpt)], sem.at[s]).wait()

    def dispatch_row(t, s):
        src = tile_ref.at[s, pl.ds(t * rpt, rpt)]
        for k in range(2):
            dst = xs_ref.at[pl.ds(pos_smem[s, k, t] * rpt, rpt)]
            pltpu.make_async_copy(src, dst, sem.at[s]).start()

    def positions_to_smem(s):
        return pltpu.make_async_copy(pos_vmem.at[s], pos_smem.at[s], psem)

    def tile_step(dispatch_prev):
        if dispatch_prev:
            positions_to_smem(prev).wait()
            for t in range(ts):
                dispatch_row(t, prev)
        for h in range(N_HEADS):
            hs = slice(h * HEAD_DIM, (h + 1) * HEAD_DIM)
            o = of_ref[0, h] + ob_ref[0, h]
            o = o * lax.rsqrt(jnp.mean(o * o, axis=-1, keepdims=True) + RMS_EPS) * on_ref[...]
            act_ref[:, hs] = (o * _silu(z_ref[:, hs])).astype(BF16)
        x = x_ref[...] + _mm(act_ref[...], wo_ref[...])
        xo_ref[...] = x
        hn = _rms(x, fg_ref[...])

        logits = lax.dot_general(rwt_ref[...], hn, (((1,), (1,)), ((), ())),
                                 preferred_element_type=F32, precision=lax.Precision.HIGHEST)
        eidx = lax.broadcasted_iota(jnp.int32, logits.shape, 0).astype(F32)
        neg = jnp.float32(-jnp.inf)
        m1 = jnp.max(logits, axis=0, keepdims=True)
        i1 = jnp.min(jnp.where(logits == m1, eidx, float(N_EXPERTS)), axis=0, keepdims=True)
        one1 = eidx == i1
        rest = jnp.where(one1, neg, logits)
        m2 = jnp.max(rest, axis=0, keepdims=True)
        i2 = jnp.min(jnp.where(rest == m2, eidx, float(N_EXPERTS)), axis=0, keepdims=True)
        one2 = eidx == i2
        e2 = jnp.exp(m2 - m1)
        w1 = 1.0 / (1.0 + e2)
        w2 = e2 * w1

        chosen = jnp.where(one1 | one2, 1.0, 0.0)
        ri = lax.broadcasted_iota(jnp.int32, (ts, ts), 0)
        ci = lax.broadcasted_iota(jnp.int32, (ts, ts), 1)
        before = jnp.where(ri < ci, 1.0, 0.0).astype(BF16)
        rank = jnp.dot(chosen.astype(BF16), before, preferred_element_type=F32)
        carry = carry_ref[...]
        slot_f = eidx * float(cap) + carry[:, 0:1] + rank
        p1 = jnp.sum(jnp.where(one1, slot_f, 0.0), axis=0, keepdims=True)
        p2 = jnp.sum(jnp.where(one2, slot_f, 0.0), axis=0, keepdims=True)
        carry = carry + jnp.sum(chosen, axis=1, keepdims=True)
        carry_ref[...] = carry
        cnt_ref[...] = carry.astype(jnp.int32)
        row8 = lax.broadcasted_iota(jnp.int32, (N_EXPERTS, ts), 0)
        pos = jnp.where(row8 == 0, p1, jnp.where(row8 == 1, p2, 0.0)).astype(jnp.int32)
        pos_ref[...] = pos
        pos_vmem[slot] = pos
        row128 = lax.broadcasted_iota(jnp.int32, (LANES, ts), 0)
        wcol_ref[...] = jnp.where(row128 == 0, w1, jnp.where(row128 == 1, w2, 0.0)).T

        _to_token_tiles(tile_ref.at[slot], hn)
        if dispatch_prev:
            wait_rows(prev)
        positions_to_smem(slot).start()

    @pl.when(g == 0)
    def _():
        carry_ref[...] = jnp.zeros_like(carry_ref)
        tile_step(False)

    @pl.when(g > 0)
    def _():
        tile_step(True)

    @pl.when(g == ng - 1)
    def _():
        positions_to_smem(slot).wait()
        lax.fori_loop(0, ts, lambda t, c: (dispatch_row(t, slot), c)[1], 0, unroll=DMA_ISSUE_UNROLL)
        wait_rows(slot)
        zero_ref[...] = jnp.zeros_like(zero_ref)
        cc = pltpu.make_async_copy(cnt_ref, cnt_smem, psem)
        cc.start()
        cc.wait()
        tails = [pltpu.make_async_copy(
            zero_ref, xs_ref.at[pl.ds((e * cap + cnt_smem[e, 0]) * rpt, tm * rpt)], zsem)
            for e in range(N_EXPERTS)]
        for c in tails:
            c.start()
        for c in tails:
            c.wait()


def _gdn_out(o_f, o_b, z2, x2, o_norm, w_out, ffn_g, router_t, ts, tm, cap):
    t, d = x2.shape
    s = o_f.shape[2]
    spb = s // ts
    rpt = TOKEN_TILE_ROWS
    o_spec = pl.BlockSpec((1, N_HEADS, ts, HEAD_DIM), lambda g: (g // spb, 0, g % spb, 0))
    return pl.pallas_call(
        functools.partial(_gdn_out_body, cap=cap, tm=tm),
        grid=(t // ts,),
        in_specs=[o_spec, o_spec,
                  pl.BlockSpec((ts, d), lambda g: (g, 0)), pl.BlockSpec((ts, d), lambda g: (g, 0)),
                  _const_spec((1, HEAD_DIM)), _const_spec((d, d)), _const_spec((1, d)),
                  _const_spec((N_EXPERTS, d))],
        out_specs=[pl.BlockSpec((ts, d), lambda g: (g, 0)),
                   pl.BlockSpec((N_EXPERTS, ts), lambda g: (0, g)),
                   pl.BlockSpec((ts, LANES), lambda g: (g, 0)),
                   pl.BlockSpec((N_EXPERTS, LANES), lambda g: (0, 0)),
                   pl.BlockSpec(memory_space=pl.ANY)],
        out_shape=[jax.ShapeDtypeStruct((t, d), F32),
                   jax.ShapeDtypeStruct((N_EXPERTS, t), jnp.int32),
                   jax.ShapeDtypeStruct((t, LANES), F32),
                   jax.ShapeDtypeStruct((N_EXPERTS, LANES), jnp.int32),
                   jax.ShapeDtypeStruct((N_EXPERTS * cap * rpt, LANES), F32)],
        scratch_shapes=[pltpu.VMEM((ts, d), BF16),
                        pltpu.VMEM((2, ts * rpt, LANES), F32),
                        pltpu.VMEM((tm * rpt, LANES), F32),
                        pltpu.VMEM((N_EXPERTS, LANES), F32),
                        pltpu.VMEM((2, N_EXPERTS, ts), jnp.int32),
                        pltpu.SMEM((2, N_EXPERTS, ts), jnp.int32),
                        pltpu.SMEM((N_EXPERTS, LANES), jnp.int32),
                        pltpu.SemaphoreType.DMA((2,)), pltpu.SemaphoreType.DMA, pltpu.SemaphoreType.DMA],
        compiler_params=_params("arbitrary"),
        name="gdn_out_router",
    )(o_f, o_b, z2, x2, o_norm, w_out, ffn_g, router_t)


def _moe_body(te_ref, tb_ref, nu_ref, xs_ref, wg_ref, wu_ref, wd_ref, ys_ref, xb_ref, acc_ref, *, fc):
    i = pl.program_id(0)
    f = pl.program_id(1)
    tm = xb_ref.shape[0]
    tf = wg_ref.shape[2]

    @pl.when((i == 0) & (f == 0))
    def _():
        acc_ref[...] = jnp.zeros_like(acc_ref)

    @pl.when(i < nu_ref[0])
    def _():
        @pl.when(f == 0)
        def _():
            xb_ref[...] = _from_token_tiles(xs_ref, tm).astype(BF16)

        x = xb_ref[...]
        for c in range(tf // fc):
            cs = slice(c * fc, (c + 1) * fc)
            a = _silu(_mm(x, wg_ref[0, :, cs])) * _mm(x, wu_ref[0, :, cs])
            y = _mm(a, wd_ref[0, cs, :])
            if c == 0:
                acc_ref[...] = jnp.where(f > 0, acc_ref[...], 0.0) + y
            else:
                acc_ref[...] += y

        @pl.when(f == pl.num_programs(1) - 1)
        def _():
            _to_token_tiles(ys_ref, acc_ref[...])


def _moe_grouped(xs, tile_expert, tile_block, n_used, wg, wu, wd, tm, tf, fc):
    ne, d, dff = wg.shape
    nt = tile_expert.shape[0]
    nf = dff // tf
    rpt = TOKEN_TILE_ROWS

    def fsel(i, f, nu):
        return jnp.where(i < nu[0], f, nf - 1)

    grid_spec = pltpu.PrefetchScalarGridSpec(
        num_scalar_prefetch=3,
        grid=(nt, nf),
        in_specs=[pl.BlockSpec((tm * rpt, LANES), lambda i, f, te, tb, nu: (tb[i], 0)),
                  pl.BlockSpec((1, d, tf), lambda i, f, te, tb, nu: (te[i], 0, fsel(i, f, nu))),
                  pl.BlockSpec((1, d, tf), lambda i, f, te, tb, nu: (te[i], 0, fsel(i, f, nu))),
                  pl.BlockSpec((1, tf, d), lambda i, f, te, tb, nu: (te[i], fsel(i, f, nu), 0))],
        out_specs=pl.BlockSpec((tm * rpt, LANES), lambda i, f, te, tb, nu: (tb[i], 0)),
        scratch_shapes=[pltpu.VMEM((tm, d), BF16), pltpu.VMEM((tm, d), F32)],
    )
    return pl.pallas_call(
        functools.partial(_moe_body, fc=fc),
        grid_spec=grid_spec,
        out_shape=jax.ShapeDtypeStruct(xs.shape, F32),
        compiler_params=_params("arbitrary", "arbitrary"),
        name="moe_grouped",
    )(tile_expert, tile_block, n_used, xs, wg, wu, wd)


COMBINE_DEPTH = 3


def _combine_body(pos0_ref, pos1_ref, pos2_ref, x_ref, wcol_ref, fn_ref, ys_ref, o_ref, gbuf, pos_smem, sem, psem):
    i = pl.program_id(0)
    n = pl.num_programs(0)
    tc = x_ref.shape[0]
    rpt = TOKEN_TILE_ROWS

    def load_positions(p_ref):
        cp = pltpu.make_async_copy(p_ref, pos_smem, psem)
        cp.start()
        cp.wait()

    def issue(t, sl):
        for k in range(2):
            src = ys_ref.at[pl.ds(pos_smem[k, t] * rpt, rpt)]
            pltpu.make_async_copy(src, gbuf.at[sl, k, pl.ds(t * rpt, rpt)], sem.at[sl]).start()

    def wait_rows(sl):
        for k in range(2):
            pltpu.make_async_copy(ys_ref.at[pl.ds(0, tc * rpt)], gbuf.at[sl, k], sem.at[sl]).wait()

    @pl.when(i == 0)
    def _():
        for sl, p_ref in enumerate((pos0_ref, pos1_ref)):
            load_positions(p_ref)
            lax.fori_loop(0, tc, lambda t, c, sl=sl: (issue(t, sl), c)[1], 0, unroll=DMA_ISSUE_UNROLL)

    for sl in range(COMBINE_DEPTH):
        @pl.when(i % COMBINE_DEPTH == sl)
        def _(sl=sl):
            load_positions(pos2_ref)
            wait_rows(sl)
            for t in range(tc):
                issue(t, (sl + 2) % COMBINE_DEPTH)
            wcol = wcol_ref[...]
            y = (x_ref[...] + wcol[:, 0:1] * _from_token_tiles(gbuf.at[sl, 0], tc)
                 + wcol[:, 1:2] * _from_token_tiles(gbuf.at[sl, 1], tc))
            o_ref[...] = _rms(y, fn_ref[...])

            @pl.when(i == n - 1)
            def _():
                wait_rows((sl + 1) % COMBINE_DEPTH)
                wait_rows((sl + 2) % COMBINE_DEPTH)


def _moe_combine(pos, x2, wcol, final_g, ys, tc):
    t, d = x2.shape
    n = t // tc
    rpt = TOKEN_TILE_ROWS

    def pos_spec(ahead):
        return pl.BlockSpec((N_EXPERTS, tc), lambda i: (0, jnp.minimum(i + ahead, n - 1)))

    return pl.pallas_call(
        _combine_body,
        grid=(n,),
        in_specs=[pos_spec(0), pos_spec(1), pos_spec(2),
                  pl.BlockSpec((tc, d), lambda i: (i, 0)),
                  pl.BlockSpec((tc, LANES), lambda i: (i, 0)),
                  _const_spec((1, d)),
                  pl.BlockSpec(memory_space=pl.ANY)],
        out_specs=pl.BlockSpec((tc, d), lambda i: (i, 0)),
        out_shape=jax.ShapeDtypeStruct((t, d), F32),
        scratch_shapes=[pltpu.VMEM((COMBINE_DEPTH, 2, tc * rpt, LANES), F32),
                        pltpu.SMEM((N_EXPERTS, tc), jnp.int32),
                        pltpu.SemaphoreType.DMA((COMBINE_DEPTH,)), pltpu.SemaphoreType.DMA],
        compiler_params=_params("arbitrary"),
        name="moe_combine_norm",
    )(pos, pos, pos, x2, wcol, final_g, ys)


def _tile_schedule(counts, tm, cap, nt):
    ntile = (counts + tm - 1) // tm
    ends = jnp.cumsum(ntile)
    n_used = ends[-1]
    i = jnp.minimum(jnp.arange(nt, dtype=jnp.int32), jnp.maximum(n_used - 1, 0))
    te = jnp.sum((i[:, None] >= ends[None, :]).astype(jnp.int32), axis=1)
    tb = te * (cap // tm) + i - (ends - ntile)[te]
    return te.astype(jnp.int32), tb.astype(jnp.int32), n_used.reshape(1).astype(jnp.int32)


def _row(v):
    return v.reshape(1, -1).astype(F32)


def _pad_cols(w, n):
    return jnp.pad(w, ((0, 0), (0, n - w.shape[1])))


def _conformer_layer(x3, mix_g, ffn_g, pw1_w, pw1_b, dw_w, dw_b, ln_g, ln_b, pw2_w, pw2_b,
                     w_gate, w_up, w_down, later_weights):
    bsz, s, d = x3.shape
    t = bsz * s
    tm = min(512, t)
    ts = min(512, s)
    x3, cast = _conf_mixer(x3, _row(mix_g), pw1_w.astype(BF16), _row(pw1_b), dw_w, _row(dw_b), _row(ln_g),
                           _row(ln_b), pw2_w.astype(BF16), _row(pw2_b), ts, [w_gate, w_up, w_down] + later_weights)
    x2 = _ffn(x3.reshape(t, d), _row(ffn_g), cast[0], cast[1], cast[2], tm, 256)
    return x2.reshape(bsz, s, d), cast[3:]


def _deltanet_moe_layer(x3, mix_g, ffn_g, w_in, w_in_bf16, conv_w, a_log, dt_bias, o_norm, w_out_bf16,
                        router, e_gate, e_up, e_down, final_g):
    bsz, s, d = x3.shape
    t = bsz * s
    tm = min(512, t)
    ts = min(512, s)
    nmain = 4 * d
    ne, _, dffe = e_gate.shape
    zero16 = jnp.zeros((2 * N_HEADS,), F32)
    alog_row = _row(_pad_cols(jnp.concatenate([zero16, a_log.reshape(-1)])[None], LANES))
    dtb_row = _row(_pad_cols(jnp.concatenate([zero16, dt_bias.reshape(-1)])[None], LANES))
    (z, w, kq, kd, p, egl), experts = _gdn_front(
        x3, _row(mix_g), w_in_bf16, _pad_cols(w_in[:, nmain:], LANES).astype(BF16), conv_w,
        alog_row, dtb_row, min(256, s),
        [e_gate.reshape(ne * d, dffe), e_up.reshape(ne * d, dffe), e_down.reshape(ne * dffe, d)])
    z = z.reshape(t, d)
    nbh = bsz * N_HEADS
    nck = s // CHUNK
    o_f, o_b = _gdn_scan(w.reshape(2, nbh, s, HEAD_DIM), kq.reshape(2, nbh, nck, 2 * CHUNK, HEAD_DIM),
                         kd.reshape(2, nbh, s, HEAD_DIM), p.reshape(2, nbh, s, HEAD_DIM),
                         egl.reshape(2, nbh, nck, 1, HEAD_DIM))
    cap = t + tm
    x2, pos, wcol, cnt, xs = _gdn_out(o_f.reshape(bsz, N_HEADS, s, HEAD_DIM), o_b.reshape(bsz, N_HEADS, s, HEAD_DIM),
                                      z, x3.reshape(t, d), _row(o_norm), w_out_bf16, _row(ffn_g),
                                      router.T, ts, tm, cap)
    te, tb, n_used = _tile_schedule(cnt[:, 0], tm, cap, 2 * t // tm + N_EXPERTS)
    tf = dffe // 2 if (dffe // 2) % 256 == 0 else dffe
    ys = _moe_grouped(xs, te, tb, n_used, experts[0].reshape(ne, d, dffe), experts[1].reshape(ne, d, dffe),
                      experts[2].reshape(ne, dffe, d), tm, tf, 256)
    out = _moe_combine(pos, x2, wcol, _row(final_g), ys, ts)
    return out.reshape(bsz, s, d)


def kernel(x, mix_norm, ffn_norm, cf_pw1_w, cf_pw1_b, cf_dw_w, cf_dw_b, cf_ln_g, cf_ln_b, cf_pw2_w, cf_pw2_b, ffn_w_gate, ffn_w_up, ffn_w_down, gdn_w_in, gdn_conv_w, gdn_a_log, gdn_dt_bias, gdn_o_norm, gdn_w_out, moe_router, moe_w_gate, moe_w_up, moe_w_down, final_norm):
    x, (w_in_bf16, w_out_bf16) = _conformer_layer(
        x, mix_norm[0], ffn_norm[0], cf_pw1_w[0], cf_pw1_b[0], cf_dw_w[0], cf_dw_b[0],
        cf_ln_g[0], cf_ln_b[0], cf_pw2_w[0], cf_pw2_b[0],
        ffn_w_gate[0], ffn_w_up[0], ffn_w_down[0], [gdn_w_in[0], gdn_w_out[0]])
    return _deltanet_moe_layer(x, mix_norm[1], ffn_norm[1], gdn_w_in[0], w_in_bf16, gdn_conv_w[0], gdn_a_log[0],
                               gdn_dt_bias[0], gdn_o_norm[0], w_out_bf16, moe_router[0],
                               moe_w_gate[0], moe_w_up[0], moe_w_down[0], final_norm)
```

```python
import functools

import jax
import jax.numpy as jnp
import numpy as np
from jax import lax
from jax.experimental import pallas as pl
from jax.experimental.pallas import tpu as pltpu

F32 = jnp.float32
BF16 = jnp.bfloat16

RMS_EPS = 1e-6
LN_EPS = 1e-5
L2_EPS = 1e-6
N_HEADS = 8
HEAD_DIM = 128
CHUNK = 64
N_EXPERTS = 8
LANES = 128
BF16_SUBLANES = 16
CONV_HALO = 16
SHORT_HALO = 8
VMEM_LIMIT_BYTES = 56 * 1024 * 1024


def _params(*sem):
    return pltpu.CompilerParams(dimension_semantics=sem, vmem_limit_bytes=VMEM_LIMIT_BYTES)


def _const_spec(shape):
    nd = len(shape)
    return pl.BlockSpec(shape, lambda *_: (0,) * nd, pipeline_mode=pl.Buffered(1))


def _cast_streams(arrays, grid):
    nsteps = grid[0] * grid[1]
    in_specs, out_specs, out_shapes = [], [], []
    for a in arrays:
        rows, cols = a.shape
        blk = next(r for r in range(BF16_SUBLANES, rows + 1, BF16_SUBLANES)
                   if rows % r == 0 and rows // r <= nsteps)
        nblk = rows // blk

        def imap(b, i, nblk=nblk):
            return (jnp.minimum(b * grid[1] + i, nblk - 1), 0)

        in_specs.append(pl.BlockSpec((blk, cols), imap))
        out_specs.append(pl.BlockSpec((blk, cols), imap))
        out_shapes.append(jax.ShapeDtypeStruct((rows, cols), BF16))
    return in_specs, out_specs, out_shapes


def _with_casts(body, n_in, n_out, n_cast):
    def wrapped(*refs):
        a, b, c = n_in + n_cast, n_in + n_cast + n_out, n_in + 2 * n_cast + n_out
        for src, dst in zip(refs[n_in:a], refs[b:c]):
            dst[...] = src[...].astype(BF16)
        body(*refs[:n_in], *refs[a:b], *refs[c:])
    return wrapped


def _rms(x, g):
    return x * lax.rsqrt(jnp.mean(x * x, axis=-1, keepdims=True) + RMS_EPS) * g


def _silu(x):
    return x * jax.nn.sigmoid(x)


def _mm(a, b):
    return jnp.dot(a.astype(BF16), b.astype(BF16), preferred_element_type=F32)


def _mm_nt(a, b):
    return lax.dot_general(a.astype(BF16), b.astype(BF16), (((1,), (1,)), ((), ())),
                           preferred_element_type=F32)


def _mm_tn(a, b):
    return lax.dot_general(a.astype(BF16), b.astype(BF16), (((0,), (0,)), ((), ())),
                           preferred_element_type=F32)


def _mm_exact(a, b):
    return jnp.dot(a, b, preferred_element_type=F32, precision=lax.Precision.HIGHEST)


def _conf_mixer_body(x_ref, xp_ref, xn_ref, g_ref, w1_ref, b1_ref, dw_ref, dwb_ref, lng_ref, lnb_ref,
                     w2_ref, b2_ref, o_ref, xe_ref, ext_ref, cv_ref, *, nc):
    s = pl.program_id(1)
    ns = pl.num_programs(1)
    ts, d = cv_ref.shape
    width = dw_ref.shape[0]
    base = CONV_HALO - width // 2
    span = ts + 8 * ((base + width - 1) // 8)
    xe_ref[0:CONV_HALO, :] = xp_ref[0]
    xe_ref[CONV_HALO:CONV_HALO + ts, :] = x_ref[0]
    xe_ref[CONV_HALO + ts:, :] = xn_ref[0]
    h = _rms(xe_ref[...], g_ref[...]).astype(BF16)
    for c in range(d // nc):
        a = _mm(h, w1_ref[:, c * nc:(c + 1) * nc]) + b1_ref[:, c * nc:(c + 1) * nc]
        b = _mm(h, w1_ref[:, d + c * nc:d + (c + 1) * nc]) + b1_ref[:, d + c * nc:d + (c + 1) * nc]
        ext_ref[0, :, c * nc:(c + 1) * nc] = a * jax.nn.sigmoid(b)

    @pl.when(s == 0)
    def _():
        ext_ref[0, 0:CONV_HALO, :] = jnp.zeros((CONV_HALO, d), F32)

    @pl.when(s == ns - 1)
    def _():
        ext_ref[0, CONV_HALO + ts:, :] = jnp.zeros((CONV_HALO, d), F32)

    for p in range(1, 8):
        ext_ref[p, 0:span, :] = ext_ref[0, pl.ds(p, span), :]
    rb = 128
    for c in range(d // LANES):
        cs = slice(c * LANES, (c + 1) * LANES)
        for r in range(ts // rb):
            acc = jnp.zeros((rb, LANES), F32)
            for k in range(width):
                off = base + k
                acc = acc + dw_ref[k:k + 1, cs] * ext_ref[off % 8, pl.ds(r * rb + 8 * (off // 8), rb), cs]
            cv_ref[r * rb:(r + 1) * rb, cs] = acc + dwb_ref[:, cs]
    y = cv_ref[...]
    mu = jnp.mean(y, axis=-1, keepdims=True)
    yc = y - mu
    yn = yc * lax.rsqrt(jnp.mean(yc * yc, axis=-1, keepdims=True) + LN_EPS) * lng_ref[...] + lnb_ref[...]
    o_ref[0] = x_ref[0] + _mm(_silu(yn), w2_ref[...]) + b2_ref[...]


def _conf_mixer(x3, g, w1, b1, dw_w, dw_b, ln_g, ln_b, w2, b2, ts, casts):
    bsz, s, d = x3.shape
    width = dw_w.shape[0]
    hb = ts // CONV_HALO
    nhb = s // CONV_HALO
    grid = (bsz, s // ts)
    c_in, c_out, c_shape = _cast_streams(casts, grid)
    res = pl.pallas_call(
        _with_casts(functools.partial(_conf_mixer_body, nc=256), 12, 1, len(casts)),
        grid=grid,
        in_specs=[pl.BlockSpec((1, ts, d), lambda b, i: (b, i, 0)),
                  pl.BlockSpec((1, CONV_HALO, d), lambda b, i: (b, jnp.maximum(i * hb - 1, 0), 0)),
                  pl.BlockSpec((1, CONV_HALO, d), lambda b, i: (b, jnp.minimum((i + 1) * hb, nhb - 1), 0)),
                  _const_spec((1, d)), _const_spec((d, 2 * d)), _const_spec((1, 2 * d)),
                  _const_spec((width, d)), _const_spec((1, d)), _const_spec((1, d)), _const_spec((1, d)),
                  _const_spec((d, d)), _const_spec((1, d))] + c_in,
        out_specs=[pl.BlockSpec((1, ts, d), lambda b, i: (b, i, 0))] + c_out,
        out_shape=[jax.ShapeDtypeStruct((bsz, s, d), F32)] + c_shape,
        scratch_shapes=[pltpu.VMEM((ts + 2 * CONV_HALO, d), F32),
                        pltpu.VMEM((8, ts + 2 * CONV_HALO, d), F32), pltpu.VMEM((ts, d), F32)],
        compiler_params=_params("parallel", "parallel"),
        name="conf_mixer",
    )(x3, x3, x3, g, w1, b1, dw_w, dw_b, ln_g, ln_b, w2, b2, *casts)
    return res[0], res[1:]


def _ffn_body(x_ref, g_ref, wg_ref, wu_ref, wd_ref, o_ref, *, fc):
    x = x_ref[...]
    h = _rms(x, g_ref[...]).astype(BF16)
    dff = wg_ref.shape[1]
    acc = x
    for c in range(dff // fc):
        cs = slice(c * fc, (c + 1) * fc)
        a = _silu(_mm(h, wg_ref[:, cs])) * _mm(h, wu_ref[:, cs])
        acc = acc + _mm(a, wd_ref[cs, :])
    o_ref[...] = acc


def _ffn(x2d, g, wg, wu, wd, tm, fc):
    t, d = x2d.shape
    dff = wg.shape[1]
    return pl.pallas_call(
        functools.partial(_ffn_body, fc=fc),
        grid=(t // tm,),
        in_specs=[pl.BlockSpec((tm, d), lambda i: (i, 0)), _const_spec((1, d)),
                  _const_spec((d, dff)), _const_spec((d, dff)), _const_spec((dff, d))],
        out_specs=pl.BlockSpec((tm, d), lambda i: (i, 0)),
        out_shape=jax.ShapeDtypeStruct((t, d), F32),
        compiler_params=_params("parallel"),
        name="dense_swiglu",
    )(x2d, g, wg, wu, wd)


PAIR = 2 * CHUNK


def _block_diag(yb, mk_ref):
    return jnp.concatenate([yb * mk_ref[MASK_LEFT].astype(BF16), yb * mk_ref[MASK_RIGHT].astype(BF16)], axis=0)


def _tri_inverse_pairs(mats, mk_ref):
    def bd(vals):
        return [_block_diag(v.astype(BF16), mk_ref) for v in vals]

    l0 = [a * mk_ref[MASK_BLOCK8] for a in mats]
    l0b = [a.astype(BF16) for a in l0]
    l2 = [_mm(a, d) for a, d in zip(l0b, bd(l0b))]
    l2b = [a.astype(BF16) for a in l2]
    l2d = bd(l2b)
    l4 = [_mm(a, d) for a, d in zip(l2b, l2d)]
    l3 = [_mm(a, d) for a, d in zip(l0b, l2d)]
    xs = [mk_ref[MASK_EYE] - a + b - t for a, b, t in zip(l0, l2, l3)]
    x4 = [_mm(x, d) for x, d in zip(xs, bd(l4))]
    xs = [x + t for x, t in zip(xs, x4)]
    for level in range(MASK_MERGE0, MASK_MERGE0 + 3):
        xb = [x.astype(BF16) for x in xs]
        t1 = [_mm(a * mk_ref[level], d) for a, d in zip(mats, bd(xb))]
        t2 = [_mm(x, d) for x, d in zip(xb, bd(t1))]
        xs = [x - t for x, t in zip(xs, t2)]
    return xs


(MASK_INCL, MASK_INCL_L, MASK_INCL_R, MASK_STRICT_L, MASK_STRICT_R) = (0, 2, 4, 6, 8)
MASK_EYE, MASK_BLOCK8, MASK_MERGE0, MASK_LEFT, MASK_RIGHT = 10, 11, 12, 15, 16
N_MASKS = 17


def _chunk_masks():
    ri, li = np.indices((CHUNK, PAIR))
    ci = li % CHUNK
    left, right = li < CHUNK, li >= CHUNK
    incl = [ci <= ri, ci >= ri]
    strict = [ci < ri, ci > ri]
    masks = (incl + [m & left for m in incl] + [m & right for m in incl]
             + [m & left for m in strict] + [m & right for m in strict]
             + [ci == ri, (ri // 8) == (ci // 8)])
    size = 8
    while size < CHUNK:
        masks.append(((ri // (2 * size)) == (ci // (2 * size))) & ((ri // size) != (ci // size)))
        size *= 2
    masks += [left, right]
    assert len(masks) == N_MASKS
    return np.stack(masks).astype(np.float32)


GDN_HEAD_GROUP = 4


def _gdn_front_body(x_ref, xp_ref, xn_ref, g_ref, w_ref, wgate_ref, cw_ref, alog_ref, dtb_ref, mk_ref,
                    z_ref, w_out_ref, kq_ref, kd_ref, p_ref, egl_ref, xe_ref, pe_ref):
    s = pl.program_id(1)
    ns = pl.num_programs(1)
    tg, d = z_ref.shape[1:]
    kw = cw_ref.shape[0]
    dk = N_HEADS * HEAD_DIM
    npair = tg // PAIR
    gw = GDN_HEAD_GROUP * HEAD_DIM
    ngroup = N_HEADS // GDN_HEAD_GROUP
    base = SHORT_HALO - kw // 2

    xe_ref[0:SHORT_HALO, :] = jnp.where(s > 0, xp_ref[0], 0.0)
    xe_ref[SHORT_HALO:SHORT_HALO + tg, :] = x_ref[0]
    xe_ref[SHORT_HALO + tg:, :] = jnp.where(s < ns - 1, xn_ref[0], 0.0)
    h = _rms(xe_ref[...], g_ref[...]).astype(BF16)
    hm = h[SHORT_HALO:SHORT_HALO + tg]

    gates = _mm(hm, wgate_ref[...])
    beta = jax.nn.sigmoid(gates)
    xa = gates + dtb_ref[...]
    softplus = jnp.maximum(xa, 0.0) + jnp.log1p(jnp.exp(-jnp.abs(xa)))
    log_a = -jnp.exp(alog_ref[...]) * softplus
    ri = lax.broadcasted_iota(jnp.int32, (tg, tg), 0)
    ci = lax.broadcasted_iota(jnp.int32, (tg, tg), 1)
    same = (ri // CHUNK) == (ci // CHUNK)
    cum_f = _mm_exact((same & (ci <= ri)).astype(F32), log_a)
    cum_b = _mm_exact((same & (ci >= ri)).astype(F32), log_a)
    lane = lax.broadcasted_iota(jnp.int32, (tg, LANES), 1)
    g_all = jnp.where(lane < 3 * N_HEADS, cum_f, cum_b)
    g_all_t = g_all.T
    beta_t = beta.T

    def project(grp):
        out = []
        for part in range(3):
            col0 = part * dk + grp * gw
            pe_ref[grp % 2, part] = _mm(h, w_ref[:, col0:col0 + gw])
            heads = []
            for c in range(GDN_HEAD_GROUP):
                col = col0 + c * LANES
                acc = jnp.zeros((tg, LANES), F32)
                for k in range(kw):
                    acc = acc + (cw_ref[k:k + 1, col:col + LANES]
                                 * pe_ref[grp % 2, part, pl.ds(base + k, tg), c * LANES:(c + 1) * LANES])
                a = _silu(acc)
                if part < 2:
                    a = a * lax.rsqrt(jnp.sum(a * a, axis=-1, keepdims=True) + L2_EPS)
                if part == 0:
                    a = a * (HEAD_DIM ** -0.5)
                heads.append(a)
            out.append(heads)
        return out

    def chunk_systems(grp, qkv):
        heads = range(grp * GDN_HEAD_GROUP, (grp + 1) * GDN_HEAD_GROUP)
        qs, ks, vs = (dict(zip(heads, part)) for part in qkv)
        raw, v16 = {}, {}
        for hd in heads:
            k16 = ks[hd].astype(BF16)
            q16 = qs[hd].astype(BF16)
            v16[hd] = vs[hd].astype(BF16)
            for m in range(npair):
                pr = slice(m * PAIR, (m + 1) * PAIR)
                raw[hd, m] = _mm_nt(jnp.concatenate([k16[pr], q16[pr]], axis=0), k16[pr])
        insts = [(hd, dr, m) for hd in heads for dr in range(2) for m in range(npair)]
        zero = jnp.zeros((CHUNK, HEAD_DIM), BF16)
        lows, rhss, beta_rows = [], [], []
        for hd, dr, m in insts:
            pr = slice(m * PAIR, (m + 1) * PAIR)
            bcol = dr * N_HEADS + hd
            gcol = 2 * N_HEADS + dr * N_HEADS + hd
            bt = beta[pr, bcol:bcol + 1]
            beta_rows.append(beta_t[bcol:bcol + 1, pr])
            g_c = g_all[pr, gcol:gcol + 1]
            g_r = g_all_t[gcol:gcol + 1, pr]
            diff = g_c - g_r
            decay = jnp.exp(diff[:CHUNK] * mk_ref[MASK_INCL_L + dr] + diff[CHUNK:] * mk_ref[MASK_INCL_R + dr])
            decay = decay * mk_ref[MASK_INCL + dr]
            kk = raw[hd, m][:PAIR] * bt
            qk = raw[hd, m][PAIR:]
            lows.append((kk[:CHUNK] * mk_ref[MASK_STRICT_L + dr] + kk[CHUNK:] * mk_ref[MASK_STRICT_R + dr]) * decay)
            pq = (qk[:CHUNK] * mk_ref[MASK_LEFT] + qk[CHUNK:] * mk_ref[MASK_RIGHT]) * decay
            half = mk_ref[MASK_LEFT + dr]
            p_tiles = [(t * half).astype(BF16) for t in (pq, pltpu.roll(pq, CHUNK, axis=1))]
            eg = jnp.exp(g_c)
            keg = (ks[hd][pr] * eg).astype(BF16)
            qeg = (qs[hd][pr] * eg).astype(BF16)
            vp = v16[hd][pr]
            rhss.append(jnp.concatenate(
                [jnp.concatenate([vp[:CHUNK], keg[:CHUNK], zero, zero], axis=1),
                 jnp.concatenate([zero, zero, vp[CHUNK:], keg[CHUNK:]], axis=1)], axis=0))
            for j in range(2):
                n = 2 * m + j
                rs = slice(n * CHUNK, (n + 1) * CHUNK)
                js = slice(j * CHUNK, (j + 1) * CHUNK)
                last = n * CHUNK + (CHUNK - 1 if dr == 0 else 0)
                g_l = g_all[last:last + 1, gcol:gcol + 1]
                p_ref[dr, 0, hd, rs, :] = p_tiles[(j + dr) % 2]
                kq_ref[dr, 0, hd, n, CHUNK:, :] = qeg[js]
                kd_ref[dr, 0, hd, rs, :] = (ks[hd][rs] * jnp.exp(g_l - g_all[rs, gcol:gcol + 1])).astype(BF16)
                egl_ref[dr, 0, hd, n] = jnp.broadcast_to(jnp.exp(g_l), (1, HEAD_DIM))
        invs = _tri_inverse_pairs(lows, mk_ref)
        sols = [_mm(inv * br, rhs) for inv, br, rhs in zip(invs, beta_rows, rhss)]
        for (hd, dr, m), sol in zip(insts, sols):
            for j in range(2):
                n = 2 * m + j
                rs = slice(n * CHUNK, (n + 1) * CHUNK)
                w_out_ref[dr, 0, hd, rs, :] = sol[:, 2 * j * HEAD_DIM:(2 * j + 1) * HEAD_DIM]
                kq_ref[dr, 0, hd, n, :CHUNK, :] = sol[:, (2 * j + 1) * HEAD_DIM:(2 * j + 2) * HEAD_DIM].astype(BF16)

    qkv = project(0)
    for grp in range(ngroup):
        nxt = project(grp + 1) if grp + 1 < ngroup else None
        if nxt is None:
            z_ref[0] = _mm(hm, w_ref[:, 3 * dk:3 * dk + d])
        chunk_systems(grp, qkv)
        qkv = nxt


def _gdn_front(x3, g, w_in, w_gate, conv_w, alog_row, dtb_row, tg, casts):
    bsz, s, d = x3.shape
    nin = w_in.shape[1]
    kw, nq = conv_w.shape
    nchunk = tg // CHUNK
    hb = tg // SHORT_HALO
    nhb = s // SHORT_HALO
    gw = GDN_HEAD_GROUP * HEAD_DIM

    def rows(dtype, last=HEAD_DIM):
        return (jax.ShapeDtypeStruct((2, bsz, N_HEADS, s, last), dtype),
                pl.BlockSpec((2, 1, N_HEADS, tg, last), lambda b, i: (0, b, 0, i, 0)))

    def per_chunk(dtype, r):
        return (jax.ShapeDtypeStruct((2, bsz, N_HEADS, s // CHUNK, r, HEAD_DIM), dtype),
                pl.BlockSpec((2, 1, N_HEADS, nchunk, r, HEAD_DIM), lambda b, i: (0, b, 0, i, 0, 0)))

    outs = [(jax.ShapeDtypeStruct((bsz, s, d), F32), pl.BlockSpec((1, tg, d), lambda b, i: (b, i, 0))),
            rows(F32), per_chunk(BF16, 2 * CHUNK), rows(BF16), rows(BF16), per_chunk(F32, 1)]
    grid = (bsz, s // tg)
    c_in, c_out, c_shape = _cast_streams(casts, grid)
    res = pl.pallas_call(
        _with_casts(_gdn_front_body, 10, len(outs), len(casts)),
        grid=grid,
        in_specs=[pl.BlockSpec((1, tg, d), lambda b, i: (b, i, 0)),
                  pl.BlockSpec((1, SHORT_HALO, d), lambda b, i: (b, jnp.maximum(i * hb - 1, 0), 0)),
                  pl.BlockSpec((1, SHORT_HALO, d), lambda b, i: (b, jnp.minimum((i + 1) * hb, nhb - 1), 0)),
                  _const_spec((1, d)), _const_spec((d, nin)), _const_spec((d, LANES)),
                  _const_spec((kw, nq)), _const_spec((1, LANES)), _const_spec((1, LANES)),
                  _const_spec((N_MASKS, CHUNK, PAIR))] + c_in,
        out_specs=[o[1] for o in outs] + c_out,
        out_shape=[o[0] for o in outs] + c_shape,
        scratch_shapes=[pltpu.VMEM((tg + 2 * SHORT_HALO, d), F32),
                        pltpu.VMEM((2, 3, tg + 2 * SHORT_HALO, gw), F32)],
        compiler_params=_params("parallel", "parallel"),
        name="gdn_front",
    )(x3, x3, x3, g, w_in, w_gate, conv_w, alog_row, dtb_row, _chunk_masks(), *casts)
    return res[:len(outs)], res[len(outs):]


GDN_SCAN_GROUP = 16


def _gdn_scan_body(wf, kqf, kdf, pf, eglf, wb, kqb, kdb, pb, eglb, of_ref, ob_ref, state_ref):
    n = pl.program_id(0)

    @pl.when(n == 0)
    def _():
        state_ref[...] = jnp.zeros_like(state_ref)

    nbh = state_ref.shape[1]
    dirs = ((wf, kqf, kdf, pf, eglf, of_ref), (wb, kqb, kdb, pb, eglb, ob_ref))

    def body(i, carry):
        bhs = [i * GDN_SCAN_GROUP + j for j in range(GDN_SCAN_GROUP)]
        zero = jnp.zeros((CHUNK, HEAD_DIM), BF16)
        st = [[state_ref[dr, b] for dr in range(2)] for b in bhs]
        r = [[_mm(dirs[dr][1][0, b, 0], s_[dr]) for dr in range(2)] for b, s_ in zip(bhs, st)]
        ub = [[(dirs[dr][0][0, b] - r_[dr][:CHUNK]).astype(BF16) for dr in range(2)] for b, r_ in zip(bhs, r)]
        ud = [jnp.concatenate([jnp.concatenate([u_[0], zero], axis=1), jnp.concatenate([zero, u_[1]], axis=1)],
                              axis=0) for u_ in ub]
        pu = [_mm(dirs[0][3][0, b] + dirs[1][3][0, b], d_) for b, d_ in zip(bhs, ud)]
        ku = [_mm_tn(jnp.concatenate([dirs[0][2][0, b], dirs[1][2][0, b]], axis=0), d_) for b, d_ in zip(bhs, ud)]
        for b, r_, pu_, ku_, s_ in zip(bhs, r, pu, ku, st):
            for dr in range(2):
                cs = slice(dr * HEAD_DIM, (dr + 1) * HEAD_DIM)
                dirs[dr][5][b] = r_[dr][CHUNK:] + pu_[:, cs]
                state_ref[dr, b] = s_[dr] * dirs[dr][4][0, b, 0] + ku_[:, cs]
        return carry

    lax.fori_loop(0, nbh // GDN_SCAN_GROUP, body, 0)


def _gdn_scan(w, kq, kd, p, egl):
    _, nbh, s, dh = w.shape
    nchunk = s // CHUNK

    def specs(dr):
        def im(n):
            return (dr, 0, n if dr == 0 else nchunk - 1 - n, 0)

        def im5(n):
            return im(n) + (0,)
        return [pl.BlockSpec((1, nbh, CHUNK, dh), im), pl.BlockSpec((1, nbh, 1, 2 * CHUNK, dh), im5),
                pl.BlockSpec((1, nbh, CHUNK, dh), im), pl.BlockSpec((1, nbh, CHUNK, dh), im),
                pl.BlockSpec((1, nbh, 1, 1, dh), im5)]

    o_shape = jax.ShapeDtypeStruct((nbh, s, dh), F32)
    return pl.pallas_call(
        _gdn_scan_body,
        grid=(nchunk,),
        in_specs=specs(0) + specs(1),
        out_specs=[pl.BlockSpec((nbh, CHUNK, dh), lambda n: (0, n, 0)),
                   pl.BlockSpec((nbh, CHUNK, dh), lambda n: (0, nchunk - 1 - n, 0))],
        out_shape=[o_shape, o_shape],
        scratch_shapes=[pltpu.VMEM((2, nbh, dh, dh), F32)],
        compiler_params=_params("arbitrary"),
        name="gdn_scan",
    )(w, kq, kd, p, egl, w, kq, kd, p, egl)


TOKEN_TILE_ROWS = 8
DMA_ISSUE_UNROLL = 8


def _to_token_tiles(dst_ref, val):
    n = val.shape[0]
    for j in range(TOKEN_TILE_ROWS):
        dst_ref[pl.ds(j, n, stride=TOKEN_TILE_ROWS), :] = val[:, j * LANES:(j + 1) * LANES]


def _from_token_tiles(src_ref, n):
    return jnp.concatenate([src_ref[pl.ds(j, n, stride=TOKEN_TILE_ROWS), :] for j in range(TOKEN_TILE_ROWS)],
                           axis=1)


def _gdn_out_body(of_ref, ob_ref, z_ref, x_ref, on_ref, wo_ref, fg_ref, rwt_ref,
                  xo_ref, pos_ref, wcol_ref, cnt_ref, xs_ref,
                  act_ref, tile_ref, zero_ref, carry_ref, pos_vmem, pos_smem, cnt_smem, sem, psem, zsem,
                  *, cap, tm):
    g = pl.program_id(0)
    ng = pl.num_programs(0)
    ts = x_ref.shape[0]
    slot = g % 2
    prev = 1 - slot
    rpt = TOKEN_TILE_ROWS

    def wait_rows(s):
        for _ in range(2):
            pltpu.make_async_copy(tile_ref.at[s], xs_ref.at[pl.ds(0, ts * rpt)], sem.at[s]).wait()

    def dispatch_row(t, s):
        src = tile_ref.at[s, pl.ds(t * rpt, rpt)]
        for k in range(2):
            dst = xs_ref.at[pl.ds(pos_smem[s, k, t] * rpt, rpt)]
            pltpu.make_async_copy(src, dst, sem.at[s]).start()

    def positions_to_smem(s):
        return pltpu.make_async_copy(pos_vmem.at[s], pos_smem.at[s], psem)

    def tile_step(dispatch_prev):
        if dispatch_prev:
            positions_to_smem(prev).wait()
            for t in range(ts):
                dispatch_row(t, prev)
        for h in range(N_HEADS):
            hs = slice(h * HEAD_DIM, (h + 1) * HEAD_DIM)
            o = of_ref[0, h] + ob_ref[0, h]
            o = o * lax.rsqrt(jnp.mean(o * o, axis=-1, keepdims=True) + RMS_EPS) * on_ref[...]
            act_ref[:, hs] = (o * _silu(z_ref[:, hs])).astype(BF16)
        x = x_ref[...] + _mm(act_ref[...], wo_ref[...])
        xo_ref[...] = x
        hn = _rms(x, fg_ref[...])

        logits = lax.dot_general(rwt_ref[...], hn, (((1,), (1,)), ((), ())),
                                 preferred_element_type=F32, precision=lax.Precision.HIGHEST)
        eidx = lax.broadcasted_iota(jnp.int32, logits.shape, 0).astype(F32)
        neg = jnp.float32(-jnp.inf)
        m1 = jnp.max(logits, axis=0, keepdims=True)
        i1 = jnp.min(jnp.where(logits == m1, eidx, float(N_EXPERTS)), axis=0, keepdims=True)
        one1 = eidx == i1
        rest = jnp.where(one1, neg, logits)
        m2 = jnp.max(rest, axis=0, keepdims=True)
        i2 = jnp.min(jnp.where(rest == m2, eidx, float(N_EXPERTS)), axis=0, keepdims=True)
        one2 = eidx == i2
        e2 = jnp.exp(m2 - m1)
        w1 = 1.0 / (1.0 + e2)
        w2 = e2 * w1

        chosen = jnp.where(one1 | one2, 1.0, 0.0)
        ri = lax.broadcasted_iota(jnp.int32, (ts, ts), 0)
        ci = lax.broadcasted_iota(jnp.int32, (ts, ts), 1)
        before = jnp.where(ri < ci, 1.0, 0.0).astype(BF16)
        rank = jnp.dot(chosen.astype(BF16), before, preferred_element_type=F32)
        carry = carry_ref[...]
        slot_f = eidx * float(cap) + carry[:, 0:1] + rank
        p1 = jnp.sum(jnp.where(one1, slot_f, 0.0), axis=0, keepdims=True)
        p2 = jnp.sum(jnp.where(one2, slot_f, 0.0), axis=0, keepdims=True)
        carry = carry + jnp.sum(chosen, axis=1, keepdims=True)
        carry_ref[...] = carry
        cnt_ref[...] = carry.astype(jnp.int32)
        row8 = lax.broadcasted_iota(jnp.int32, (N_EXPERTS, ts), 0)
        pos = jnp.where(row8 == 0, p1, jnp.where(row8 == 1, p2, 0.0)).astype(jnp.int32)
        pos_ref[...] = pos
        pos_vmem[slot] = pos
        row128 = lax.broadcasted_iota(jnp.int32, (LANES, ts), 0)
        wcol_ref[...] = jnp.where(row128 == 0, w1, jnp.where(row128 == 1, w2, 0.0)).T

        _to_token_tiles(tile_ref.at[slot], hn)
        if dispatch_prev:
            wait_rows(prev)
        positions_to_smem(slot).start()

    @pl.when(g == 0)
    def _():
        carry_ref[...] = jnp.zeros_like(carry_ref)
        tile_step(False)

    @pl.when(g > 0)
    def _():
        tile_step(True)

    @pl.when(g == ng - 1)
    def _():
        positions_to_smem(slot).wait()
        lax.fori_loop(0, ts, lambda t, c: (dispatch_row(t, slot), c)[1], 0, unroll=DMA_ISSUE_UNROLL)
        wait_rows(slot)
        zero_ref[...] = jnp.zeros_like(zero_ref)
        cc = pltpu.make_async_copy(cnt_ref, cnt_smem, psem)
        cc.start()
        cc.wait()
        tails = [pltpu.make_async_copy(
            zero_ref, xs_ref.at[pl.ds((e * cap + cnt_smem[e, 0]) * rpt, tm * rpt)], zsem)
            for e in range(N_EXPERTS)]
        for c in tails:
            c.start()
        for c in tails:
            c.wait()


def _gdn_out(o_f, o_b, z2, x2, o_norm, w_out, ffn_g, router_t, ts, tm, cap):
    t, d = x2.shape
    s = o_f.shape[2]
    spb = s // ts
    rpt = TOKEN_TILE_ROWS
    o_spec = pl.BlockSpec((1, N_HEADS, ts, HEAD_DIM), lambda g: (g // spb, 0, g % spb, 0))
    return pl.pallas_call(
        functools.partial(_gdn_out_body, cap=cap, tm=tm),
        grid=(t // ts,),
        in_specs=[o_spec, o_spec,
                  pl.BlockSpec((ts, d), lambda g: (g, 0)), pl.BlockSpec((ts, d), lambda g: (g, 0)),
                  _const_spec((1, HEAD_DIM)), _const_spec((d, d)), _const_spec((1, d)),
                  _const_spec((N_EXPERTS, d))],
        out_specs=[pl.BlockSpec((ts, d), lambda g: (g, 0)),
                   pl.BlockSpec((N_EXPERTS, ts), lambda g: (0, g)),
                   pl.BlockSpec((ts, LANES), lambda g: (g, 0)),
                   pl.BlockSpec((N_EXPERTS, LANES), lambda g: (0, 0)),
                   pl.BlockSpec(memory_space=pl.ANY)],
        out_shape=[jax.ShapeDtypeStruct((t, d), F32),
                   jax.ShapeDtypeStruct((N_EXPERTS, t), jnp.int32),
                   jax.ShapeDtypeStruct((t, LANES), F32),
                   jax.ShapeDtypeStruct((N_EXPERTS, LANES), jnp.int32),
                   jax.ShapeDtypeStruct((N_EXPERTS * cap * rpt, LANES), F32)],
        scratch_shapes=[pltpu.VMEM((ts, d), BF16),
                        pltpu.VMEM((2, ts * rpt, LANES), F32),
                        pltpu.VMEM((tm * rpt, LANES), F32),
                        pltpu.VMEM((N_EXPERTS, LANES), F32),
                        pltpu.VMEM((2, N_EXPERTS, ts), jnp.int32),
                        pltpu.SMEM((2, N_EXPERTS, ts), jnp.int32),
                        pltpu.SMEM((N_EXPERTS, LANES), jnp.int32),
                        pltpu.SemaphoreType.DMA((2,)), pltpu.SemaphoreType.DMA, pltpu.SemaphoreType.DMA],
        compiler_params=_params("arbitrary"),
        name="gdn_out_router",
    )(o_f, o_b, z2, x2, o_norm, w_out, ffn_g, router_t)


def _moe_body(te_ref, tb_ref, nu_ref, xs_ref, wg_ref, wu_ref, wd_ref, ys_ref, xb_ref, acc_ref, *, fc):
    i = pl.program_id(0)
    f = pl.program_id(1)
    tm = xb_ref.shape[0]
    tf = wg_ref.shape[2]

    @pl.when((i == 0) & (f == 0))
    def _():
        acc_ref[...] = jnp.zeros_like(acc_ref)

    @pl.when(i < nu_ref[0])
    def _():
        @pl.when(f == 0)
        def _():
            xb_ref[...] = _from_token_tiles(xs_ref, tm).astype(BF16)

        x = xb_ref[...]
        for c in range(tf // fc):
            cs = slice(c * fc, (c + 1) * fc)
            a = _silu(_mm(x, wg_ref[0, :, cs])) * _mm(x, wu_ref[0, :, cs])
            y = _mm(a, wd_ref[0, cs, :])
            if c == 0:
                acc_ref[...] = jnp.where(f > 0, acc_ref[...], 0.0) + y
            else:
                acc_ref[...] += y

        @pl.when(f == pl.num_programs(1) - 1)
        def _():
            _to_token_tiles(ys_ref, acc_ref[...])


def _moe_grouped(xs, tile_expert, tile_block, n_used, wg, wu, wd, tm, tf, fc):
    ne, d, dff = wg.shape
    nt = tile_expert.shape[0]
    nf = dff // tf
    rpt = TOKEN_TILE_ROWS

    def fsel(i, f, nu):
        return jnp.where(i < nu[0], f, nf - 1)

    grid_spec = pltpu.PrefetchScalarGridSpec(
        num_scalar_prefetch=3,
        grid=(nt, nf),
        in_specs=[pl.BlockSpec((tm * rpt, LANES), lambda i, f, te, tb, nu: (tb[i], 0)),
                  pl.BlockSpec((1, d, tf), lambda i, f, te, tb, nu: (te[i], 0, fsel(i, f, nu))),
                  pl.BlockSpec((1, d, tf), lambda i, f, te, tb, nu: (te[i], 0, fsel(i, f, nu))),
                  pl.BlockSpec((1, tf, d), lambda i, f, te, tb, nu: (te[i], fsel(i, f, nu), 0))],
        out_specs=pl.BlockSpec((tm * rpt, LANES), lambda i, f, te, tb, nu: (tb[i], 0)),
        scratch_shapes=[pltpu.VMEM((tm, d), BF16), pltpu.VMEM((tm, d), F32)],
    )
    return pl.pallas_call(
        functools.partial(_moe_body, fc=fc),
        grid_spec=grid_spec,
        out_shape=jax.ShapeDtypeStruct(xs.shape, F32),
        compiler_params=_params("arbitrary", "arbitrary"),
        name="moe_grouped",
    )(tile_expert, tile_block, n_used, xs, wg, wu, wd)


COMBINE_DEPTH = 3


def _combine_body(pos0_ref, pos1_ref, pos2_ref, x_ref, wcol_ref, fn_ref, ys_ref, o_ref, gbuf, pos_smem, sem, psem):
    i = pl.program_id(0)
    n = pl.num_programs(0)
    tc = x_ref.shape[0]
    rpt = TOKEN_TILE_ROWS

    def load_positions(p_ref):
        cp = pltpu.make_async_copy(p_ref, pos_smem, psem)
        cp.start()
        cp.wait()

    def issue(t, sl):
        for k in range(2):
            src = ys_ref.at[pl.ds(pos_smem[k, t] * rpt, rpt)]
            pltpu.make_async_copy(src, gbuf.at[sl, k, pl.ds(t * rpt, rpt)], sem.at[sl]).start()

    def wait_rows(sl):
        for k in range(2):
            pltpu.make_async_copy(ys_ref.at[pl.ds(0, tc * rpt)], gbuf.at[sl, k], sem.at[sl]).wait()

    @pl.when(i == 0)
    def _():
        for sl, p_ref in enumerate((pos0_ref, pos1_ref)):
            load_positions(p_ref)
            lax.fori_loop(0, tc, lambda t, c, sl=sl: (issue(t, sl), c)[1], 0, unroll=DMA_ISSUE_UNROLL)

    for sl in range(COMBINE_DEPTH):
        @pl.when(i % COMBINE_DEPTH == sl)
        def _(sl=sl):
            load_positions(pos2_ref)
            wait_rows(sl)
            for t in range(tc):
                issue(t, (sl + 2) % COMBINE_DEPTH)
            wcol = wcol_ref[...]
            y = (x_ref[...] + wcol[:, 0:1] * _from_token_tiles(gbuf.at[sl, 0], tc)
                 + wcol[:, 1:2] * _from_token_tiles(gbuf.at[sl, 1], tc))
            o_ref[...] = _rms(y, fn_ref[...])

            @pl.when(i == n - 1)
            def _():
                wait_rows((sl + 1) % COMBINE_DEPTH)
                wait_rows((sl + 2) % COMBINE_DEPTH)


def _moe_combine(pos, x2, wcol, final_g, ys, tc):
    t, d = x2.shape
    n = t // tc
    rpt = TOKEN_TILE_ROWS

    def pos_spec(ahead):
        return pl.BlockSpec((N_EXPERTS, tc), lambda i: (0, jnp.minimum(i + ahead, n - 1)))

    return pl.pallas_call(
        _combine_body,
        grid=(n,),
        in_specs=[pos_spec(0), pos_spec(1), pos_spec(2),
                  pl.BlockSpec((tc, d), lambda i: (i, 0)),
                  pl.BlockSpec((tc, LANES), lambda i: (i, 0)),
                  _const_spec((1, d)),
                  pl.BlockSpec(memory_space=pl.ANY)],
        out_specs=pl.BlockSpec((tc, d), lambda i: (i, 0)),
        out_shape=jax.ShapeDtypeStruct((t, d), F32),
        scratch_shapes=[pltpu.VMEM((COMBINE_DEPTH, 2, tc * rpt, LANES), F32),
                        pltpu.SMEM((N_EXPERTS, tc), jnp.int32),
                        pltpu.SemaphoreType.DMA((COMBINE_DEPTH,)), pltpu.SemaphoreType.DMA],
        compiler_params=_params("arbitrary"),
        name="moe_combine_norm",
    )(pos, pos, pos, x2, wcol, final_g, ys)


def _tile_schedule(counts, tm, cap, nt):
    ntile = (counts + tm - 1) // tm
    ends = jnp.cumsum(ntile)
    n_used = ends[-1]
    i = jnp.minimum(jnp.arange(nt, dtype=jnp.int32), jnp.maximum(n_used - 1, 0))
    te = jnp.sum((i[:, None] >= ends[None, :]).astype(jnp.int32), axis=1)
    tb = te * (cap // tm) + i - (ends - ntile)[te]
    return te.astype(jnp.int32), tb.astype(jnp.int32), n_used.reshape(1).astype(jnp.int32)


def _row(v):
    return v.reshape(1, -1).astype(F32)


def _pad_cols(w, n):
    return jnp.pad(w, ((0, 0), (0, n - w.shape[1])))


def _conformer_layer(x3, mix_g, ffn_g, pw1_w, pw1_b, dw_w, dw_b, ln_g, ln_b, pw2_w, pw2_b,
                     w_gate, w_up, w_down, later_weights):
    bsz, s, d = x3.shape
    t = bsz * s
    tm = min(512, t)
    ts = min(512, s)
    x3, cast = _conf_mixer(x3, _row(mix_g), pw1_w.astype(BF16), _row(pw1_b), dw_w, _row(dw_b), _row(ln_g),
                           _row(ln_b), pw2_w.astype(BF16), _row(pw2_b), ts, [w_gate, w_up, w_down] + later_weights)
    x2 = _ffn(x3.reshape(t, d), _row(ffn_g), cast[0], cast[1], cast[2], tm, 256)
    return x2.reshape(bsz, s, d), cast[3:]


def _deltanet_moe_layer(x3, mix_g, ffn_g, w_in_bf16, conv_w, a_log, dt_bias, o_norm, w_out_bf16,
                        router, e_gate, e_up, e_down, final_g):
    bsz, s, d = x3.shape
    t = bsz * s
    tm = min(512, t)
    ts = min(512, s)
    nmain = 4 * d
    ne, _, dffe = e_gate.shape
    zero16 = jnp.zeros((2 * N_HEADS,), F32)
    alog_row = _row(_pad_cols(jnp.concatenate([zero16, a_log.reshape(-1)])[None], LANES))
    dtb_row = _row(_pad_cols(jnp.concatenate([zero16, dt_bias.reshape(-1)])[None], LANES))
    (z, w, kq, kd, p, egl), experts = _gdn_front(
        x3, _row(mix_g), w_in_bf16, _pad_cols(w_in_bf16[:, nmain:], LANES), conv_w,
        alog_row, dtb_row, min(256, s),
        [e_gate.reshape(ne * d, dffe), e_up.reshape(ne * d, dffe), e_down.reshape(ne * dffe, d)])
    z = z.reshape(t, d)
    nbh = bsz * N_HEADS
    nck = s // CHUNK
    o_f, o_b = _gdn_scan(w.reshape(2, nbh, s, HEAD_DIM), kq.reshape(2, nbh, nck, 2 * CHUNK, HEAD_DIM),
                         kd.reshape(2, nbh, s, HEAD_DIM), p.reshape(2, nbh, s, HEAD_DIM),
                         egl.reshape(2, nbh, nck, 1, HEAD_DIM))
    cap = t + tm
    x2, pos, wcol, cnt, xs = _gdn_out(o_f.reshape(bsz, N_HEADS, s, HEAD_DIM), o_b.reshape(bsz, N_HEADS, s, HEAD_DIM),
                                      z, x3.reshape(t, d), _row(o_norm), w_out_bf16, _row(ffn_g),
                                      router.T, ts, tm, cap)
    te, tb, n_used = _tile_schedule(cnt[:, 0], tm, cap, 2 * t // tm + N_EXPERTS)
    tf = dffe // 2 if (dffe // 2) % 256 == 0 else dffe
    ys = _moe_grouped(xs, te, tb, n_used, experts[0].reshape(ne, d, dffe), experts[1].reshape(ne, d, dffe),
                      experts[2].reshape(ne, dffe, d), tm, tf, 256)
    out = _moe_combine(pos, x2, wcol, _row(final_g), ys, ts)
    return out.reshape(bsz, s, d)


def kernel(x, mix_norm, ffn_norm, cf_pw1_w, cf_pw1_b, cf_dw_w, cf_dw_b, cf_ln_g, cf_ln_b, cf_pw2_w, cf_pw2_b, ffn_w_gate, ffn_w_up, ffn_w_down, gdn_w_in, gdn_conv_w, gdn_a_log, gdn_dt_bias, gdn_o_norm, gdn_w_out, moe_router, moe_w_gate, moe_w_up, moe_w_down, final_norm):
    x, (w_in_bf16, w_out_bf16) = _conformer_layer(
        x, mix_norm[0], ffn_norm[0], cf_pw1_w[0], cf_pw1_b[0], cf_dw_w[0], cf_dw_b[0],
        cf_ln_g[0], cf_ln_b[0], cf_pw2_w[0], cf_pw2_b[0],
        ffn_w_gate[0], ffn_w_up[0], ffn_w_down[0], [gdn_w_in[0], gdn_w_out[0]])
    return _deltanet_moe_layer(x, mix_norm[1], ffn_norm[1], w_in_bf16, gdn_conv_w[0], gdn_a_log[0],
                               gdn_dt_bias[0], gdn_o_norm[0], w_out_bf16, moe_router[0],
                               moe_w_gate[0], moe_w_up[0], moe_w_down[0], final_norm)
```

```python
import functools

import jax
import jax.numpy as jnp
from jax import lax
from jax.experimental import pallas as pl
from jax.experimental.pallas import tpu as pltpu

F32 = jnp.float32
BF16 = jnp.bfloat16

RMS_EPS = 1e-6
LN_EPS = 1e-5
L2_EPS = 1e-6
N_HEADS = 8
HEAD_DIM = 128
CHUNK = 64
N_EXPERTS = 8
LANES = 128
BF16_SUBLANES = 16
CONV_HALO = 16
SHORT_HALO = 8
VMEM_LIMIT_BYTES = 56 * 1024 * 1024


def _params(*sem):
    return pltpu.CompilerParams(dimension_semantics=sem, vmem_limit_bytes=VMEM_LIMIT_BYTES)


def _const_spec(shape):
    nd = len(shape)
    return pl.BlockSpec(shape, lambda *_: (0,) * nd, pipeline_mode=pl.Buffered(1))


def _cast_streams(arrays, grid):
    nsteps = grid[0] * grid[1]
    in_specs, out_specs, out_shapes = [], [], []
    for a in arrays:
        rows, cols = a.shape
        blk = next(r for r in range(BF16_SUBLANES, rows + 1, BF16_SUBLANES)
                   if rows % r == 0 and rows // r <= nsteps)
        nblk = rows // blk

        def imap(b, i, nblk=nblk):
            return (jnp.minimum(b * grid[1] + i, nblk - 1), 0)

        in_specs.append(pl.BlockSpec((blk, cols), imap))
        out_specs.append(pl.BlockSpec((blk, cols), imap))
        out_shapes.append(jax.ShapeDtypeStruct((rows, cols), BF16))
    return in_specs, out_specs, out_shapes


def _with_casts(body, n_in, n_out, n_cast):
    def wrapped(*refs):
        a, b, c = n_in + n_cast, n_in + n_cast + n_out, n_in + 2 * n_cast + n_out
        for src, dst in zip(refs[n_in:a], refs[b:c]):
            dst[...] = src[...].astype(BF16)
        body(*refs[:n_in], *refs[a:b], *refs[c:])
    return wrapped


def _rms(x, g):
    return x * lax.rsqrt(jnp.mean(x * x, axis=-1, keepdims=True) + RMS_EPS) * g


def _silu(x):
    return x * jax.nn.sigmoid(x)


def _mm(a, b):
    return jnp.dot(a.astype(BF16), b.astype(BF16), preferred_element_type=F32)


def _mm_nt(a, b):
    return lax.dot_general(a.astype(BF16), b.astype(BF16), (((1,), (1,)), ((), ())),
                           preferred_element_type=F32)


def _mm_tn(a, b):
    return lax.dot_general(a.astype(BF16), b.astype(BF16), (((0,), (0,)), ((), ())),
                           preferred_element_type=F32)


def _mm_exact(a, b):
    return jnp.dot(a, b, preferred_element_type=F32, precision=lax.Precision.HIGHEST)


def _conf_mixer_body(x_ref, xp_ref, xn_ref, g_ref, w1_ref, b1_ref, dw_ref, dwb_ref, lng_ref, lnb_ref,
                     w2_ref, b2_ref, o_ref, xe_ref, ext_ref, cv_ref, *, nc):
    s = pl.program_id(1)
    ns = pl.num_programs(1)
    ts, d = cv_ref.shape
    width = dw_ref.shape[0]
    base = CONV_HALO - width // 2
    span = ts + 8 * ((base + width - 1) // 8)
    xe_ref[0:CONV_HALO, :] = xp_ref[0]
    xe_ref[CONV_HALO:CONV_HALO + ts, :] = x_ref[0]
    xe_ref[CONV_HALO + ts:, :] = xn_ref[0]
    h = _rms(xe_ref[...], g_ref[...]).astype(BF16)
    for c in range(d // nc):
        a = _mm(h, w1_ref[:, c * nc:(c + 1) * nc]) + b1_ref[:, c * nc:(c + 1) * nc]
        b = _mm(h, w1_ref[:, d + c * nc:d + (c + 1) * nc]) + b1_ref[:, d + c * nc:d + (c + 1) * nc]
        ext_ref[0, :, c * nc:(c + 1) * nc] = a * jax.nn.sigmoid(b)

    @pl.when(s == 0)
    def _():
        ext_ref[0, 0:CONV_HALO, :] = jnp.zeros((CONV_HALO, d), F32)

    @pl.when(s == ns - 1)
    def _():
        ext_ref[0, CONV_HALO + ts:, :] = jnp.zeros((CONV_HALO, d), F32)

    for p in range(1, 8):
        ext_ref[p, 0:span, :] = ext_ref[0, pl.ds(p, span), :]
    rb = 128
    for c in range(d // LANES):
        cs = slice(c * LANES, (c + 1) * LANES)
        for r in range(ts // rb):
            acc = jnp.zeros((rb, LANES), F32)
            for k in range(width):
                off = base + k
                acc = acc + dw_ref[k:k + 1, cs] * ext_ref[off % 8, pl.ds(r * rb + 8 * (off // 8), rb), cs]
            cv_ref[r * rb:(r + 1) * rb, cs] = acc + dwb_ref[:, cs]
    y = cv_ref[...]
    mu = jnp.mean(y, axis=-1, keepdims=True)
    yc = y - mu
    yn = yc * lax.rsqrt(jnp.mean(yc * yc, axis=-1, keepdims=True) + LN_EPS) * lng_ref[...] + lnb_ref[...]
    o_ref[0] = x_ref[0] + _mm(_silu(yn), w2_ref[...]) + b2_ref[...]


def _conf_mixer(x3, g, w1, b1, dw_w, dw_b, ln_g, ln_b, w2, b2, ts, casts):
    bsz, s, d = x3.shape
    width = dw_w.shape[0]
    hb = ts // CONV_HALO
    nhb = s // CONV_HALO
    grid = (bsz, s // ts)
    c_in, c_out, c_shape = _cast_streams(casts, grid)
    res = pl.pallas_call(
        _with_casts(functools.partial(_conf_mixer_body, nc=256), 12, 1, len(casts)),
        grid=grid,
        in_specs=[pl.BlockSpec((1, ts, d), lambda b, i: (b, i, 0)),
                  pl.BlockSpec((1, CONV_HALO, d), lambda b, i: (b, jnp.maximum(i * hb - 1, 0), 0)),
                  pl.BlockSpec((1, CONV_HALO, d), lambda b, i: (b, jnp.minimum((i + 1) * hb, nhb - 1), 0)),
                  _const_spec((1, d)), _const_spec((d, 2 * d)), _const_spec((1, 2 * d)),
                  _const_spec((width, d)), _const_spec((1, d)), _const_spec((1, d)), _const_spec((1, d)),
                  _const_spec((d, d)), _const_spec((1, d))] + c_in,
        out_specs=[pl.BlockSpec((1, ts, d), lambda b, i: (b, i, 0))] + c_out,
        out_shape=[jax.ShapeDtypeStruct((bsz, s, d), F32)] + c_shape,
        scratch_shapes=[pltpu.VMEM((ts + 2 * CONV_HALO, d), F32),
                        pltpu.VMEM((8, ts + 2 * CONV_HALO, d), F32), pltpu.VMEM((ts, d), F32)],
        compiler_params=_params("parallel", "parallel"),
        name="conf_mixer",
    )(x3, x3, x3, g, w1, b1, dw_w, dw_b, ln_g, ln_b, w2, b2, *casts)
    return res[0], res[1:]


def _ffn_body(x_ref, g_ref, wg_ref, wu_ref, wd_ref, o_ref, *, fc):
    x = x_ref[...]
    h = _rms(x, g_ref[...]).astype(BF16)
    dff = wg_ref.shape[1]
    acc = x
    for c in range(dff // fc):
        cs = slice(c * fc, (c + 1) * fc)
        a = _silu(_mm(h, wg_ref[:, cs])) * _mm(h, wu_ref[:, cs])
        acc = acc + _mm(a, wd_ref[cs, :])
    o_ref[...] = acc


def _ffn(x2d, g, wg, wu, wd, tm, fc):
    t, d = x2d.shape
    dff = wg.shape[1]
    return pl.pallas_call(
        functools.partial(_ffn_body, fc=fc),
        grid=(t // tm,),
        in_specs=[pl.BlockSpec((tm, d), lambda i: (i, 0)), _const_spec((1, d)),
                  _const_spec((d, dff)), _const_spec((d, dff)), _const_spec((dff, d))],
        out_specs=pl.BlockSpec((tm, d), lambda i: (i, 0)),
        out_shape=jax.ShapeDtypeStruct((t, d), F32),
        compiler_params=_params("parallel"),
        name="dense_swiglu",
    )(x2d, g, wg, wu, wd)


PAIR = 2 * CHUNK


def _block_diag(yb, mk_ref):
    return jnp.concatenate([yb * mk_ref[MASK_LEFT].astype(BF16), yb * mk_ref[MASK_RIGHT].astype(BF16)], axis=0)


def _tri_inverse_pairs(mats, mk_ref):
    def bd(vals):
        return [_block_diag(v.astype(BF16), mk_ref) for v in vals]

    l0 = [a * mk_ref[MASK_BLOCK8] for a in mats]
    l0b = [a.astype(BF16) for a in l0]
    l2 = [_mm(a, d) for a, d in zip(l0b, bd(l0b))]
    l2b = [a.astype(BF16) for a in l2]
    l2d = bd(l2b)
    l4 = [_mm(a, d) for a, d in zip(l2b, l2d)]
    l3 = [_mm(a, d) for a, d in zip(l0b, l2d)]
    xs = [mk_ref[MASK_EYE] - a + b - t for a, b, t in zip(l0, l2, l3)]
    x4 = [_mm(x, d) for x, d in zip(xs, bd(l4))]
    xs = [x + t for x, t in zip(xs, x4)]
    for level in range(MASK_MERGE0, MASK_MERGE0 + 3):
        xb = [x.astype(BF16) for x in xs]
        t1 = [_mm(a * mk_ref[level], d) for a, d in zip(mats, bd(xb))]
        t2 = [_mm(x, d) for x, d in zip(xb, bd(t1))]
        xs = [x - t for x, t in zip(xs, t2)]
    return xs


(MASK_INCL, MASK_INCL_L, MASK_INCL_R, MASK_STRICT_L, MASK_STRICT_R) = (0, 2, 4, 6, 8)
MASK_EYE, MASK_BLOCK8, MASK_MERGE0, MASK_LEFT, MASK_RIGHT = 10, 11, 12, 15, 16
N_MASKS = 17


def _chunk_masks():
    ri = lax.broadcasted_iota(jnp.int32, (CHUNK, PAIR), 0)
    li = lax.broadcasted_iota(jnp.int32, (CHUNK, PAIR), 1)
    ci = li % CHUNK
    left, right = li < CHUNK, li >= CHUNK
    incl = [ci <= ri, ci >= ri]
    strict = [ci < ri, ci > ri]
    masks = (incl + [m & left for m in incl] + [m & right for m in incl]
             + [m & left for m in strict] + [m & right for m in strict]
             + [ci == ri, (ri // 8) == (ci // 8)])
    size = 8
    while size < CHUNK:
        masks.append(((ri // (2 * size)) == (ci // (2 * size))) & ((ri // size) != (ci // size)))
        size *= 2
    masks += [left, right]
    assert len(masks) == N_MASKS
    return jnp.stack(masks).astype(F32)


GDN_HEAD_GROUP = 4


def _gdn_front_body(x_ref, xp_ref, xn_ref, g_ref, w_ref, wgate_ref, cw_ref, alog_ref, dtb_ref, mk_ref,
                    z_ref, w_out_ref, kq_ref, kd_ref, p_ref, egl_ref, xe_ref, pe_ref):
    s = pl.program_id(1)
    ns = pl.num_programs(1)
    tg, d = z_ref.shape[1:]
    kw = cw_ref.shape[0]
    dk = N_HEADS * HEAD_DIM
    npair = tg // PAIR
    gw = GDN_HEAD_GROUP * HEAD_DIM
    ngroup = N_HEADS // GDN_HEAD_GROUP
    base = SHORT_HALO - kw // 2

    xe_ref[0:SHORT_HALO, :] = jnp.where(s > 0, xp_ref[0], 0.0)
    xe_ref[SHORT_HALO:SHORT_HALO + tg, :] = x_ref[0]
    xe_ref[SHORT_HALO + tg:, :] = jnp.where(s < ns - 1, xn_ref[0], 0.0)
    h = _rms(xe_ref[...], g_ref[...]).astype(BF16)
    hm = h[SHORT_HALO:SHORT_HALO + tg]

    gates = _mm(hm, wgate_ref[...])
    beta = jax.nn.sigmoid(gates)
    xa = gates + dtb_ref[...]
    softplus = jnp.maximum(xa, 0.0) + jnp.log1p(jnp.exp(-jnp.abs(xa)))
    log_a = -jnp.exp(alog_ref[...]) * softplus
    ri = lax.broadcasted_iota(jnp.int32, (tg, tg), 0)
    ci = lax.broadcasted_iota(jnp.int32, (tg, tg), 1)
    same = (ri // CHUNK) == (ci // CHUNK)
    cum_f = _mm_exact((same & (ci <= ri)).astype(F32), log_a)
    cum_b = _mm_exact((same & (ci >= ri)).astype(F32), log_a)
    lane = lax.broadcasted_iota(jnp.int32, (tg, LANES), 1)
    g_all = jnp.where(lane < 3 * N_HEADS, cum_f, cum_b)
    g_all_t = g_all.T
    beta_t = beta.T

    def project(grp):
        out = []
        for part in range(3):
            col0 = part * dk + grp * gw
            pe_ref[grp % 2, part] = _mm(h, w_ref[:, col0:col0 + gw])
            heads = []
            for c in range(GDN_HEAD_GROUP):
                col = col0 + c * LANES
                acc = jnp.zeros((tg, LANES), F32)
                for k in range(kw):
                    acc = acc + (cw_ref[k:k + 1, col:col + LANES]
                                 * pe_ref[grp % 2, part, pl.ds(base + k, tg), c * LANES:(c + 1) * LANES])
                a = _silu(acc)
                if part < 2:
                    a = a * lax.rsqrt(jnp.sum(a * a, axis=-1, keepdims=True) + L2_EPS)
                if part == 0:
                    a = a * (HEAD_DIM ** -0.5)
                heads.append(a)
            out.append(heads)
        return out

    def chunk_systems(grp, qkv):
        heads = range(grp * GDN_HEAD_GROUP, (grp + 1) * GDN_HEAD_GROUP)
        qs, ks, vs = (dict(zip(heads, part)) for part in qkv)
        raw, v16 = {}, {}
        for hd in heads:
            k16 = ks[hd].astype(BF16)
            q16 = qs[hd].astype(BF16)
            v16[hd] = vs[hd].astype(BF16)
            for m in range(npair):
                pr = slice(m * PAIR, (m + 1) * PAIR)
                raw[hd, m] = _mm_nt(jnp.concatenate([k16[pr], q16[pr]], axis=0), k16[pr])
        insts = [(hd, dr, m) for hd in heads for dr in range(2) for m in range(npair)]
        zero = jnp.zeros((CHUNK, HEAD_DIM), BF16)
        lows, rhss, beta_rows = [], [], []
        for hd, dr, m in insts:
            pr = slice(m * PAIR, (m + 1) * PAIR)
            bcol = dr * N_HEADS + hd
            gcol = 2 * N_HEADS + dr * N_HEADS + hd
            bt = beta[pr, bcol:bcol + 1]
            beta_rows.append(beta_t[bcol:bcol + 1, pr])
            g_c = g_all[pr, gcol:gcol + 1]
            g_r = g_all_t[gcol:gcol + 1, pr]
            diff = g_c - g_r
            decay = jnp.exp(diff[:CHUNK] * mk_ref[MASK_INCL_L + dr] + diff[CHUNK:] * mk_ref[MASK_INCL_R + dr])
            decay = decay * mk_ref[MASK_INCL + dr]
            kk = raw[hd, m][:PAIR] * bt
            qk = raw[hd, m][PAIR:]
            lows.append((kk[:CHUNK] * mk_ref[MASK_STRICT_L + dr] + kk[CHUNK:] * mk_ref[MASK_STRICT_R + dr]) * decay)
            pq = (qk[:CHUNK] * mk_ref[MASK_LEFT] + qk[CHUNK:] * mk_ref[MASK_RIGHT]) * decay
            half = mk_ref[MASK_LEFT + dr]
            p_tiles = [(t * half).astype(BF16) for t in (pq, pltpu.roll(pq, CHUNK, axis=1))]
            eg = jnp.exp(g_c)
            keg = (ks[hd][pr] * eg).astype(BF16)
            qeg = (qs[hd][pr] * eg).astype(BF16)
            vp = v16[hd][pr]
            rhss.append(jnp.concatenate(
                [jnp.concatenate([vp[:CHUNK], keg[:CHUNK], zero, zero], axis=1),
                 jnp.concatenate([zero, zero, vp[CHUNK:], keg[CHUNK:]], axis=1)], axis=0))
            for j in range(2):
                n = 2 * m + j
                rs = slice(n * CHUNK, (n + 1) * CHUNK)
                js = slice(j * CHUNK, (j + 1) * CHUNK)
                last = n * CHUNK + (CHUNK - 1 if dr == 0 else 0)
                g_l = g_all[last:last + 1, gcol:gcol + 1]
                p_ref[dr, 0, hd, rs, :] = p_tiles[(j + dr) % 2]
                kq_ref[dr, 0, hd, n, CHUNK:, :] = qeg[js]
                kd_ref[dr, 0, hd, rs, :] = (ks[hd][rs] * jnp.exp(g_l - g_all[rs, gcol:gcol + 1])).astype(BF16)
                egl_ref[dr, 0, hd, n] = jnp.broadcast_to(jnp.exp(g_l), (1, HEAD_DIM))
        invs = _tri_inverse_pairs(lows, mk_ref)
        sols = [_mm(inv * br, rhs) for inv, br, rhs in zip(invs, beta_rows, rhss)]
        for (hd, dr, m), sol in zip(insts, sols):
            for j in range(2):
                n = 2 * m + j
                rs = slice(n * CHUNK, (n + 1) * CHUNK)
                w_out_ref[dr, 0, hd, rs, :] = sol[:, 2 * j * HEAD_DIM:(2 * j + 1) * HEAD_DIM]
                kq_ref[dr, 0, hd, n, :CHUNK, :] = sol[:, (2 * j + 1) * HEAD_DIM:(2 * j + 2) * HEAD_DIM].astype(BF16)

    qkv = project(0)
    for grp in range(ngroup):
        nxt = project(grp + 1) if grp + 1 < ngroup else None
        if nxt is None:
            z_ref[0] = _mm(hm, w_ref[:, 3 * dk:3 * dk + d])
        chunk_systems(grp, qkv)
        qkv = nxt


def _gdn_front(x3, g, w_in, w_gate, conv_w, alog_row, dtb_row, tg, casts):
    bsz, s, d = x3.shape
    nin = w_in.shape[1]
    kw, nq = conv_w.shape
    nchunk = tg // CHUNK
    hb = tg // SHORT_HALO
    nhb = s // SHORT_HALO
    gw = GDN_HEAD_GROUP * HEAD_DIM

    def rows(dtype, last=HEAD_DIM):
        return (jax.ShapeDtypeStruct((2, bsz, N_HEADS, s, last), dtype),
                pl.BlockSpec((2, 1, N_HEADS, tg, last), lambda b, i: (0, b, 0, i, 0)))

    def per_chunk(dtype, r):
        return (jax.ShapeDtypeStruct((2, bsz, N_HEADS, s // CHUNK, r, HEAD_DIM), dtype),
                pl.BlockSpec((2, 1, N_HEADS, nchunk, r, HEAD_DIM), lambda b, i: (0, b, 0, i, 0, 0)))

    outs = [(jax.ShapeDtypeStruct((bsz, s, d), F32), pl.BlockSpec((1, tg, d), lambda b, i: (b, i, 0))),
            rows(F32), per_chunk(BF16, 2 * CHUNK), rows(BF16), rows(BF16), per_chunk(F32, 1)]
    grid = (bsz, s // tg)
    c_in, c_out, c_shape = _cast_streams(casts, grid)
    res = pl.pallas_call(
        _with_casts(_gdn_front_body, 10, len(outs), len(casts)),
        grid=grid,
        in_specs=[pl.BlockSpec((1, tg, d), lambda b, i: (b, i, 0)),
                  pl.BlockSpec((1, SHORT_HALO, d), lambda b, i: (b, jnp.maximum(i * hb - 1, 0), 0)),
                  pl.BlockSpec((1, SHORT_HALO, d), lambda b, i: (b, jnp.minimum((i + 1) * hb, nhb - 1), 0)),
                  _const_spec((1, d)), _const_spec((d, nin)), _const_spec((d, LANES)),
                  _const_spec((kw, nq)), _const_spec((1, LANES)), _const_spec((1, LANES)),
                  _const_spec((N_MASKS, CHUNK, PAIR))] + c_in,
        out_specs=[o[1] for o in outs] + c_out,
        out_shape=[o[0] for o in outs] + c_shape,
        scratch_shapes=[pltpu.VMEM((tg + 2 * SHORT_HALO, d), F32),
                        pltpu.VMEM((2, 3, tg + 2 * SHORT_HALO, gw), F32)],
        compiler_params=_params("parallel", "parallel"),
        name="gdn_front",
    )(x3, x3, x3, g, w_in, w_gate, conv_w, alog_row, dtb_row, _chunk_masks(), *casts)
    return res[:len(outs)], res[len(outs):]


GDN_SCAN_GROUP = 16


def _gdn_scan_body(wf, kqf, kdf, pf, eglf, wb, kqb, kdb, pb, eglb, of_ref, ob_ref, state_ref):
    n = pl.program_id(0)

    @pl.when(n == 0)
    def _():
        state_ref[...] = jnp.zeros_like(state_ref)

    nbh = state_ref.shape[1]
    dirs = ((wf, kqf, kdf, pf, eglf, of_ref), (wb, kqb, kdb, pb, eglb, ob_ref))

    def body(i, carry):
        bhs = [i * GDN_SCAN_GROUP + j for j in range(GDN_SCAN_GROUP)]
        zero = jnp.zeros((CHUNK, HEAD_DIM), BF16)
        st = [[state_ref[dr, b] for dr in range(2)] for b in bhs]
        r = [[_mm(dirs[dr][1][0, b, 0], s_[dr]) for dr in range(2)] for b, s_ in zip(bhs, st)]
        ub = [[(dirs[dr][0][0, b] - r_[dr][:CHUNK]).astype(BF16) for dr in range(2)] for b, r_ in zip(bhs, r)]
        ud = [jnp.concatenate([jnp.concatenate([u_[0], zero], axis=1), jnp.concatenate([zero, u_[1]], axis=1)],
                              axis=0) for u_ in ub]
        pu = [_mm(dirs[0][3][0, b] + dirs[1][3][0, b], d_) for b, d_ in zip(bhs, ud)]
        ku = [_mm_tn(jnp.concatenate([dirs[0][2][0, b], dirs[1][2][0, b]], axis=0), d_) for b, d_ in zip(bhs, ud)]
        for b, r_, pu_, ku_, s_ in zip(bhs, r, pu, ku, st):
            for dr in range(2):
                cs = slice(dr * HEAD_DIM, (dr + 1) * HEAD_DIM)
                dirs[dr][5][b] = r_[dr][CHUNK:] + pu_[:, cs]
                state_ref[dr, b] = s_[dr] * dirs[dr][4][0, b, 0] + ku_[:, cs]
        return carry

    lax.fori_loop(0, nbh // GDN_SCAN_GROUP, body, 0)


def _gdn_scan(w, kq, kd, p, egl):
    _, nbh, s, dh = w.shape
    nchunk = s // CHUNK

    def specs(dr):
        def im(n):
            return (dr, 0, n if dr == 0 else nchunk - 1 - n, 0)

        def im5(n):
            return im(n) + (0,)
        return [pl.BlockSpec((1, nbh, CHUNK, dh), im), pl.BlockSpec((1, nbh, 1, 2 * CHUNK, dh), im5),
                pl.BlockSpec((1, nbh, CHUNK, dh), im), pl.BlockSpec((1, nbh, CHUNK, dh), im),
                pl.BlockSpec((1, nbh, 1, 1, dh), im5)]

    o_shape = jax.ShapeDtypeStruct((nbh, s, dh), F32)
    return pl.pallas_call(
        _gdn_scan_body,
        grid=(nchunk,),
        in_specs=specs(0) + specs(1),
        out_specs=[pl.BlockSpec((nbh, CHUNK, dh), lambda n: (0, n, 0)),
                   pl.BlockSpec((nbh, CHUNK, dh), lambda n: (0, nchunk - 1 - n, 0))],
        out_shape=[o_shape, o_shape],
        scratch_shapes=[pltpu.VMEM((2, nbh, dh, dh), F32)],
        compiler_params=_params("arbitrary"),
        name="gdn_scan",
    )(w, kq, kd, p, egl, w, kq, kd, p, egl)


TOKEN_TILE_ROWS = 8
DMA_ISSUE_UNROLL = 8


def _to_token_tiles(dst_ref, val):
    n = val.shape[0]
    for j in range(TOKEN_TILE_ROWS):
        dst_ref[pl.ds(j, n, stride=TOKEN_TILE_ROWS), :] = val[:, j * LANES:(j + 1) * LANES]


def _from_token_tiles(src_ref, n):
    return jnp.concatenate([src_ref[pl.ds(j, n, stride=TOKEN_TILE_ROWS), :] for j in range(TOKEN_TILE_ROWS)],
                           axis=1)


def _gdn_out_body(of_ref, ob_ref, z_ref, x_ref, on_ref, wo_ref, fg_ref, rwt_ref,
                  xo_ref, pos_ref, wcol_ref, cnt_ref, xs_ref,
                  act_ref, tile_ref, zero_ref, carry_ref, pos_vmem, pos_smem, cnt_smem, sem, psem, zsem,
                  *, cap, tm):
    g = pl.program_id(0)
    ng = pl.num_programs(0)
    ts = x_ref.shape[0]
    slot = g % 2
    prev = 1 - slot
    rpt = TOKEN_TILE_ROWS

    def wait_rows(s):
        for _ in range(2):
            pltpu.make_async_copy(tile_ref.at[s], xs_ref.at[pl.ds(0, ts * rpt)], sem.at[s]).wait()

    def dispatch_row(t, s):
        src = tile_ref.at[s, pl.ds(t * rpt, rpt)]
        for k in range(2):
            dst = xs_ref.at[pl.ds(pos_smem[s, k, t] * rpt, rpt)]
            pltpu.make_async_copy(src, dst, sem.at[s]).start(priority=k)

    def positions_to_smem(s):
        return pltpu.make_async_copy(pos_vmem.at[s], pos_smem.at[s], psem)

    def tile_step(dispatch_prev):
        if dispatch_prev:
            positions_to_smem(prev).wait()
            for t in range(ts):
                dispatch_row(t, prev)
        for h in range(N_HEADS):
            hs = slice(h * HEAD_DIM, (h + 1) * HEAD_DIM)
            o = of_ref[0, h] + ob_ref[0, h]
            o = o * lax.rsqrt(jnp.mean(o * o, axis=-1, keepdims=True) + RMS_EPS) * on_ref[...]
            act_ref[:, hs] = (o * _silu(z_ref[:, hs])).astype(BF16)
        x = x_ref[...] + _mm(act_ref[...], wo_ref[...])
        xo_ref[...] = x
        hn = _rms(x, fg_ref[...])

        logits = lax.dot_general(rwt_ref[...], hn, (((1,), (1,)), ((), ())),
                                 preferred_element_type=F32, precision=lax.Precision.HIGHEST)
        eidx = lax.broadcasted_iota(jnp.int32, logits.shape, 0).astype(F32)
        neg = jnp.float32(-jnp.inf)
        m1 = jnp.max(logits, axis=0, keepdims=True)
        i1 = jnp.min(jnp.where(logits == m1, eidx, float(N_EXPERTS)), axis=0, keepdims=True)
        one1 = eidx == i1
        rest = jnp.where(one1, neg, logits)
        m2 = jnp.max(rest, axis=0, keepdims=True)
        i2 = jnp.min(jnp.where(rest == m2, eidx, float(N_EXPERTS)), axis=0, keepdims=True)
        one2 = eidx == i2
        e2 = jnp.exp(m2 - m1)
        w1 = 1.0 / (1.0 + e2)
        w2 = e2 * w1

        chosen = jnp.where(one1 | one2, 1.0, 0.0)
        ri = lax.broadcasted_iota(jnp.int32, (ts, ts), 0)
        ci = lax.broadcasted_iota(jnp.int32, (ts, ts), 1)
        before = jnp.where(ri < ci, 1.0, 0.0).astype(BF16)
        rank = jnp.dot(chosen.astype(BF16), before, preferred_element_type=F32)
        carry = carry_ref[...]
        slot_f = eidx * float(cap) + carry[:, 0:1] + rank
        p1 = jnp.sum(jnp.where(one1, slot_f, 0.0), axis=0, keepdims=True)
        p2 = jnp.sum(jnp.where(one2, slot_f, 0.0), axis=0, keepdims=True)
        carry = carry + jnp.sum(chosen, axis=1, keepdims=True)
        carry_ref[...] = carry
        cnt_ref[...] = carry.astype(jnp.int32)
        row8 = lax.broadcasted_iota(jnp.int32, (N_EXPERTS, ts), 0)
        pos = jnp.where(row8 == 0, p1, jnp.where(row8 == 1, p2, 0.0)).astype(jnp.int32)
        pos_ref[...] = pos
        pos_vmem[slot] = pos
        row128 = lax.broadcasted_iota(jnp.int32, (LANES, ts), 0)
        wcol_ref[...] = jnp.where(row128 == 0, w1, jnp.where(row128 == 1, w2, 0.0)).T

        _to_token_tiles(tile_ref.at[slot], hn)
        if dispatch_prev:
            wait_rows(prev)
        positions_to_smem(slot).start()

    @pl.when(g == 0)
    def _():
        carry_ref[...] = jnp.zeros_like(carry_ref)
        tile_step(False)

    @pl.when(g > 0)
    def _():
        tile_step(True)

    @pl.when(g == ng - 1)
    def _():
        positions_to_smem(slot).wait()
        lax.fori_loop(0, ts, lambda t, c: (dispatch_row(t, slot), c)[1], 0, unroll=DMA_ISSUE_UNROLL)
        wait_rows(slot)
        zero_ref[...] = jnp.zeros_like(zero_ref)
        cc = pltpu.make_async_copy(cnt_ref, cnt_smem, psem)
        cc.start()
        cc.wait()
        tails = [pltpu.make_async_copy(
            zero_ref, xs_ref.at[pl.ds((e * cap + cnt_smem[e, 0]) * rpt, tm * rpt)], zsem)
            for e in range(N_EXPERTS)]
        for c in tails:
            c.start()
        for c in tails:
            c.wait()


def _gdn_out(o_f, o_b, z2, x2, o_norm, w_out, ffn_g, router_t, ts, tm, cap):
    t, d = x2.shape
    s = o_f.shape[2]
    spb = s // ts
    rpt = TOKEN_TILE_ROWS
    o_spec = pl.BlockSpec((1, N_HEADS, ts, HEAD_DIM), lambda g: (g // spb, 0, g % spb, 0))
    return pl.pallas_call(
        functools.partial(_gdn_out_body, cap=cap, tm=tm),
        grid=(t // ts,),
        in_specs=[o_spec, o_spec,
                  pl.BlockSpec((ts, d), lambda g: (g, 0)), pl.BlockSpec((ts, d), lambda g: (g, 0)),
                  _const_spec((1, HEAD_DIM)), _const_spec((d, d)), _const_spec((1, d)),
                  _const_spec((N_EXPERTS, d))],
        out_specs=[pl.BlockSpec((ts, d), lambda g: (g, 0)),
                   pl.BlockSpec((N_EXPERTS, ts), lambda g: (0, g)),
                   pl.BlockSpec((ts, LANES), lambda g: (g, 0)),
                   pl.BlockSpec((N_EXPERTS, LANES), lambda g: (0, 0)),
                   pl.BlockSpec(memory_space=pl.ANY)],
        out_shape=[jax.ShapeDtypeStruct((t, d), F32),
                   jax.ShapeDtypeStruct((N_EXPERTS, t), jnp.int32),
                   jax.ShapeDtypeStruct((t, LANES), F32),
                   jax.ShapeDtypeStruct((N_EXPERTS, LANES), jnp.int32),
                   jax.ShapeDtypeStruct((N_EXPERTS * cap * rpt, LANES), F32)],
        scratch_shapes=[pltpu.VMEM((ts, d), BF16),
                        pltpu.VMEM((2, ts * rpt, LANES), F32),
                        pltpu.VMEM((tm * rpt, LANES), F32),
                        pltpu.VMEM((N_EXPERTS, LANES), F32),
                        pltpu.VMEM((2, N_EXPERTS, ts), jnp.int32),
                        pltpu.SMEM((2, N_EXPERTS, ts), jnp.int32),
                        pltpu.SMEM((N_EXPERTS, LANES), jnp.int32),
                        pltpu.SemaphoreType.DMA((2,)), pltpu.SemaphoreType.DMA, pltpu.SemaphoreType.DMA],
        compiler_params=_params("arbitrary"),
        name="gdn_out_router",
    )(o_f, o_b, z2, x2, o_norm, w_out, ffn_g, router_t)


def _moe_body(te_ref, tb_ref, nu_ref, xs_ref, wg_ref, wu_ref, wd_ref, ys_ref, xb_ref, acc_ref, *, fc):
    i = pl.program_id(0)
    f = pl.program_id(1)
    tm = xb_ref.shape[0]
    tf = wg_ref.shape[2]

    @pl.when((i == 0) & (f == 0))
    def _():
        acc_ref[...] = jnp.zeros_like(acc_ref)

    @pl.when(i < nu_ref[0])
    def _():
        @pl.when(f == 0)
        def _():
            xb_ref[...] = _from_token_tiles(xs_ref, tm).astype(BF16)

        x = xb_ref[...]
        for c in range(tf // fc):
            cs = slice(c * fc, (c + 1) * fc)
            a = _silu(_mm(x, wg_ref[0, :, cs])) * _mm(x, wu_ref[0, :, cs])
            y = _mm(a, wd_ref[0, cs, :])
            if c == 0:
                acc_ref[...] = jnp.where(f > 0, acc_ref[...], 0.0) + y
            else:
                acc_ref[...] += y

        @pl.when(f == pl.num_programs(1) - 1)
        def _():
            _to_token_tiles(ys_ref, acc_ref[...])


def _moe_grouped(xs, tile_expert, tile_block, n_used, wg, wu, wd, tm, tf, fc):
    ne, d, dff = wg.shape
    nt = tile_expert.shape[0]
    nf = dff // tf
    rpt = TOKEN_TILE_ROWS

    def fsel(i, f, nu):
        return jnp.where(i < nu[0], f, nf - 1)

    grid_spec = pltpu.PrefetchScalarGridSpec(
        num_scalar_prefetch=3,
        grid=(nt, nf),
        in_specs=[pl.BlockSpec((tm * rpt, LANES), lambda i, f, te, tb, nu: (tb[i], 0)),
                  pl.BlockSpec((1, d, tf), lambda i, f, te, tb, nu: (te[i], 0, fsel(i, f, nu))),
                  pl.BlockSpec((1, d, tf), lambda i, f, te, tb, nu: (te[i], 0, fsel(i, f, nu))),
                  pl.BlockSpec((1, tf, d), lambda i, f, te, tb, nu: (te[i], fsel(i, f, nu), 0))],
        out_specs=pl.BlockSpec((tm * rpt, LANES), lambda i, f, te, tb, nu: (tb[i], 0)),
        scratch_shapes=[pltpu.VMEM((tm, d), BF16), pltpu.VMEM((tm, d), F32)],
    )
    return pl.pallas_call(
        functools.partial(_moe_body, fc=fc),
        grid_spec=grid_spec,
        out_shape=jax.ShapeDtypeStruct(xs.shape, F32),
        compiler_params=_params("arbitrary", "arbitrary"),
        name="moe_grouped",
    )(tile_expert, tile_block, n_used, xs, wg, wu, wd)


COMBINE_DEPTH = 3


def _combine_body(pos0_ref, pos1_ref, pos2_ref, x_ref, wcol_ref, fn_ref, ys_ref, o_ref, gbuf, pos_smem, sem, psem):
    i = pl.program_id(0)
    n = pl.num_programs(0)
    tc = x_ref.shape[0]
    rpt = TOKEN_TILE_ROWS

    def load_positions(p_ref):
        cp = pltpu.make_async_copy(p_ref, pos_smem, psem)
        cp.start()
        cp.wait()

    def issue(t, sl):
        for k in range(2):
            src = ys_ref.at[pl.ds(pos_smem[k, t] * rpt, rpt)]
            pltpu.make_async_copy(src, gbuf.at[sl, k, pl.ds(t * rpt, rpt)], sem.at[sl]).start(priority=k)

    def wait_rows(sl):
        for k in range(2):
            pltpu.make_async_copy(ys_ref.at[pl.ds(0, tc * rpt)], gbuf.at[sl, k], sem.at[sl]).wait()

    @pl.when(i == 0)
    def _():
        for sl, p_ref in enumerate((pos0_ref, pos1_ref)):
            load_positions(p_ref)
            lax.fori_loop(0, tc, lambda t, c, sl=sl: (issue(t, sl), c)[1], 0, unroll=DMA_ISSUE_UNROLL)

    for sl in range(COMBINE_DEPTH):
        @pl.when(i % COMBINE_DEPTH == sl)
        def _(sl=sl):
            load_positions(pos2_ref)
            wait_rows(sl)
            for t in range(tc):
                issue(t, (sl + 2) % COMBINE_DEPTH)
            wcol = wcol_ref[...]
            y = (x_ref[...] + wcol[:, 0:1] * _from_token_tiles(gbuf.at[sl, 0], tc)
                 + wcol[:, 1:2] * _from_token_tiles(gbuf.at[sl, 1], tc))
            o_ref[...] = _rms(y, fn_ref[...])

            @pl.when(i == n - 1)
            def _():
                wait_rows((sl + 1) % COMBINE_DEPTH)
                wait_rows((sl + 2) % COMBINE_DEPTH)


def _moe_combine(pos, x2, wcol, final_g, ys, tc):
    t, d = x2.shape
    n = t // tc
    rpt = TOKEN_TILE_ROWS

    def pos_spec(ahead):
        return pl.BlockSpec((N_EXPERTS, tc), lambda i: (0, jnp.minimum(i + ahead, n - 1)))

    return pl.pallas_call(
        _combine_body,
        grid=(n,),
        in_specs=[pos_spec(0), pos_spec(1), pos_spec(2),
                  pl.BlockSpec((tc, d), lambda i: (i, 0)),
                  pl.BlockSpec((tc, LANES), lambda i: (i, 0)),
                  _const_spec((1, d)),
                  pl.BlockSpec(memory_space=pl.ANY)],
        out_specs=pl.BlockSpec((tc, d), lambda i: (i, 0)),
        out_shape=jax.ShapeDtypeStruct((t, d), F32),
        scratch_shapes=[pltpu.VMEM((COMBINE_DEPTH, 2, tc * rpt, LANES), F32),
                        pltpu.SMEM((N_EXPERTS, tc), jnp.int32),
                        pltpu.SemaphoreType.DMA((COMBINE_DEPTH,)), pltpu.SemaphoreType.DMA],
        compiler_params=_params("arbitrary"),
        name="moe_combine_norm",
    )(pos, pos, pos, x2, wcol, final_g, ys)


def _tile_schedule(counts, tm, cap, nt):
    ntile = (counts + tm - 1) // tm
    ends = jnp.cumsum(ntile)
    n_used = ends[-1]
    i = jnp.minimum(jnp.arange(nt, dtype=jnp.int32), jnp.maximum(n_used - 1, 0))
    te = jnp.sum((i[:, None] >= ends[None, :]).astype(jnp.int32), axis=1)
    tb = te * (cap // tm) + i - (ends - ntile)[te]
    return te.astype(jnp.int32), tb.astype(jnp.int32), n_used.reshape(1).astype(jnp.int32)


def _row(v):
    return v.reshape(1, -1).astype(F32)


def _pad_cols(w, n):
    return jnp.pad(w, ((0, 0), (0, n - w.shape[1])))


def _conformer_layer(x3, mix_g, ffn_g, pw1_w, pw1_b, dw_w, dw_b, ln_g, ln_b, pw2_w, pw2_b,
                     w_gate, w_up, w_down, later_weights):
    bsz, s, d = x3.shape
    t = bsz * s
    tm = min(512, t)
    ts = min(512, s)
    x3, cast = _conf_mixer(x3, _row(mix_g), pw1_w.astype(BF16), _row(pw1_b), dw_w, _row(dw_b), _row(ln_g),
                           _row(ln_b), pw2_w.astype(BF16), _row(pw2_b), ts, [w_gate, w_up, w_down] + later_weights)
    x2 = _ffn(x3.reshape(t, d), _row(ffn_g), cast[0], cast[1], cast[2], tm, 256)
    return x2.reshape(bsz, s, d), cast[3:]


def _deltanet_moe_layer(x3, mix_g, ffn_g, w_in, w_in_bf16, conv_w, a_log, dt_bias, o_norm, w_out_bf16,
                        router, e_gate, e_up, e_down, final_g):
    bsz, s, d = x3.shape
    t = bsz * s
    tm = min(512, t)
    ts = min(512, s)
    nmain = 4 * d
    ne, _, dffe = e_gate.shape
    zero16 = jnp.zeros((2 * N_HEADS,), F32)
    alog_row = _row(_pad_cols(jnp.concatenate([zero16, a_log.reshape(-1)])[None], LANES))
    dtb_row = _row(_pad_cols(jnp.concatenate([zero16, dt_bias.reshape(-1)])[None], LANES))
    (z, w, kq, kd, p, egl), experts = _gdn_front(
        x3, _row(mix_g), w_in_bf16, _pad_cols(w_in[:, nmain:], LANES).astype(BF16), conv_w,
        alog_row, dtb_row, min(256, s),
        [e_gate.reshape(ne * d, dffe), e_up.reshape(ne * d, dffe), e_down.reshape(ne * dffe, d)])
    z = z.reshape(t, d)
    nbh = bsz * N_HEADS
    nck = s // CHUNK
    o_f, o_b = _gdn_scan(w.reshape(2, nbh, s, HEAD_DIM), kq.reshape(2, nbh, nck, 2 * CHUNK, HEAD_DIM),
                         kd.reshape(2, nbh, s, HEAD_DIM), p.reshape(2, nbh, s, HEAD_DIM),
                         egl.reshape(2, nbh, nck, 1, HEAD_DIM))
    cap = t + tm
    x2, pos, wcol, cnt, xs = _gdn_out(o_f.reshape(bsz, N_HEADS, s, HEAD_DIM), o_b.reshape(bsz, N_HEADS, s, HEAD_DIM),
                                      z, x3.reshape(t, d), _row(o_norm), w_out_bf16, _row(ffn_g),
                                      router.T, ts, tm, cap)
    te, tb, n_used = _tile_schedule(cnt[:, 0], tm, cap, 2 * t // tm + N_EXPERTS)
    tf = dffe // 2 if (dffe // 2) % 256 == 0 else dffe
    ys = _moe_grouped(xs, te, tb, n_used, experts[0].reshape(ne, d, dffe), experts[1].reshape(ne, d, dffe),
                      experts[2].reshape(ne, dffe, d), tm, tf, 256)
    out = _moe_combine(pos, x2, wcol, _row(final_g), ys, ts)
    return out.reshape(bsz, s, d)


def kernel(x, mix_norm, ffn_norm, cf_pw1_w, cf_pw1_b, cf_dw_w, cf_dw_b, cf_ln_g, cf_ln_b, cf_pw2_w, cf_pw2_b, ffn_w_gate, ffn_w_up, ffn_w_down, gdn_w_in, gdn_conv_w, gdn_a_log, gdn_dt_bias, gdn_o_norm, gdn_w_out, moe_router, moe_w_gate, moe_w_up, moe_w_down, final_norm):
    x, (w_in_bf16, w_out_bf16) = _conformer_layer(
        x, mix_norm[0], ffn_norm[0], cf_pw1_w[0], cf_pw1_b[0], cf_dw_w[0], cf_dw_b[0],
        cf_ln_g[0], cf_ln_b[0], cf_pw2_w[0], cf_pw2_b[0],
        ffn_w_gate[0], ffn_w_up[0], ffn_w_down[0], [gdn_w_in[0], gdn_w_out[0]])
    return _deltanet_moe_layer(x, mix_norm[1], ffn_norm[1], gdn_w_in[0], w_in_bf16, gdn_conv_w[0], gdn_a_log[0],
                               gdn_dt_bias[0], gdn_o_norm[0], w_out_bf16, moe_router[0],
                               moe_w_gate[0], moe_w_up[0], moe_w_down[0], final_norm)
```
